```python
import math
import jax
import jax.numpy as jnp
from jax import lax
import numpy as np

D_MODEL = 2048
BATCH = 16
SEQ = 256
DEPTH = 4
DEC_BATCH = 4
DEC_SEQ = 1024
PAST_LEN = 512

GRID_W = 64
N_MIXERS = 4
N_ATTN_LAYERS = (DEPTH + 3) // 4
N_MLSTM_LAYERS = (DEPTH + 2) // 4
N_SSD_LAYERS = (DEPTH + 1) // 4
N_HYENA_LAYERS = DEPTH // 4
NORM_EPS = 1e-6
SHORT_CONV = 3

ATTN_HEAD_DIM = 64
ATTN_HEADS = D_MODEL // (2 * ATTN_HEAD_DIM)
ROPE_BASE = 10000.0
Q_BLOCK = 128

MLSTM_HEADS = 8
MLSTM_QK_DIM = D_MODEL // (2 * MLSTM_HEADS)
MLSTM_V_DIM = D_MODEL // MLSTM_HEADS
MLSTM_QK_WIDTH = MLSTM_HEADS * MLSTM_QK_DIM
MLSTM_V_WIDTH = MLSTM_HEADS * MLSTM_V_DIM
MLSTM_CHUNK = 64

SSD_INNER = 2 * D_MODEL
SSD_HEAD_DIM = 64
SSD_HEADS = SSD_INNER // SSD_HEAD_DIM
SSD_GROUPS = 8
SSD_STATE = 128
SSD_GN = SSD_GROUPS * SSD_STATE
SSD_CHUNK = 64

HYENA_ORDER = 2
HYENA_EMB = 33
HYENA_FILTER_WIDTH = 64
HYENA_DECAY_SHORT = 0.3
HYENA_DECAY_LONG = 1.5
HYENA_TARGET = 1e-2

MOE_GROUPS = 4
MOE_EXPERTS_PER_GROUP = 8
MOE_EXPERTS = MOE_GROUPS * MOE_EXPERTS_PER_GROUP
MOE_TOP_K = 2
MOE_FF = 768
MOE_BLOCK = 128

kernel_name = 'hybrid_flow_backbone_step'


def rms_norm(x, w):
    xf = x.astype(jnp.float32)
    y = xf * lax.rsqrt(jnp.mean(jnp.square(xf), axis=-1, keepdims=True) + NORM_EPS)
    return (y * w.astype(jnp.float32)).astype(x.dtype)


def short_conv(x, w, b):
    width, ch = w.shape
    y = lax.conv_general_dilated(x, w[:, None, :].astype(x.dtype), window_strides=(1,),
                                 padding=[(width // 2, width // 2)],
                                 dimension_numbers=('NWC', 'WIO', 'NWC'), feature_group_count=ch)
    return y + b.astype(x.dtype)


def adaln(cond, w, b):
    mod = jax.nn.silu(cond) @ w + b
    return jnp.split(mod[:, None, :], 6, axis=-1)


def axial_rope_tables(length):
    rows = length // GRID_W
    row = jnp.broadcast_to(jnp.arange(rows, dtype=jnp.float32)[:, None], (rows, GRID_W)).reshape(length)
    col = jnp.broadcast_to(jnp.arange(GRID_W, dtype=jnp.float32)[None, :], (rows, GRID_W)).reshape(length)
    axis_dim = ATTN_HEAD_DIM // 2
    inv_freq = ROPE_BASE ** (-jnp.arange(0, axis_dim, 2, dtype=jnp.float32) / axis_dim)
    ang_r = row[:, None] * inv_freq
    ang_c = col[:, None] * inv_freq
    return jnp.cos(ang_r), jnp.sin(ang_r), jnp.cos(ang_c), jnp.sin(ang_c)


def rope_1d(x, cos, sin):
    x1, x2 = jnp.split(x, 2, axis=-1)
    cos = cos[:, None, None, :].astype(x.dtype)
    sin = sin[:, None, None, :].astype(x.dtype)
    return jnp.concatenate([x1 * cos - x2 * sin, x2 * cos + x1 * sin], axis=-1)


def axial_rope(x, tables):
    cos_r, sin_r, cos_c, sin_c = tables
    x_row, x_col = jnp.split(x, 2, axis=-1)
    return jnp.concatenate([rope_1d(x_row, cos_r, sin_r), rope_1d(x_col, cos_c, sin_c)], axis=-1)


def attn_qkv(h, w_in, q_norm, k_norm):
    bsz, length, _ = h.shape
    q, k, v = jnp.split(h @ w_in, 3, axis=-1)
    shp = (bsz, length, ATTN_HEADS, 2, ATTN_HEAD_DIM)
    return (rms_norm(q.reshape(shp), q_norm), rms_norm(k.reshape(shp), k_norm),
            v.reshape(bsz, length, ATTN_HEADS, 2 * ATTN_HEAD_DIM))


def diff_lambda(lam_vecs, lam_init):
    lv = lam_vecs.astype(jnp.float32)
    return jnp.exp(jnp.sum(lv[0] * lv[1])) - jnp.exp(jnp.sum(lv[2] * lv[3])) + lam_init


def diff_attention(q, k, v, lam):
    bsz, lq = q.shape[:2]
    n_blk = lq // Q_BLOCK
    q_blocks = jnp.swapaxes(q.reshape((bsz, n_blk, Q_BLOCK) + q.shape[2:]), 0, 1)
    scale = ATTN_HEAD_DIM ** -0.5

    def one_block(qb):
        s = jnp.einsum('bqhcd,bkhcd->bhcqk', qb, k).astype(jnp.float32) * scale
        p = jax.nn.softmax(s, axis=-1)
        a = p[:, :, 0] - lam * p[:, :, 1]
        return jnp.einsum('bhqk,bkhe->bqhe', a.astype(v.dtype), v)

    o = lax.map(one_block, q_blocks)
    return jnp.swapaxes(o, 0, 1).reshape(bsz, lq, ATTN_HEADS, 2 * ATTN_HEAD_DIM)


def attn_out(o, sub_norm, w_out, lam_init):
    bsz, length = o.shape[:2]
    o = rms_norm(o, sub_norm) * (1.0 - lam_init)
    return o.reshape(bsz, length, D_MODEL) @ w_out


def attn_context(h, w_in, q_norm, k_norm, lam_vecs, sub_norm, w_out, lam_init):
    q, k, v = attn_qkv(h, w_in, q_norm, k_norm)
    o = diff_attention(q, k, v, diff_lambda(lam_vecs, lam_init))
    return attn_out(o, sub_norm, w_out, lam_init), (k, v)


def attn_latent(h, ctx_k, ctx_v, w_in, q_norm, k_norm, lam_vecs, sub_norm, w_out, lam_init):
    q, k, v = attn_qkv(h, w_in, q_norm, k_norm)
    tables = axial_rope_tables(h.shape[1])
    q = axial_rope(q, tables)
    k = axial_rope(k, tables)
    k_all = jnp.concatenate([ctx_k.astype(k.dtype), k], axis=1)
    v_all = jnp.concatenate([ctx_v.astype(v.dtype), v], axis=1)
    o = diff_attention(q, k_all, v_all, diff_lambda(lam_vecs, lam_init))
    return attn_out(o, sub_norm, w_out, lam_init), ()


def mlstm_chunk_scan(q, k, v, i_pre, log_f, c0, n0, m0):
    bsz, length, nh, _ = q.shape
    dv = v.shape[-1]
    t_c = MLSTM_CHUNK
    n_c = length // t_c

    def chunks(a):
        return jnp.swapaxes(a.astype(jnp.float32).reshape((bsz, n_c, t_c) + a.shape[2:]), 0, 1)

    causal = jnp.tril(jnp.ones((t_c, t_c), bool))

    def step(carry, inp):
        cm, nv, m = carry
        qc, kc, vc, ic, fc = inp
        b = jnp.cumsum(fc, axis=1).transpose(0, 2, 1)
        ih = ic.transpose(0, 2, 1)
        dlog = jnp.where(causal, b[..., :, None] - b[..., None, :] + ih[..., None, :], -jnp.inf)
        inter = b + m[..., None]
        mt = jnp.maximum(inter, jnp.max(dlog, axis=-1))
        w_intra = jnp.exp(dlog - mt[..., None])
        w_inter = jnp.exp(inter - mt)
        s = jnp.einsum('bthd,bshd->bhts', qc, kc) * w_intra
        num = jnp.einsum('bhts,bshe->bhte', s, vc) + w_inter[..., None] * jnp.einsum('bthd,bhde->bhte', qc, cm)
        den = jnp.sum(s, axis=-1) + w_inter * jnp.einsum('bthd,bhd->bht', qc, nv)
        hc = num / jnp.maximum(jnp.abs(den), jnp.exp(-mt))[..., None]
        m_new = mt[..., -1]
        w_end = jnp.exp(b[..., -1:] - b + ih - m_new[..., None])
        decay = jnp.exp(b[..., -1] + m - m_new)
        c_new = decay[..., None, None] * cm + jnp.einsum('bhs,bshd,bshe->bhde', w_end, kc, vc)
        n_new = decay[..., None] * nv + jnp.einsum('bhs,bshd->bhd', w_end, kc)
        return (c_new, n_new, m_new), hc.transpose(0, 2, 1, 3)

    init = (c0.astype(jnp.float32), n0.astype(jnp.float32), m0.astype(jnp.float32))
    (cf, nf, mf), hs = lax.scan(step, init, (chunks(q), chunks(k), chunks(v), chunks(i_pre), chunks(log_f)))
    return jnp.swapaxes(hs, 0, 1).reshape(bsz, length, nh, dv), cf, nf, mf


def mlstm_mixer(h, c0, n0, m0, w_in, conv_w, conv_b, w_gate, b_gate, head_norm, w_out):
    bsz, length, _ = h.shape
    proj = h @ w_in
    qk = jax.nn.silu(short_conv(proj[..., :2 * MLSTM_QK_WIDTH], conv_w, conv_b))
    q = qk[..., :MLSTM_QK_WIDTH].reshape(bsz, length, MLSTM_HEADS, MLSTM_QK_DIM)
    k = qk[..., MLSTM_QK_WIDTH:].reshape(bsz, length, MLSTM_HEADS, MLSTM_QK_DIM) * (MLSTM_QK_DIM ** -0.5)
    v = proj[..., 2 * MLSTM_QK_WIDTH:2 * MLSTM_QK_WIDTH + MLSTM_V_WIDTH].reshape(bsz, length, MLSTM_HEADS, MLSTM_V_DIM)
    o = jax.nn.sigmoid(proj[..., 2 * MLSTM_QK_WIDTH + MLSTM_V_WIDTH:])
    g = ((h @ w_gate).astype(jnp.float32) + b_gate.astype(jnp.float32)).reshape(bsz, length, 2, 2, MLSTM_HEADS)
    i_pre = g[:, :, :, 0]
    log_f = jax.nn.log_sigmoid(g[:, :, :, 1])
    flip = lambda a: a[:, ::-1]
    hf, cf, nf, mf = mlstm_chunk_scan(q, k, v, i_pre[:, :, 0], log_f[:, :, 0], c0[:, 0], n0[:, 0], m0[:, 0])
    hb, cb, nb, mb = mlstm_chunk_scan(flip(q), flip(k), flip(v), flip(i_pre[:, :, 1]), flip(log_f[:, :, 1]),
                                      c0[:, 1], n0[:, 1], m0[:, 1])
    hsum = (hf + flip(hb)).astype(h.dtype)
    y = rms_norm(hsum, head_norm).reshape(bsz, length, MLSTM_V_WIDTH) * o
    return y @ w_out, (jnp.stack([cf, cb], 1), jnp.stack([nf, nb], 1), jnp.stack([mf, mb], 1))


def segsum(a):
    t = a.shape[-1]
    x = jnp.broadcast_to(a[..., :, None], a.shape + (t,))
    x = jnp.where(jnp.tril(jnp.ones((t, t), bool), -1), x, 0.0)
    x = jnp.cumsum(x, axis=-2)
    return jnp.where(jnp.tril(jnp.ones((t, t), bool)), x, -jnp.inf)


def ssd_chunk_scan(x, dt, a_neg, bm, cm, s0):
    bsz, length, nh, hp = x.shape
    ng, ns = bm.shape[2:]
    ne = nh // ng
    t_c = SSD_CHUNK
    n_c = length // t_c
    xd = (x.astype(jnp.float32) * dt[..., None]).reshape(bsz, n_c, t_c, ng, ne, hp)
    a = (dt * a_neg).reshape(bsz, n_c, t_c, ng, ne).transpose(0, 3, 4, 1, 2)
    a_cum = jnp.cumsum(a, axis=-1)
    bc = bm.astype(jnp.float32).reshape(bsz, n_c, t_c, ng, ns)
    cc = cm.astype(jnp.float32).reshape(bsz, n_c, t_c, ng, ns)
    cb = jnp.einsum('bclgn,bcsgn->bgcls', cc, bc)
    w_diag = cb[:, :, None] * jnp.exp(segsum(a))
    y_diag = jnp.einsum('bgecls,bcsgep->bclgep', w_diag, xd)
    decay_to_end = jnp.exp(a_cum[..., -1:] - a_cum)
    states = jnp.einsum('bclgn,bgecl,bclgep->bcgepn', bc, decay_to_end, xd)
    states = jnp.concatenate([s0.astype(jnp.float32).reshape(bsz, 1, ng, ne, hp, ns), states], axis=1)
    chunk_a = jnp.pad(a_cum[..., -1], ((0, 0), (0, 0), (0, 0), (1, 0)))
    chunk_decay = jnp.exp(segsum(chunk_a))
    all_states = jnp.einsum('bgezc,bcgepn->bzgepn', chunk_decay, states)
    y_off = jnp.einsum('bclgn,bcgepn,bgecl->bclgep', cc, all_states[:, :-1], jnp.exp(a_cum))
    y = (y_diag + y_off).reshape(bsz, length, nh, hp)
    return y, all_states[:, -1].reshape(bsz, nh, hp, ns)


def ssd_mixer(h, s0, w_in, conv_w, conv_b, dt_bias, a_log, d_skip, norm_w, w_out):
    bsz, length, _ = h.shape
    proj = h @ w_in
    z = proj[..., :SSD_INNER]
    xbc = jax.nn.silu(short_conv(proj[..., SSD_INNER:2 * SSD_INNER + 2 * SSD_GN], conv_w, conv_b))
    dt = jax.nn.softplus(proj[..., 2 * SSD_INNER + 2 * SSD_GN:].astype(jnp.float32).reshape(bsz, length, 2, SSD_HEADS)
                         + dt_bias.astype(jnp.float32))
    x = xbc[..., :SSD_INNER].reshape(bsz, length, SSD_HEADS, SSD_HEAD_DIM)
    bm = xbc[..., SSD_INNER:SSD_INNER + SSD_GN].reshape(bsz, length, SSD_GROUPS, SSD_STATE)
    cm = xbc[..., SSD_INNER + SSD_GN:].reshape(bsz, length, SSD_GROUPS, SSD_STATE)
    a_neg = -jnp.exp(a_log.astype(jnp.float32))
    flip = lambda t: t[:, ::-1]
    yf, sf = ssd_chunk_scan(x, dt[:, :, 0], a_neg[0], bm, cm, s0[:, 0])
    yb, sb = ssd_chunk_scan(flip(x), flip(dt[:, :, 1]), a_neg[1], flip(bm), flip(cm), s0[:, 1])
    y = yf + flip(yb) + x.astype(jnp.float32) * d_skip.astype(jnp.float32)[:, None]
    y = y.reshape(bsz, length, SSD_INNER).astype(h.dtype) * jax.nn.silu(z)
    return rms_norm(y, norm_w) @ w_out, (jnp.stack([sf, sb], 1),)


def hyena_filters(length, w1, b1, w2, b2, w3, freq):
    t = jnp.linspace(0.0, 1.0, length, dtype=jnp.float32)[:, None]
    bands = (HYENA_EMB - 1) // 2
    f = jnp.linspace(1e-4, bands - 1, bands, dtype=jnp.float32)
    w = 2.0 * math.pi * jnp.arange(length, dtype=jnp.float32)[:, None] / length
    z = jnp.concatenate([t, jnp.cos(f * w), -jnp.sin(f * w)], axis=-1)
    fr = freq.astype(jnp.float32)
    hid = jnp.sin(fr * (z @ w1.astype(jnp.float32) + b1.astype(jnp.float32)))
    hid = jnp.sin(fr * (hid @ w2.astype(jnp.float32) + b2.astype(jnp.float32)))
    filt = (hid @ w3.astype(jnp.float32)).reshape(length, HYENA_ORDER, 2, D_MODEL)
    min_decay = math.log(HYENA_TARGET) / HYENA_DECAY_LONG
    max_decay = math.log(HYENA_TARGET) / HYENA_DECAY_SHORT
    deltas = jnp.linspace(min_decay, max_decay, D_MODEL, dtype=jnp.float32)
    window = jnp.exp(-t * jnp.abs(deltas))
    return filt * window[:, None, None, :]


def two_sided_long_conv(u, h_fwd, h_bwd, bias):
    length, ch = h_fwd.shape
    taps = jnp.concatenate([h_fwd, jnp.zeros((1, ch), jnp.float32), h_bwd[:0:-1]], axis=0)
    taps = taps / jnp.sum(jnp.abs(taps), axis=0, keepdims=True)
    tf = jnp.fft.rfft(taps, axis=0)
    uf = jnp.fft.rfft(u.astype(jnp.float32), n=2 * length, axis=1)
    y = jnp.fft.irfft(uf * tf[None], n=2 * length, axis=1)[:, :length]
    return (y + u.astype(jnp.float32) * bias.astype(jnp.float32)).astype(u.dtype)


def hyena_mixer(h, w_in, conv_w, conv_b, f_w1, f_b1, f_w2, f_b2, f_w3, f_freq, skip_bias, w_out):
    length = h.shape[1]
    v, x1, x2 = jnp.split(short_conv(h @ w_in, conv_w, conv_b), 3, axis=-1)
    filt = hyena_filters(length, f_w1, f_b1, f_w2, f_b2, f_w3, f_freq)
    y = x1 * two_sided_long_conv(v, filt[:, 0, 0], filt[:, 0, 1], skip_bias[0])
    y = x2 * two_sided_long_conv(y, filt[:, 1, 0], filt[:, 1, 1], skip_bias[1])
    return y @ w_out


def grouped_expert_mlp(xf, experts, w_gate, w_up, w_down):
    n_tok, dm = xf.shape
    top_k = experts.shape[1]
    n_pair = n_tok * top_k
    n_exp = w_gate.shape[0]
    n_slots = -(-(n_pair + n_exp * (MOE_BLOCK - 1)) // MOE_BLOCK) * MOE_BLOCK
    n_blocks = n_slots // MOE_BLOCK
    flat_e = experts.reshape(-1)
    order = jnp.argsort(flat_e)
    sorted_e = flat_e[order]
    sorted_tok = (order // top_k).astype(jnp.int32)
    counts = jnp.bincount(flat_e, length=n_exp)
    padded = (counts + MOE_BLOCK - 1) // MOE_BLOCK * MOE_BLOCK
    pad_end = jnp.cumsum(padded)
    pad_start = pad_end - padded
    start = jnp.cumsum(counts) - counts
    dest_sorted = (pad_start[sorted_e] + jnp.arange(n_pair) - start[sorted_e]).astype(jnp.int32)
    slot_tok = jnp.full((n_slots,), n_tok, jnp.int32).at[dest_sorted].set(sorted_tok)
    x_slots = jnp.concatenate([xf, jnp.zeros((1, dm), xf.dtype)], axis=0)[slot_tok]
    block_e = jnp.minimum(jnp.searchsorted(pad_end, jnp.arange(n_blocks) * MOE_BLOCK, side='right'), n_exp - 1)

    def expert_block(args):
        xb, e = args
        return (jax.nn.silu(xb @ w_gate[e]) * (xb @ w_up[e])) @ w_down[e]

    y_slots = lax.map(expert_block, (x_slots.reshape(n_blocks, MOE_BLOCK, dm), block_e)).reshape(n_slots, dm)
    dest = jnp.zeros((n_pair,), jnp.int32).at[order].set(dest_sorted)
    return y_slots[dest].reshape(n_tok, top_k, dm)


def hier_moe(h, w_rg, b_rg, w_re, b_re, w_gate, w_up, w_down):
    bsz, length, dm = h.shape
    xf = h.reshape(bsz * length, dm)
    n_tok = xf.shape[0]
    g_prob = jax.nn.softmax((xf @ w_rg).astype(jnp.float32) + b_rg.astype(jnp.float32), axis=-1)
    g_p, g_sel = lax.top_k(g_prob, 1)
    e_logits = ((xf @ w_re).astype(jnp.float32) + b_re.astype(jnp.float32)).reshape(n_tok, MOE_GROUPS, MOE_EXPERTS_PER_GROUP)
    e_in = e_logits[jnp.arange(n_tok), g_sel[:, 0]]
    e_p, e_sel = lax.top_k(jax.nn.softmax(e_in, axis=-1), MOE_TOP_K)
    gates = g_p * e_p / jnp.sum(e_p, axis=-1, keepdims=True)
    experts = g_sel * MOE_EXPERTS_PER_GROUP + e_sel
    y = grouped_expert_mlp(xf, experts, w_gate, w_up, w_down)
    return jnp.einsum('nk,nkd->nd', gates.astype(y.dtype), y).reshape(bsz, length, dm)


def setup_inputs(seed: int = 0) -> dict:
    key = jax.random.key(seed)
    keys = iter(jax.random.split(key, 80))

    def nrm(shape, scale=1.0):
        return jax.random.normal(next(keys), shape, jnp.float32) * scale

    def gain(shape):
        return 1.0 + nrm(shape, 0.02)

    dm = D_MODEL
    na, nm, nsd, nh = N_ATTN_LAYERS, N_MLSTM_LAYERS, N_SSD_LAYERS, N_HYENA_LAYERS
    ig_bias = nrm((nm, 2, MLSTM_HEADS), 0.1)
    fg_bias = jnp.linspace(3.0, 6.0, MLSTM_HEADS, dtype=jnp.float32) + nrm((nm, 2, MLSTM_HEADS), 0.1)
    mlstm_b_gate = jnp.stack([ig_bias, fg_bias], axis=2).reshape(nm, 4 * MLSTM_HEADS)
    dt0 = jnp.exp(jax.random.uniform(next(keys), (nsd, 2, SSD_HEADS), jnp.float32, math.log(1e-3), math.log(1e-1)))
    ssd_dt_bias = dt0 + jnp.log(-jnp.expm1(-dt0))
    ssd_a_log = jnp.log(jax.random.uniform(next(keys), (nsd, 2, SSD_HEADS), jnp.float32, 1.0, 16.0))
    return {
        'x_prompt': nrm((BATCH, SEQ, dm)),
        'x_sample': nrm((DEC_BATCH, DEC_SEQ, dm)),
        'cache_attn_k': nrm((DEC_BATCH, na, PAST_LEN, ATTN_HEADS, 2, ATTN_HEAD_DIM)),
        'cache_attn_v': nrm((DEC_BATCH, na, PAST_LEN, ATTN_HEADS, 2 * ATTN_HEAD_DIM)),
        'state_mlstm_C': nrm((DEC_BATCH, nm, 2, MLSTM_HEADS, MLSTM_QK_DIM, MLSTM_V_DIM), 0.1),
        'state_mlstm_n': nrm((DEC_BATCH, nm, 2, MLSTM_HEADS, MLSTM_QK_DIM), 0.1),
        'state_mlstm_m': nrm((DEC_BATCH, nm, 2, MLSTM_HEADS)),
        'state_ssd': nrm((DEC_BATCH, nsd, 2, SSD_HEADS, SSD_HEAD_DIM, SSD_STATE), 0.1),
        'c': nrm((DEC_BATCH, dm)),
        'c_ctx': nrm((dm,)),
        'norm_mix': gain((DEPTH, dm)),
        'norm_ffn': gain((DEPTH, dm)),
        'w_mod': nrm((DEPTH, dm, 6 * dm), 0.5 * dm ** -0.5),
        'b_mod': nrm((DEPTH, 6 * dm), 0.02),
        'attn_w_in': nrm((na, dm, 3 * dm), dm ** -0.5),
        'attn_q_norm': gain((na, ATTN_HEAD_DIM)),
        'attn_k_norm': gain((na, ATTN_HEAD_DIM)),
        'attn_lambda': nrm((na, 4, ATTN_HEAD_DIM), 0.1),
        'attn_sub_norm': gain((na, 2 * ATTN_HEAD_DIM)),
        'attn_w_out': nrm((na, dm, dm), dm ** -0.5),
        'mlstm_w_in': nrm((nm, dm, 2 * MLSTM_QK_WIDTH + 2 * MLSTM_V_WIDTH), dm ** -0.5),
        'mlstm_conv_w': nrm((nm, SHORT_CONV, 2 * MLSTM_QK_WIDTH), SHORT_CONV ** -0.5),
        'mlstm_conv_b': nrm((nm, 2 * MLSTM_QK_WIDTH), 0.02),
        'mlstm_w_gate': nrm((nm, dm, 4 * MLSTM_HEADS), dm ** -0.5),
        'mlstm_b_gate': mlstm_b_gate,
        'mlstm_head_norm': gain((nm, MLSTM_V_DIM)),
        'mlstm_w_out': nrm((nm, MLSTM_V_WIDTH, dm), MLSTM_V_WIDTH ** -0.5),
        'ssd_w_in': nrm((nsd, dm, 2 * SSD_INNER + 2 * SSD_GN + 2 * SSD_HEADS), dm ** -0.5),
        'ssd_conv_w': nrm((nsd, SHORT_CONV, SSD_INNER + 2 * SSD_GN), SHORT_CONV ** -0.5),
        'ssd_conv_b': nrm((nsd, SSD_INNER + 2 * SSD_GN), 0.02),
        'ssd_dt_bias': ssd_dt_bias,
        'ssd_a_log': ssd_a_log,
        'ssd_d_skip': 1.0 + nrm((nsd, SSD_HEADS), 0.1),
        'ssd_norm': gain((nsd, SSD_INNER)),
        'ssd_w_out': nrm((nsd, SSD_INNER, dm), SSD_INNER ** -0.5),
        'hyena_w_in': nrm((nh, dm, 3 * dm), dm ** -0.5),
        'hyena_conv_w': nrm((nh, SHORT_CONV, 3 * dm), SHORT_CONV ** -0.5),
        'hyena_conv_b': nrm((nh, 3 * dm), 0.02),
        'hyena_f_w1': nrm((nh, HYENA_EMB, HYENA_FILTER_WIDTH), HYENA_EMB ** -0.5),
        'hyena_f_b1': nrm((nh, HYENA_FILTER_WIDTH), 0.1),
        'hyena_f_w2': nrm((nh, HYENA_FILTER_WIDTH, HYENA_FILTER_WIDTH), HYENA_FILTER_WIDTH ** -0.5),
        'hyena_f_b2': nrm((nh, HYENA_FILTER_WIDTH), 0.1),
        'hyena_f_w3': nrm((nh, HYENA_FILTER_WIDTH, HYENA_ORDER * 2 * dm), HYENA_FILTER_WIDTH ** -0.5),
        'hyena_f_freq': 1.0 + nrm((nh, HYENA_FILTER_WIDTH), 0.1),
        'hyena_skip_bias': nrm((nh, HYENA_ORDER, dm), 0.5),
        'hyena_w_out': nrm((nh, dm, dm), dm ** -0.5),
        'moe_w_group': nrm((DEPTH, dm, MOE_GROUPS), dm ** -0.5),
        'moe_b_group': nrm((DEPTH, MOE_GROUPS), 0.01),
        'moe_w_expert': nrm((DEPTH, dm, MOE_EXPERTS), dm ** -0.5),
        'moe_b_expert': nrm((DEPTH, MOE_EXPERTS), 0.01),
        'moe_w_gate': nrm((DEPTH, MOE_EXPERTS, dm, MOE_FF), dm ** -0.5),
        'moe_w_up': nrm((DEPTH, MOE_EXPERTS, dm, MOE_FF), dm ** -0.5),
        'moe_w_down': nrm((DEPTH, MOE_EXPERTS, MOE_FF, dm), MOE_FF ** -0.5),
    }


def reference(x_prompt, x_sample, cache_attn_k, cache_attn_v, state_mlstm_C, state_mlstm_n, state_mlstm_m,
              state_ssd, c, c_ctx, norm_mix, norm_ffn, w_mod, b_mod,
              attn_w_in, attn_q_norm, attn_k_norm, attn_lambda, attn_sub_norm, attn_w_out,
              mlstm_w_in, mlstm_conv_w, mlstm_conv_b, mlstm_w_gate, mlstm_b_gate, mlstm_head_norm, mlstm_w_out,
              ssd_w_in, ssd_conv_w, ssd_conv_b, ssd_dt_bias, ssd_a_log, ssd_d_skip, ssd_norm, ssd_w_out,
              hyena_w_in, hyena_conv_w, hyena_conv_b, hyena_f_w1, hyena_f_b1, hyena_f_w2, hyena_f_b2,
              hyena_f_w3, hyena_f_freq, hyena_skip_bias, hyena_w_out,
              moe_w_group, moe_b_group, moe_w_expert, moe_b_expert, moe_w_gate, moe_w_up, moe_w_down):

    def token_mixer(i, h, cache):
        kind, j = i % N_MIXERS, i // N_MIXERS
        bsz = h.shape[0]
        if kind == 0:
            lam_init = 0.8 - 0.6 * math.exp(-0.3 * i)
            w = (attn_w_in[j], attn_q_norm[j], attn_k_norm[j], attn_lambda[j], attn_sub_norm[j], attn_w_out[j], lam_init)
            if cache is None:
                return attn_context(h, *w)
            return attn_latent(h, cache[0], cache[1], *w)
        if kind == 1:
            if cache is None:
                cache = (jnp.zeros((bsz, 2, MLSTM_HEADS, MLSTM_QK_DIM, MLSTM_V_DIM), jnp.float32),
                         jnp.zeros((bsz, 2, MLSTM_HEADS, MLSTM_QK_DIM), jnp.float32),
                         jnp.zeros((bsz, 2, MLSTM_HEADS), jnp.float32))
            return mlstm_mixer(h, cache[0], cache[1], cache[2], mlstm_w_in[j], mlstm_conv_w[j], mlstm_conv_b[j],
                               mlstm_w_gate[j], mlstm_b_gate[j], mlstm_head_norm[j], mlstm_w_out[j])
        if kind == 2:
            s0 = jnp.zeros((bsz, 2, SSD_HEADS, SSD_HEAD_DIM, SSD_STATE), jnp.float32) if cache is None else cache[0]
            return ssd_mixer(h, s0, ssd_w_in[j], ssd_conv_w[j], ssd_conv_b[j], ssd_dt_bias[j], ssd_a_log[j],
                             ssd_d_skip[j], ssd_norm[j], ssd_w_out[j])
        out = hyena_mixer(h, hyena_w_in[j], hyena_conv_w[j], hyena_conv_b[j], hyena_f_w1[j], hyena_f_b1[j],
                          hyena_f_w2[j], hyena_f_b2[j], hyena_f_w3[j], hyena_f_freq[j], hyena_skip_bias[j], hyena_w_out[j])
        return out, ()

    def trunk_layer(i, x, cond, cache):
        sh1, sc1, g1, sh2, sc2, g2 = adaln(cond, w_mod[i], b_mod[i])
        out, st = token_mixer(i, rms_norm(x, norm_mix[i]) * (1 + sc1) + sh1, cache)
        x = x + g1 * out
        h2 = rms_norm(x, norm_ffn[i]) * (1 + sc2) + sh2
        x = x + g2 * hier_moe(h2, moe_w_group[i], moe_b_group[i], moe_w_expert[i], moe_b_expert[i],
                              moe_w_gate[i], moe_w_up[i], moe_w_down[i])
        return x, st

    ctx_states = [[] for _ in range(N_MIXERS)]
    y_prompt = x_prompt
    for i in range(DEPTH):
        y_prompt, st = trunk_layer(i, y_prompt, c_ctx[None, :], None)
        ctx_states[i % N_MIXERS].append(st)
    new_attn_k = jnp.stack([st[0] for st in ctx_states[0]], axis=1)
    new_attn_v = jnp.stack([st[1] for st in ctx_states[0]], axis=1)
    new_mlstm_C = jnp.stack([st[0] for st in ctx_states[1]], axis=1)
    new_mlstm_n = jnp.stack([st[1] for st in ctx_states[1]], axis=1)
    new_mlstm_m = jnp.stack([st[2] for st in ctx_states[1]], axis=1)
    new_ssd_state = jnp.stack([st[0] for st in ctx_states[2]], axis=1)

    y_sample = x_sample
    for i in range(DEPTH):
        kind, j = i % N_MIXERS, i // N_MIXERS
        if kind == 0:
            cache = (cache_attn_k[:, j], cache_attn_v[:, j])
        elif kind == 1:
            cache = (state_mlstm_C[:, j], state_mlstm_n[:, j], state_mlstm_m[:, j])
        elif kind == 2:
            cache = (state_ssd[:, j],)
        else:
            cache = ()
        y_sample, _ = trunk_layer(i, y_sample, c, cache)

    return (y_prompt, y_sample, new_attn_k, new_attn_v, new_mlstm_C, new_mlstm_n, new_mlstm_m, new_ssd_state)
```

```python
import functools
import math

import jax
import jax.numpy as jnp
from jax import lax
from jax.experimental import pallas as pl
from jax.experimental.pallas import tpu as pltpu

F32 = jnp.float32
BF16 = jnp.bfloat16
NORM_EPS = 1e-6
ROPE_GRID_W = 64
ROPE_BASE = 10000.0
HYENA_EMB = 33
HYENA_DECAY_SHORT = 0.3
HYENA_DECAY_LONG = 1.5
HYENA_TARGET = 1e-2
MOE_TOP_K = 2
MOE_ROW_BLOCK = 256
SCAN_CHUNK = 256
VMEM_LIMIT = 56 * 1024 * 1024

_NT = (((1,), (1,)), ((), ()))
_TN = (((0,), (0,)), ((), ()))


def _tile(n, pref, align):
    if n <= pref:
        return n
    t = (pref // align) * align
    while t >= align:
        if n % t == 0:
            return t
        t -= align
    return n


def _params(sem):
    return pltpu.CompilerParams(dimension_semantics=sem, vmem_limit_bytes=VMEM_LIMIT)


def _dot(a, b):
    return jnp.dot(a, b, preferred_element_type=F32)


def _split_bf16(a):
    hi = a.astype(BF16)
    lo = (a - hi.astype(F32)).astype(BF16)
    return hi, lo


def _mm_kernel(x_ref, w_ref, *rest, nk, x3, mode):
    if mode == "resid":
        res_ref, gate_ref, o_ref, acc_ref = rest
    elif mode == "bias":
        b_ref, o_ref, acc_ref = rest
    else:
        o_ref, acc_ref = rest
    k = pl.program_id(2)

    @pl.when(k == 0)
    def _():
        acc_ref[...] = jnp.zeros_like(acc_ref)

    if x3:
        xh, xl = _split_bf16(x_ref[...].astype(F32))
        wh, wl = _split_bf16(w_ref[...].astype(F32))
        acc_ref[...] += _dot(xh, wh) + _dot(xh, wl) + _dot(xl, wh)
    else:
        acc_ref[...] += _dot(x_ref[...].astype(BF16), w_ref[...].astype(BF16))

    @pl.when(k == nk - 1)
    def _():
        r = acc_ref[...]
        if mode == "bias":
            r = r + b_ref[...]
        elif mode == "resid":
            r = res_ref[...] + gate_ref[...] * r
        o_ref[...] = r.astype(o_ref.dtype)


def matmul(x, w, *, w_idx=None, n0=0, n=None, tm=1024, tn=1024, tk=1024, out_dtype=F32, x3=False,
           bias=None, resid=None, gate=None, row_map=None, name="matmul"):
    m, kdim = x.shape
    n_full = w.shape[-1]
    n = n_full if n is None else n
    tm = _tile(m, tm, 8)
    tk = _tile(kdim, tk, 128)
    tn = _tile(n, tn, 128)
    assert n0 % tn == 0 and m % tm == 0 and kdim % tk == 0 and n % tn == 0
    j0 = n0 // tn
    nk = kdim // tk
    if w.ndim == 3:
        w_spec = pl.BlockSpec((None, tk, tn), lambda i, j, k: (w_idx, k, j + j0))
    else:
        w_spec = pl.BlockSpec((tk, tn), lambda i, j, k: (k, j + j0))
    in_specs = [pl.BlockSpec((tm, tk), lambda i, j, k: (i, k)), w_spec]
    args = [x, w]
    mode = "plain"
    if bias is not None:
        mode = "bias"
        in_specs.append(pl.BlockSpec((1, tn), lambda i, j, k: (0, j)))
        args.append(bias)
    elif resid is not None:
        mode = "resid"
        in_specs.append(pl.BlockSpec((tm, tn), lambda i, j, k: (i, j)))
        in_specs.append(pl.BlockSpec((None, 1, tn), lambda i, j, k: (row_map(i * tm), 0, j)))
        args += [resid, gate]
    return pl.pallas_call(
        functools.partial(_mm_kernel, nk=nk, x3=x3, mode=mode),
        grid=(m // tm, n // tn, nk),
        in_specs=in_specs,
        out_specs=pl.BlockSpec((tm, tn), lambda i, j, k: (i, j)),
        out_shape=jax.ShapeDtypeStruct((m, n), out_dtype),
        scratch_shapes=[pltpu.VMEM((tm, tn), F32)],
        compiler_params=_params(("parallel", "parallel", "arbitrary")),
        name=name,
    )(*args)


def _norm_mod_kernel(x_ref, w_ref, sc_ref, sh_ref, *o_refs):
    x = x_ref[...]
    y = x * lax.rsqrt(jnp.mean(x * x, axis=-1, keepdims=True) + NORM_EPS) * w_ref[...]
    h = y * (1.0 + sc_ref[...]) + sh_ref[...]
    for o_ref in o_refs:
        o_ref[...] = h.astype(o_ref.dtype)


def norm_mod(x, w, scale, shift, row_map, out_dtypes, tm):
    rows, d = x.shape
    row = pl.BlockSpec((tm, d), lambda i: (i, 0))
    mod = pl.BlockSpec((None, 1, d), lambda i: (row_map(i * tm), 0, 0))
    outs = pl.pallas_call(
        _norm_mod_kernel,
        grid=(rows // tm,),
        in_specs=[row, pl.BlockSpec((1, d), lambda i: (0, 0)), mod, mod],
        out_specs=[row for _ in out_dtypes],
        out_shape=[jax.ShapeDtypeStruct((rows, d), dt) for dt in out_dtypes],
        compiler_params=_params(("parallel",)),
        name="norm_mod",
    )(x, w.reshape(1, d), scale, shift)
    return outs


def _attn_kernel(lam_ref, q_ref, k_ref, v_ref, sn_ref, o_ref, *, half, scale, post_scale):
    q = q_ref[...].astype(F32)
    k = k_ref[...]
    v = v_ref[...]
    lane = lax.broadcasted_iota(jnp.int32, q.shape, 1)
    q1 = jnp.where(lane < half, q, 0.0).astype(BF16)
    q2 = jnp.where(lane >= half, q, 0.0).astype(BF16)

    def softmax(s):
        e = jnp.exp(s - jnp.max(s, axis=-1, keepdims=True))
        return e / jnp.sum(e, axis=-1, keepdims=True)

    p1 = softmax(lax.dot_general(q1, k, _NT, preferred_element_type=F32) * scale)
    p2 = softmax(lax.dot_general(q2, k, _NT, preferred_element_type=F32) * scale)
    a = p1 - lam_ref[0] * p2
    o = _dot(a.astype(BF16), v)
    o = o * lax.rsqrt(jnp.mean(o * o, axis=-1, keepdims=True) + NORM_EPS) * sn_ref[...] * post_scale
    o_ref[...] = o.astype(o_ref.dtype)


def diff_attention(q, k, v, lam, sub_norm, *, nseq, lq, lk, heads, post_scale):
    hd2 = q.shape[1] // heads
    tq = _tile(lq, 256, 8)
    nq = lq // tq
    return pl.pallas_call(
        functools.partial(_attn_kernel, half=hd2 // 2, scale=(hd2 // 2) ** -0.5, post_scale=post_scale),
        grid=(nseq, heads, nq),
        in_specs=[
            pl.BlockSpec(memory_space=pltpu.SMEM),
            pl.BlockSpec((tq, hd2), lambda s, h, i: (s * nq + i, h)),
            pl.BlockSpec((lk, hd2), lambda s, h, i: (s, h)),
            pl.BlockSpec((lk, hd2), lambda s, h, i: (s, h)),
            pl.BlockSpec((1, hd2), lambda s, h, i: (0, 0)),
        ],
        out_specs=pl.BlockSpec((tq, hd2), lambda s, h, i: (s * nq + i, h)),
        out_shape=jax.ShapeDtypeStruct(q.shape, BF16),
        compiler_params=_params(("parallel", "parallel", "parallel")),
        name="diff_attention",
    )(lam.reshape(1), q, k, v, sub_norm.reshape(1, hd2))


def _causal_masks(t, d):
    sign = jnp.where(d == 0, 1, -1)
    r = lax.broadcasted_iota(jnp.int32, (t, t), 0)
    s = lax.broadcasted_iota(jnp.int32, (t, t), 1)
    diff = (s - r) * sign
    return diff <= 0, diff >= 0


def _chunk_row(s, d, c, nc):
    return s * nc + jnp.where(d == 0, c, nc - 1 - c)


def _mlstm_kernel(*refs, t, nc, has_init, emit_state):
    refs = list(refs)
    q_ref, k_ref, v_ref, gcol_ref, grow_ref = refs[:5]
    pos = 5
    if has_init:
        c0_ref, n0_ref, m0_ref = refs[pos:pos + 3]
        pos += 3
    h_ref = refs[pos]
    pos += 1
    if emit_state:
        co_ref, no_ref, mo_ref = refs[pos:pos + 3]
        pos += 3
    c_s, n_s, m_s = refs[pos:pos + 3]
    d = pl.program_id(1)
    c = pl.program_id(3)

    @pl.when(c == 0)
    def _():
        if has_init:
            c_s[...] = c0_ref[...]
            n_s[...] = n0_ref[...]
            m_s[...] = m0_ref[...]
        else:
            c_s[...] = jnp.zeros_like(c_s)
            n_s[...] = jnp.zeros_like(n_s)
            m_s[...] = jnp.zeros_like(m_s)

    q = q_ref[...]
    k = k_ref[...]
    v = v_ref[...]
    i_col = gcol_ref[:, 0:1]
    f_col = gcol_ref[:, 1:2]
    i_row = grow_ref[0:1, :]
    f_row = grow_ref[1:2, :]
    causal, causal_t = _causal_masks(t, d)
    b_col = jnp.sum(jnp.where(causal, f_row, 0.0), axis=1, keepdims=True)
    b_row = jnp.sum(jnp.where(causal_t, f_col, 0.0), axis=0, keepdims=True)
    dlog = jnp.where(causal, b_col - b_row + i_row, -jnp.inf)
    m_prev = m_s[...]
    inter = b_col + m_prev
    mt = jnp.maximum(inter, jnp.max(dlog, axis=1, keepdims=True))
    w_intra = jnp.exp(dlog - mt)
    w_inter = jnp.exp(inter - mt)
    cm = c_s[...]
    nv = n_s[...]
    sm = lax.dot_general(q, k, _NT, preferred_element_type=F32) * w_intra
    num = _dot(sm.astype(BF16), v) + w_inter * _dot(q, cm.astype(BF16))
    den = jnp.sum(sm, axis=1, keepdims=True) + w_inter * jnp.sum(q.astype(F32) * nv, axis=1, keepdims=True)
    h_ref[...] = num / jnp.maximum(jnp.abs(den), jnp.exp(-mt))

    rowid = lax.broadcasted_iota(jnp.int32, (t, 1), 0)
    end_row = jnp.where(d == 0, t - 1, 0)
    m_new = jnp.sum(jnp.where(rowid == end_row, mt, 0.0), axis=0, keepdims=True)
    b_last = jnp.sum(f_row, axis=1, keepdims=True)
    kw = k.astype(F32) * jnp.exp(b_last - b_col + i_col - m_new)
    decay = jnp.exp(b_last + m_prev - m_new)
    c_new = decay * cm + lax.dot_general(kw.astype(BF16), v, _TN, preferred_element_type=F32)
    n_new = decay * nv + jnp.sum(kw, axis=0, keepdims=True)
    c_s[...] = c_new
    n_s[...] = n_new
    m_s[...] = m_new
    if emit_state:
        @pl.when(c == nc - 1)
        def _():
            co_ref[...] = c_new
            no_ref[...] = n_new
            mo_ref[...] = m_new


def mlstm_scan(q, k, v, gcol, grow, init, *, nseq, length, heads, emit_state):
    rows = q.shape[0]
    dk = q.shape[1] // heads
    dv = v.shape[1] // heads
    t = _tile(length, SCAN_CHUNK, 8)
    nc = length // t
    rc = lambda s, d, h, c: _chunk_row(s, d, c, nc)
    in_specs = [
        pl.BlockSpec((t, dk), lambda s, d, h, c: (rc(s, d, h, c), h)),
        pl.BlockSpec((t, dk), lambda s, d, h, c: (rc(s, d, h, c), h)),
        pl.BlockSpec((t, dv), lambda s, d, h, c: (rc(s, d, h, c), h)),
        pl.BlockSpec((None, None, t, 2), lambda s, d, h, c: (d, h, rc(s, d, h, c), 0)),
        pl.BlockSpec((None, None, None, 2, t), lambda s, d, h, c: (d, h, rc(s, d, h, c), 0, 0)),
    ]
    args = [q, k, v, gcol, grow]
    st_specs = [
        pl.BlockSpec((None, None, None, dk, dv), lambda s, d, h, c: (s, d, h, 0, 0)),
        pl.BlockSpec((None, None, None, 1, dk), lambda s, d, h, c: (s, d, h, 0, 0)),
        pl.BlockSpec((None, None, None, 1, 1), lambda s, d, h, c: (s, d, h, 0, 0)),
    ]
    if init is not None:
        in_specs += st_specs
        args += list(init)
    out_specs = [pl.BlockSpec((None, t, dv), lambda s, d, h, c: (d, rc(s, d, h, c), h))]
    out_shape = [jax.ShapeDtypeStruct((2, rows, heads * dv), F32)]
    if emit_state:
        out_specs += st_specs
        out_shape += [jax.ShapeDtypeStruct((nseq, 2, heads, dk, dv), F32),
                      jax.ShapeDtypeStruct((nseq, 2, heads, 1, dk), F32),
                      jax.ShapeDtypeStruct((nseq, 2, heads, 1, 1), F32)]
    return pl.pallas_call(
        functools.partial(_mlstm_kernel, t=t, nc=nc, has_init=init is not None, emit_state=emit_state),
        grid=(nseq, 2, heads, nc),
        in_specs=in_specs,
        out_specs=out_specs,
        out_shape=out_shape,
        scratch_shapes=[pltpu.VMEM((dk, dv), F32), pltpu.VMEM((1, dk), F32), pltpu.VMEM((1, 1), F32)],
        compiler_params=_params(("parallel", "parallel", "parallel", "arbitrary")),
        name="mlstm_scan",
    )(*args)


def _ssd_kernel(*refs, t, nc, hpg, p, has_init, emit_state):
    refs = list(refs)
    x_ref, b_ref, c_ref, acol_ref, arow_ref, dtcol_ref = refs[:6]
    pos = 6
    if has_init:
        s0_ref = refs[pos]
        pos += 1
    y_ref = refs[pos]
    pos += 1
    if emit_state:
        so_ref = refs[pos]
        pos += 1
    st = refs[pos]
    d = pl.program_id(1)
    c = pl.program_id(3)
    width = hpg * p

    @pl.when(c == 0)
    def _():
        if has_init:
            st[...] = s0_ref[...].T
        else:
            st[...] = jnp.zeros_like(st)

    causal, causal_t = _causal_masks(t, d)
    lane_head = lax.broadcasted_iota(jnp.int32, (1, width), 1) // p
    lane_pair = lax.broadcasted_iota(jnp.int32, (1, 2 * p), 1)
    bm = b_ref[...]
    cmat = c_ref[...]
    dt_e = jnp.zeros((t, width), F32)
    for e in range(hpg):
        dt_e = jnp.where(lane_head == e, dtcol_ref[:, e:e + 1], dt_e)
    xd = x_ref[...] * dt_e
    xdb = xd.astype(BF16)
    cb = lax.dot_general(cmat, bm, _NT, preferred_element_type=F32)
    cum_e = jnp.zeros((t, width), F32)
    tot_e = jnp.zeros((1, width), F32)
    y_pairs = []
    for e in range(hpg):
        a_row = arow_ref[e:e + 1, :]
        a_col = acol_ref[:, e:e + 1]
        cum_col = jnp.sum(jnp.where(causal, a_row, 0.0), axis=1, keepdims=True)
        cum_row = jnp.sum(jnp.where(causal_t, a_col, 0.0), axis=0, keepdims=True)
        wmat = (cb * jnp.exp(jnp.where(causal, cum_col - cum_row, -jnp.inf))).astype(BF16)
        j = e // 2
        yp = _dot(wmat, xdb[:, j * 2 * p:(j + 1) * 2 * p])
        if e % 2 == 0:
            y_even = yp
        else:
            y_pairs.append(jnp.where(lane_pair < p, y_even, yp))
        cum_e = jnp.where(lane_head == e, cum_col, cum_e)
        tot_e = jnp.where(lane_head == e, jnp.sum(a_row, axis=1, keepdims=True), tot_e)
    s_prev = st[...]
    y_off = _dot(cmat, s_prev.astype(BF16)) * jnp.exp(cum_e)
    y_ref[...] = jnp.concatenate(y_pairs, axis=1) + y_off
    xw = (xd * jnp.exp(tot_e - cum_e)).astype(BF16)
    s_new = jnp.exp(tot_e) * s_prev + lax.dot_general(bm, xw, _TN, preferred_element_type=F32)
    st[...] = s_new
    if emit_state:
        @pl.when(c == nc - 1)
        def _():
            so_ref[...] = s_new.T


def ssd_scan(x, bmat, cmat, acol, arow, dtcol, init, *, nseq, length, groups, hpg, emit_state):
    rows, inner = x.shape
    width = inner // groups
    p = width // hpg
    ns = bmat.shape[1] // groups
    t = _tile(length, SCAN_CHUNK, 8)
    nc = length // t
    rc = lambda s, d, g, c: _chunk_row(s, d, c, nc)
    in_specs = [
        pl.BlockSpec((t, width), lambda s, d, g, c: (rc(s, d, g, c), g)),
        pl.BlockSpec((t, ns), lambda s, d, g, c: (rc(s, d, g, c), g)),
        pl.BlockSpec((t, ns), lambda s, d, g, c: (rc(s, d, g, c), g)),
        pl.BlockSpec((None, None, t, hpg), lambda s, d, g, c: (d, g, rc(s, d, g, c), 0)),
        pl.BlockSpec((None, None, None, hpg, t), lambda s, d, g, c: (d, g, rc(s, d, g, c), 0, 0)),
        pl.BlockSpec((None, None, t, hpg), lambda s, d, g, c: (d, g, rc(s, d, g, c), 0)),
    ]
    args = [x, bmat, cmat, acol, arow, dtcol]
    st_spec = pl.BlockSpec((None, None, None, width, ns), lambda s, d, g, c: (s, d, g, 0, 0))
    if init is not None:
        in_specs.append(st_spec)
        args.append(init)
    out_specs = [pl.BlockSpec((None, t, width), lambda s, d, g, c: (d, rc(s, d, g, c), g))]
    out_shape = [jax.ShapeDtypeStruct((2, rows, inner), F32)]
    if emit_state:
        out_specs.append(st_spec)
        out_shape.append(jax.ShapeDtypeStruct((nseq, 2, groups, width, ns), F32))
    return pl.pallas_call(
        functools.partial(_ssd_kernel, t=t, nc=nc, hpg=hpg, p=p, has_init=init is not None, emit_state=emit_state),
        grid=(nseq, 2, groups, nc),
        in_specs=in_specs,
        out_specs=out_specs,
        out_shape=out_shape,
        scratch_shapes=[pltpu.VMEM((ns, width), F32)],
        compiler_params=_params(("parallel", "parallel", "parallel", "arbitrary")),
        name="ssd_scan",
    )(*args)


def _hyena_kernel(u_ref, f_ref, g_ref, tp_ref, tq_ref, bias_ref, xm_ref, o_ref, *, length):
    u = u_ref[...]
    a = _dot(f_ref[...], u.astype(BF16))
    a_sw = jnp.concatenate([a[length:], a[:length]], axis=0)
    y = a * tp_ref[...] + a_sw * tq_ref[...]
    conv = _dot(g_ref[...], y.astype(BF16))
    o_ref[...] = ((conv + u * bias_ref[...]) * xm_ref[...]).astype(o_ref.dtype)


def hyena_conv(u, fmat, gmat, tp, tq, bias, xm, *, nseq, length, out_dtype):
    rows, d = u.shape
    tc = _tile(d, 256, 128)
    n2 = 2 * length
    return pl.pallas_call(
        functools.partial(_hyena_kernel, length=length),
        grid=(d // tc, nseq),
        in_specs=[
            pl.BlockSpec((length, tc), lambda j, s: (s, j)),
            pl.BlockSpec((n2, length), lambda j, s: (0, 0)),
            pl.BlockSpec((length, n2), lambda j, s: (0, 0)),
            pl.BlockSpec((n2, tc), lambda j, s: (0, j)),
            pl.BlockSpec((n2, tc), lambda j, s: (0, j)),
            pl.BlockSpec((1, tc), lambda j, s: (0, j)),
            pl.BlockSpec((length, tc), lambda j, s: (s, j)),
        ],
        out_specs=pl.BlockSpec((length, tc), lambda j, s: (s, j)),
        out_shape=jax.ShapeDtypeStruct((rows, d), out_dtype),
        compiler_params=_params(("parallel", "parallel")),
        name="hyena_conv",
    )(u, fmat, gmat, tp, tq, bias.reshape(1, d), xm)


def _dft_matrices(length):
    n2 = 2 * length
    r = jnp.arange(n2, dtype=jnp.int32)
    kfreq = jnp.where(r <= length, r, r - length)
    ang = ((kfreq[:, None] * r[None, :]) % n2).astype(F32) * (2.0 * math.pi / n2)
    full = jnp.where((r > length)[:, None], -jnp.sin(ang), jnp.cos(ang))
    wk = jnp.where((r == 0) | (r == length), 1.0, 2.0) / n2
    inv = (full[:, :length] * wk[:, None]).T
    return full, full[:, :length].astype(BF16), inv.astype(BF16)


def _hyena_filter_spectra(length, f_w1, f_b1, f_w2, f_b2, f_w3, f_freq, d):
    hp = lax.Precision.HIGHEST
    t = jnp.linspace(0.0, 1.0, length, dtype=F32)[:, None]
    bands = (HYENA_EMB - 1) // 2
    f = jnp.linspace(1e-4, bands - 1, bands, dtype=F32)
    w = 2.0 * math.pi * jnp.arange(length, dtype=F32)[:, None] / length
    z = jnp.concatenate([t, jnp.cos(f * w), -jnp.sin(f * w)], axis=-1)
    hid = jnp.sin(f_freq * (jnp.dot(z, f_w1, precision=hp) + f_b1))
    hid = jnp.sin(f_freq * (jnp.dot(hid, f_w2, precision=hp) + f_b2))
    filt = matmul(hid, f_w3, x3=True, tk=hid.shape[1], name="hyena_filter").reshape(length, 2, 2, d)
    min_decay = math.log(HYENA_TARGET) / HYENA_DECAY_LONG
    max_decay = math.log(HYENA_TARGET) / HYENA_DECAY_SHORT
    deltas = jnp.linspace(min_decay, max_decay, d, dtype=F32)
    filt = filt * jnp.exp(-t * jnp.abs(deltas))[:, None, None, :]
    full, fmat, gmat = _dft_matrices(length)
    tps, tqs = [], []
    for o in range(2):
        h_fwd, h_bwd = filt[:, o, 0], filt[:, o, 1]
        taps = jnp.concatenate([h_fwd, jnp.zeros((1, d), F32), h_bwd[:0:-1]], axis=0)
        taps = taps / jnp.sum(jnp.abs(taps), axis=0, keepdims=True)
        spec = matmul(full, taps, x3=True, name="hyena_filter_dft")
        top, bot = spec[:length], spec[length:]
        zero = jnp.zeros((1, d), F32)
        im = jnp.concatenate([zero, bot[1:]], axis=0)
        tps.append(jnp.concatenate([top, bot[0:1], top[1:]], axis=0))
        tqs.append(jnp.concatenate([-im, im], axis=0))
    return fmat, gmat, tps, tqs


def _moe_up_kernel(be_ref, nu_ref, x_ref, wg_ref, wu_ref, o_ref, *, fchunk):
    @pl.when(pl.program_id(0) < nu_ref[0])
    def _():
        x = x_ref[...]
        for f0 in range(0, o_ref.shape[1], fchunk):
            g = _dot(x, wg_ref[:, f0:f0 + fchunk].astype(BF16))
            u = _dot(x, wu_ref[:, f0:f0 + fchunk].astype(BF16))
            o_ref[:, f0:f0 + fchunk] = (g / (1.0 + jnp.exp(-g)) * u).astype(o_ref.dtype)


def _moe_down_kernel(be_ref, nu_ref, h_ref, wd_ref, o_ref):
    @pl.when(pl.program_id(0) < nu_ref[0])
    def _():
        o_ref[...] = _dot(h_ref[...], wd_ref[...].astype(BF16))


def moe_experts(x_slots, block_e, n_used, w_gate, w_up, w_down, layer):
    n_slots, d = x_slots.shape
    ff = w_gate.shape[-1]
    tb = MOE_ROW_BLOCK
    n_blocks = n_slots // tb
    blk = lambda b, be, nu: (jnp.minimum(b, nu[0] - 1), 0)
    wsel = lambda b, be, nu: (layer, be[b], 0, 0)
    hmid = pl.pallas_call(
        functools.partial(_moe_up_kernel, fchunk=_tile(ff, 256, 128)),
        grid_spec=pltpu.PrefetchScalarGridSpec(
            num_scalar_prefetch=2, grid=(n_blocks,),
            in_specs=[pl.BlockSpec((tb, d), blk),
                      pl.BlockSpec((None, None, d, ff), wsel),
                      pl.BlockSpec((None, None, d, ff), wsel)],
            out_specs=pl.BlockSpec((tb, ff), blk)),
        out_shape=jax.ShapeDtypeStruct((n_slots, ff), BF16),
        compiler_params=_params(("arbitrary",)),
        name="moe_gate_up",
    )(block_e, n_used, x_slots, w_gate, w_up)
    return pl.pallas_call(
        _moe_down_kernel,
        grid_spec=pltpu.PrefetchScalarGridSpec(
            num_scalar_prefetch=2, grid=(n_blocks,),
            in_specs=[pl.BlockSpec((tb, ff), blk),
                      pl.BlockSpec((None, None, ff, d), wsel)],
            out_specs=pl.BlockSpec((tb, d), blk)),
        out_shape=jax.ShapeDtypeStruct((n_slots, d), F32),
        compiler_params=_params(("arbitrary",)),
        name="moe_down",
    )(block_e, n_used, hmid, w_down)


def hier_moe(h_f32, h_bf16, w_rg, b_rg, w_re, b_re, w_gate, w_up, w_down, layer):
    rows, d = h_f32.shape
    n_grp = w_rg.shape[-1]
    n_exp = w_re.shape[-1]
    epg = n_exp // n_grp
    tb = MOE_ROW_BLOCK
    w_router = jnp.concatenate([w_rg[layer], w_re[layer]], axis=-1)
    logits = matmul(h_f32, w_router, x3=True, tm=512, name="moe_router")
    g_prob = jax.nn.softmax(logits[:, :n_grp] + b_rg[layer], axis=-1)
    g_p, g_sel = lax.top_k(g_prob, 1)
    e_logits = (logits[:, n_grp:] + b_re[layer]).reshape(rows, n_grp, epg)
    e_in = jnp.take_along_axis(e_logits, g_sel[:, :, None], axis=1)[:, 0]
    e_p, e_sel = lax.top_k(jax.nn.softmax(e_in, axis=-1), MOE_TOP_K)
    gates = g_p * e_p / jnp.sum(e_p, axis=-1, keepdims=True)
    experts = g_sel * epg + e_sel

    n_pair = rows * MOE_TOP_K
    n_blocks = -(-(n_pair + n_exp * (tb - 1)) // tb)
    flat_e = experts.reshape(-1)
    order = jnp.argsort(flat_e)
    sorted_e = flat_e[order]
    counts = jnp.bincount(flat_e, length=n_exp)
    padded = (counts + tb - 1) // tb * tb
    pad_end = jnp.cumsum(padded)
    pad_start = pad_end - padded
    start = jnp.cumsum(counts) - counts
    dest_sorted = (pad_start[sorted_e] + jnp.arange(n_pair) - start[sorted_e]).astype(jnp.int32)
    slot_tok = jnp.zeros((n_blocks * tb,), jnp.int32).at[dest_sorted].set((order // MOE_TOP_K).astype(jnp.int32))
    dest = jnp.zeros((n_pair,), jnp.int32).at[order].set(dest_sorted)
    n_used = (pad_end[-1] // tb).astype(jnp.int32)
    block_e = jnp.minimum(jnp.searchsorted(pad_end, jnp.arange(n_blocks) * tb, side="right"), n_exp - 1)
    block_e = jnp.where(jnp.arange(n_blocks) < n_used, block_e, block_e[n_used - 1]).astype(jnp.int32)

    y_slots = moe_experts(h_bf16[slot_tok], block_e, n_used.reshape(1), w_gate, w_up, w_down, layer)
    y = y_slots[dest].reshape(rows, MOE_TOP_K, d)
    return jnp.sum(gates[:, :, None] * y, axis=1)


def _per_seq(fn, x, n_ctx, seq, dec_seq):
    c = x.shape[-1]
    a = fn(x[:n_ctx].reshape(-1, seq, c)).reshape(n_ctx, -1)
    b = fn(x[n_ctx:].reshape(-1, dec_seq, c)).reshape(x.shape[0] - n_ctx, -1)
    return jnp.concatenate([a, b], axis=0)


def _short_conv(x, w, b):
    length = x.shape[1]
    prev = jnp.pad(x, ((0, 0), (1, 0), (0, 0)))[:, :length]
    nxt = jnp.pad(x, ((0, 0), (0, 1), (0, 0)))[:, 1:]
    return prev * w[0] + x * w[1] + nxt * w[2] + b


def _rope_tables(length, hd):
    rows = length // ROPE_GRID_W
    row = jnp.broadcast_to(jnp.arange(rows, dtype=F32)[:, None], (rows, ROPE_GRID_W)).reshape(length)
    col = jnp.broadcast_to(jnp.arange(ROPE_GRID_W, dtype=F32)[None, :], (rows, ROPE_GRID_W)).reshape(length)
    axis_dim = hd // 2
    inv_freq = ROPE_BASE ** (-jnp.arange(0, axis_dim, 2, dtype=F32) / axis_dim)
    ang_r = row[:, None] * inv_freq
    ang_c = col[:, None] * inv_freq
    return jnp.cos(ang_r), jnp.sin(ang_r), jnp.cos(ang_c), jnp.sin(ang_c)


def _axial_rope(x, tables):
    cos_r, sin_r, cos_c, sin_c = tables

    def rope_1d(y, cos, sin):
        y1, y2 = jnp.split(y, 2, axis=-1)
        cos = cos[:, None, None, :]
        sin = sin[:, None, None, :]
        return jnp.concatenate([y1 * cos - y2 * sin, y2 * cos + y1 * sin], axis=-1)

    x_row, x_col = jnp.split(x, 2, axis=-1)
    return jnp.concatenate([rope_1d(x_row, cos_r, sin_r), rope_1d(x_col, cos_c, sin_c)], axis=-1)


def _rms(x, w):
    return x * lax.rsqrt(jnp.mean(jnp.square(x), axis=-1, keepdims=True) + NORM_EPS) * w


def kernel(x_prompt, x_sample, cache_attn_k, cache_attn_v, state_mlstm_C, state_mlstm_n, state_mlstm_m, state_ssd, c, c_ctx, norm_mix, norm_ffn, w_mod, b_mod, attn_w_in, attn_q_norm, attn_k_norm, attn_lambda, attn_sub_norm, attn_w_out, mlstm_w_in, mlstm_conv_w, mlstm_conv_b, mlstm_w_gate, mlstm_b_gate, mlstm_head_norm, mlstm_w_out, ssd_w_in, ssd_conv_w, ssd_conv_b, ssd_dt_bias, ssd_a_log, ssd_d_skip, ssd_norm, ssd_w_out, hyena_w_in, hyena_conv_w, hyena_conv_b, hyena_f_w1, hyena_f_b1, hyena_f_w2, hyena_f_b2, hyena_f_w3, hyena_f_freq, hyena_skip_bias, hyena_w_out, moe_w_group, moe_b_group, moe_w_expert, moe_b_expert, moe_w_gate, moe_w_up, moe_w_down):
    batch, seq, dm = x_prompt.shape
    dec_batch, dec_seq, _ = x_sample.shape
    depth = norm_mix.shape[0]
    n_ctx = batch * seq
    n_lat = dec_batch * dec_seq
    rows = n_ctx + n_lat
    past = cache_attn_k.shape[2]
    a_heads, a_hd = cache_attn_k.shape[3], cache_attn_k.shape[5]
    m_heads, m_dk, m_dv = state_mlstm_C.shape[3:]
    s_heads, s_p, s_n = state_ssd.shape[3:]
    s_inner = s_heads * s_p
    s_groups = (ssd_conv_w.shape[-1] - s_inner) // (2 * s_n)
    s_hpg = s_heads // s_groups

    def row_map(r):
        return jnp.where(r < n_ctx, 0, 1 + (r - n_ctx) // dec_seq)

    per_seq = functools.partial(_per_seq, n_ctx=n_ctx, seq=seq, dec_seq=dec_seq)
    row_unit = math.gcd(n_ctx, dec_seq)
    tm_norm = _tile(row_unit, 256, 8)
    tm_res = _tile(row_unit, 1024, 8)

    n_cond = 1 + dec_batch
    cond = jnp.concatenate([c_ctx[None, :], c], axis=0)
    cond = jnp.pad(jax.nn.silu(cond), ((0, -n_cond % 8), (0, 0))).astype(BF16)

    x = jnp.concatenate([x_prompt.reshape(n_ctx, dm), x_sample.reshape(n_lat, dm)], axis=0)
    outs = {}
    for i in range(depth):
        kind, j = i % 4, i // 4
        mod = matmul(cond, w_mod, w_idx=i, bias=b_mod[i][None, :], tm=8, tn=1024, tk=2048, name="adaln_mod")
        sh1, sc1, g1, sh2, sc2, g2 = [mod[:, None, q * dm:(q + 1) * dm] for q in range(6)]
        (h1,) = norm_mod(x, norm_mix[i], sc1, sh1, row_map, [BF16], tm_norm)

        if kind == 0:
            lam_init = 0.8 - 0.6 * math.exp(-0.3 * i)
            qkv = matmul(h1, attn_w_in, w_idx=j, name="attn_in")
            shp = (rows, a_heads, 2, a_hd)
            q = _rms(qkv[:, :dm].reshape(shp), attn_q_norm[j])
            k = _rms(qkv[:, dm:2 * dm].reshape(shp), attn_k_norm[j])
            v = qkv[:, 2 * dm:]
            outs["k"] = k[:n_ctx].reshape(batch, 1, seq, a_heads, 2, a_hd)
            outs["v"] = v[:n_ctx].reshape(batch, 1, seq, a_heads, 2 * a_hd)
            tables = _rope_tables(dec_seq, a_hd)
            rope = lambda y: _axial_rope(y.reshape(dec_batch, dec_seq, a_heads, 2, a_hd), tables).reshape(n_lat, dm)
            q_lat, k_lat = rope(q[n_ctx:]), rope(k[n_ctx:])
            lv = attn_lambda[j]
            lam = jnp.exp(jnp.sum(lv[0] * lv[1])) - jnp.exp(jnp.sum(lv[2] * lv[3])) + lam_init
            k_all = jnp.concatenate([cache_attn_k[:, j].reshape(dec_batch, past, dm),
                                     k_lat.reshape(dec_batch, dec_seq, dm)], axis=1).reshape(-1, dm)
            v_all = jnp.concatenate([cache_attn_v[:, j].reshape(dec_batch, past, dm),
                                     v[n_ctx:].reshape(dec_batch, dec_seq, dm)], axis=1).reshape(-1, dm)
            o_ctx = diff_attention(q[:n_ctx].reshape(n_ctx, dm).astype(BF16), k[:n_ctx].reshape(n_ctx, dm).astype(BF16),
                                   v[:n_ctx].astype(BF16), lam, attn_sub_norm[j], nseq=batch, lq=seq, lk=seq,
                                   heads=a_heads, post_scale=1.0 - lam_init)
            o_lat = diff_attention(q_lat.astype(BF16), k_all.astype(BF16), v_all.astype(BF16), lam, attn_sub_norm[j],
                                   nseq=dec_batch, lq=dec_seq, lk=past + dec_seq, heads=a_heads,
                                   post_scale=1.0 - lam_init)
            mix_in, w_out = jnp.concatenate([o_ctx, o_lat], axis=0), attn_w_out

        elif kind == 1:
            qk_w = 2 * m_heads * m_dk
            v_w = m_heads * m_dv
            proj = matmul(h1, mlstm_w_in, w_idx=j, name="mlstm_in")
            qk = jax.nn.silu(per_seq(lambda y: _short_conv(y, mlstm_conv_w[j], mlstm_conv_b[j]), proj[:, :qk_w]))
            q = qk[:, :qk_w // 2].astype(BF16)
            k = (qk[:, qk_w // 2:] * (m_dk ** -0.5)).astype(BF16)
            v = proj[:, qk_w:qk_w + v_w].astype(BF16)
            o_gate = jax.nn.sigmoid(proj[:, qk_w + v_w:])
            g = matmul(h1, mlstm_w_gate, w_idx=j, x3=True, bias=mlstm_b_gate[j][None, :], tm=512, name="mlstm_gate")
            g = g.reshape(rows, 2, 2, m_heads)
            gates = jnp.stack([g[:, :, 0], jax.nn.log_sigmoid(g[:, :, 1])], axis=-1)
            gcol = gates.transpose(1, 2, 0, 3)

            def scan(lo, hi, nseq, length, init, emit):
                t = _tile(length, SCAN_CHUNK, 8)
                grow = gcol[:, :, lo:hi].reshape(2, m_heads, (hi - lo) // t, t, 2).transpose(0, 1, 2, 4, 3)
                return mlstm_scan(q[lo:hi], k[lo:hi], v[lo:hi], gcol[:, :, lo:hi], grow, init,
                                  nseq=nseq, length=length, heads=m_heads, emit_state=emit)

            h_ctx, c_f, n_f, m_f = scan(0, n_ctx, batch, seq, None, True)
            init = (state_mlstm_C[:, j], state_mlstm_n[:, j][:, :, :, None, :], state_mlstm_m[:, j][:, :, :, None, None])
            (h_lat,) = scan(n_ctx, rows, dec_batch, dec_seq, init, False)
            outs["C"] = c_f[:, None]
            outs["n"] = n_f[:, None, :, :, 0]
            outs["m"] = m_f[:, None, :, :, 0, 0]
            hsum = jnp.concatenate([h_ctx[0] + h_ctx[1], h_lat[0] + h_lat[1]], axis=0)
            y = _rms(hsum.reshape(rows, m_heads, m_dv), mlstm_head_norm[j]).reshape(rows, v_w) * o_gate
            mix_in, w_out = y.astype(BF16), mlstm_w_out

        elif kind == 2:
            gn = s_groups * s_n
            zx = matmul(h1, ssd_w_in, w_idx=j, n=2 * s_inner + 2 * gn, tn=512, name="ssd_in")
            dt_raw = matmul(h1, ssd_w_in, w_idx=j, n0=2 * s_inner + 2 * gn, n=2 * s_heads, x3=True, tm=512, name="ssd_dt")
            z = zx[:, :s_inner]
            xbc = jax.nn.silu(per_seq(lambda y: _short_conv(y, ssd_conv_w[j], ssd_conv_b[j]), zx[:, s_inner:]))
            xs = xbc[:, :s_inner]
            bmat = xbc[:, s_inner:s_inner + gn].astype(BF16)
            cmat = xbc[:, s_inner + gn:].astype(BF16)
            dt = jax.nn.softplus(dt_raw.reshape(rows, 2, s_heads) + ssd_dt_bias[j])
            a = dt * (-jnp.exp(ssd_a_log[j]))
            to_col = lambda y: y.reshape(rows, 2, s_groups, s_hpg).transpose(1, 2, 0, 3)
            acol, dtcol = to_col(a), to_col(dt)

            def scan(lo, hi, nseq, length, init, emit):
                t = _tile(length, SCAN_CHUNK, 8)
                arow = acol[:, :, lo:hi].reshape(2, s_groups, (hi - lo) // t, t, s_hpg).transpose(0, 1, 2, 4, 3)
                return ssd_scan(xs[lo:hi], bmat[lo:hi], cmat[lo:hi], acol[:, :, lo:hi], arow, dtcol[:, :, lo:hi], init,
                                nseq=nseq, length=length, groups=s_groups, hpg=s_hpg, emit_state=emit)

            y_ctx, s_f = scan(0, n_ctx, batch, seq, None, True)
            init = state_ssd[:, j].reshape(dec_batch, 2, s_groups, s_hpg * s_p, s_n)
            (y_lat,) = scan(n_ctx, rows, dec_batch, dec_seq, init, False)
            outs["ssd"] = s_f.reshape(batch, 1, 2, s_heads, s_p, s_n)
            y = jnp.concatenate([y_ctx[0] + y_ctx[1], y_lat[0] + y_lat[1]], axis=0)
            y = y + (xs.reshape(rows, s_heads, s_p) * ssd_d_skip[j][:, None]).reshape(rows, s_inner)
            y = _rms(y * jax.nn.silu(z), ssd_norm[j])
            mix_in, w_out = y.astype(BF16), ssd_w_out

        else:
            proj = matmul(h1, hyena_w_in, w_idx=j, name="hyena_in")
            proj = per_seq(lambda y: _short_conv(y, hyena_conv_w[j], hyena_conv_b[j]), proj)
            v, x1, x2 = proj[:, :dm], proj[:, dm:2 * dm], proj[:, 2 * dm:]
            parts = []
            for lo, hi, nseq, length in ((0, n_ctx, batch, seq), (n_ctx, rows, dec_batch, dec_seq)):
                fmat, gmat, tps, tqs = _hyena_filter_spectra(length, hyena_f_w1[j], hyena_f_b1[j], hyena_f_w2[j],
                                                             hyena_f_b2[j], hyena_f_w3[j], hyena_f_freq[j], dm)
                y = hyena_conv(v[lo:hi], fmat, gmat, tps[0], tqs[0], hyena_skip_bias[j][0], x1[lo:hi],
                               nseq=nseq, length=length, out_dtype=F32)
                y = hyena_conv(y, fmat, gmat, tps[1], tqs[1], hyena_skip_bias[j][1], x2[lo:hi],
                               nseq=nseq, length=length, out_dtype=BF16)
                parts.append(y)
            mix_in, w_out = jnp.concatenate(parts, axis=0), hyena_w_out

        x = matmul(mix_in, w_out, w_idx=j, resid=x, gate=g1, row_map=row_map, tm=tm_res, name="mixer_out")
        h2, h2b = norm_mod(x, norm_ffn[i], sc2, sh2, row_map, [F32, BF16], tm_norm)
        moe = hier_moe(h2, h2b, moe_w_group, moe_b_group, moe_w_expert, moe_b_expert,
                       moe_w_gate, moe_w_up, moe_w_down, i)
        g2_rows = jnp.concatenate([jnp.broadcast_to(g2[0], (n_ctx, dm)),
                                   jnp.repeat(g2[1:n_cond, 0], dec_seq, axis=0)], axis=0)
        x = x + g2_rows * moe

    y_prompt = x[:n_ctx].reshape(batch, seq, dm)
    y_sample = x[n_ctx:].reshape(dec_batch, dec_seq, dm)
    return (y_prompt, y_sample, outs["k"], outs["v"], outs["C"], outs["n"], outs["m"], outs["ssd"])
```

```python
import functools
import math

import jax
import jax.numpy as jnp
from jax import lax
from jax.experimental import pallas as pl
from jax.experimental.pallas import tpu as pltpu

F32 = jnp.float32
BF16 = jnp.bfloat16
NORM_EPS = 1e-6
ROPE_GRID_W = 64
ROPE_BASE = 10000.0
HYENA_EMB = 33
HYENA_DECAY_SHORT = 0.3
HYENA_DECAY_LONG = 1.5
HYENA_TARGET = 1e-2
MOE_TOP_K = 2
MOE_ROW_BLOCK = 256
SCAN_CHUNK = 256
VMEM_LIMIT = 56 * 1024 * 1024

_NT = (((1,), (1,)), ((), ()))
_TN = (((0,), (0,)), ((), ()))


def _tile(n, pref, align):
    if n <= pref:
        return n
    t = (pref // align) * align
    while t >= align:
        if n % t == 0:
            return t
        t -= align
    return n


def _params(sem):
    return pltpu.CompilerParams(dimension_semantics=sem, vmem_limit_bytes=VMEM_LIMIT)


def _dot(a, b):
    return jnp.dot(a, b, preferred_element_type=F32)


def _split_bf16(a):
    hi = a.astype(BF16)
    lo = (a - hi.astype(F32)).astype(BF16)
    return hi, lo


def _mm_kernel(x_ref, w_ref, *rest, nk, x3, mode):
    if mode == "resid":
        res_ref, gate_ref, o_ref, acc_ref = rest
    elif mode == "bias":
        b_ref, o_ref, acc_ref = rest
    else:
        o_ref, acc_ref = rest
    k = pl.program_id(2)

    @pl.when(k == 0)
    def _():
        acc_ref[...] = jnp.zeros_like(acc_ref)

    if x3:
        xh, xl = _split_bf16(x_ref[...].astype(F32))
        wh, wl = _split_bf16(w_ref[...].astype(F32))
        acc_ref[...] += _dot(xh, wh) + _dot(xh, wl) + _dot(xl, wh)
    else:
        acc_ref[...] += _dot(x_ref[...].astype(BF16), w_ref[...].astype(BF16))

    @pl.when(k == nk - 1)
    def _():
        r = acc_ref[...]
        if mode == "bias":
            r = r + b_ref[...]
        elif mode == "resid":
            r = res_ref[...] + gate_ref[...] * r
        o_ref[...] = r.astype(o_ref.dtype)


def matmul(x, w, *, w_idx=None, n0=0, n=None, tm=1024, tn=1024, tk=1024, out_dtype=F32, x3=False,
           bias=None, resid=None, gate=None, row_map=None, name="matmul"):
    m, kdim = x.shape
    n_full = w.shape[-1]
    n = n_full if n is None else n
    tm = _tile(m, tm, 8)
    tk = _tile(kdim, tk, 128)
    tn = _tile(n, tn, 128)
    assert n0 % tn == 0 and m % tm == 0 and kdim % tk == 0 and n % tn == 0
    j0 = n0 // tn
    nk = kdim // tk
    if w.ndim == 3:
        w_spec = pl.BlockSpec((None, tk, tn), lambda i, j, k: (w_idx, k, j + j0))
    else:
        w_spec = pl.BlockSpec((tk, tn), lambda i, j, k: (k, j + j0))
    in_specs = [pl.BlockSpec((tm, tk), lambda i, j, k: (i, k)), w_spec]
    args = [x, w]
    mode = "plain"
    if bias is not None:
        mode = "bias"
        in_specs.append(pl.BlockSpec((1, tn), lambda i, j, k: (0, j)))
        args.append(bias)
    elif resid is not None:
        mode = "resid"
        in_specs.append(pl.BlockSpec((tm, tn), lambda i, j, k: (i, j)))
        in_specs.append(pl.BlockSpec((None, 1, tn), lambda i, j, k: (row_map(i * tm), 0, j)))
        args += [resid, gate]
    return pl.pallas_call(
        functools.partial(_mm_kernel, nk=nk, x3=x3, mode=mode),
        grid=(m // tm, n // tn, nk),
        in_specs=in_specs,
        out_specs=pl.BlockSpec((tm, tn), lambda i, j, k: (i, j)),
        out_shape=jax.ShapeDtypeStruct((m, n), out_dtype),
        scratch_shapes=[pltpu.VMEM((tm, tn), F32)],
        compiler_params=_params(("parallel", "parallel", "arbitrary")),
        name=name,
    )(*args)


def _norm_mod_kernel(x_ref, w_ref, sc_ref, sh_ref, *o_refs):
    x = x_ref[...]
    y = x * lax.rsqrt(jnp.mean(x * x, axis=-1, keepdims=True) + NORM_EPS) * w_ref[...]
    h = y * (1.0 + sc_ref[...]) + sh_ref[...]
    for o_ref in o_refs:
        o_ref[...] = h.astype(o_ref.dtype)


def norm_mod(x, w, scale, shift, row_map, out_dtypes, tm):
    rows, d = x.shape
    row = pl.BlockSpec((tm, d), lambda i: (i, 0))
    mod = pl.BlockSpec((None, 1, d), lambda i: (row_map(i * tm), 0, 0))
    outs = pl.pallas_call(
        _norm_mod_kernel,
        grid=(rows // tm,),
        in_specs=[row, pl.BlockSpec((1, d), lambda i: (0, 0)), mod, mod],
        out_specs=[row for _ in out_dtypes],
        out_shape=[jax.ShapeDtypeStruct((rows, d), dt) for dt in out_dtypes],
        compiler_params=_params(("parallel",)),
        name="norm_mod",
    )(x, w.reshape(1, d), scale, shift)
    return outs


LANES = 128


def _store_slab(ref, val, slab):
    n = val.shape[0]
    for s in range(slab):
        ref[pl.ds(s, n, stride=slab), :] = val[:, s * LANES:(s + 1) * LANES]


def _load_slab_chunk(ref, s, n, slab):
    return ref[pl.ds(s, n, stride=slab), :]


def _load_slab(ref, n, slab):
    return jnp.concatenate([_load_slab_chunk(ref, s, n, slab) for s in range(slab)], axis=1)


def _norm_router_kernel(x_ref, w_ref, sc_ref, sh_ref, wr_ref, h_ref, lt_ref, *, slab):
    x = x_ref[...]
    y = x * lax.rsqrt(jnp.mean(x * x, axis=-1, keepdims=True) + NORM_EPS) * w_ref[...]
    h = y * (1.0 + sc_ref[...]) + sh_ref[...]
    _store_slab(h_ref, h, slab)
    hh, hl = _split_bf16(h)
    wh, wl = _split_bf16(wr_ref[...])
    nt = lambda a, b: lax.dot_general(a, b, _NT, preferred_element_type=F32)
    lt_ref[...] = nt(wh, hh) + nt(wh, hl) + nt(wl, hh)


def norm_mod_router(x, w, scale, shift, w_router_t, row_map, tm):
    rows, d = x.shape
    nr = w_router_t.shape[0]
    slab = d // LANES
    row = pl.BlockSpec((tm, d), lambda i: (i, 0))
    mod = pl.BlockSpec((None, 1, d), lambda i: (row_map(i * tm), 0, 0))
    return pl.pallas_call(
        functools.partial(_norm_router_kernel, slab=slab),
        grid=(rows // tm,),
        in_specs=[row, pl.BlockSpec((1, d), lambda i: (0, 0)), mod, mod, pl.BlockSpec((nr, d), lambda i: (0, 0))],
        out_specs=[pl.BlockSpec((tm * slab, LANES), lambda i: (i, 0)), pl.BlockSpec((nr, tm), lambda i: (0, i))],
        out_shape=[jax.ShapeDtypeStruct((rows * slab, LANES), F32), jax.ShapeDtypeStruct((nr, rows), F32)],
        compiler_params=_params(("parallel",)),
        name="norm_mod_router",
    )(x, w.reshape(1, d), scale, shift, w_router_t)


def _attn_kernel(lam_ref, q_ref, k_ref, v_ref, sn_ref, o_ref, *, half, scale, post_scale):
    q = q_ref[...].astype(F32)
    k = k_ref[...]
    v = v_ref[...]
    lane = lax.broadcasted_iota(jnp.int32, q.shape, 1)
    q1 = jnp.where(lane < half, q, 0.0).astype(BF16)
    q2 = jnp.where(lane >= half, q, 0.0).astype(BF16)

    def softmax(s):
        e = jnp.exp(s - jnp.max(s, axis=-1, keepdims=True))
        return e / jnp.sum(e, axis=-1, keepdims=True)

    p1 = softmax(lax.dot_general(q1, k, _NT, preferred_element_type=F32) * scale)
    p2 = softmax(lax.dot_general(q2, k, _NT, preferred_element_type=F32) * scale)
    a = p1 - lam_ref[0] * p2
    o = _dot(a.astype(BF16), v)
    o = o * lax.rsqrt(jnp.mean(o * o, axis=-1, keepdims=True) + NORM_EPS) * sn_ref[...] * post_scale
    o_ref[...] = o.astype(o_ref.dtype)


def diff_attention(q, k, v, lam, sub_norm, *, nseq, lq, lk, heads, post_scale):
    hd2 = q.shape[1] // heads
    tq = _tile(lq, 256, 8)
    nq = lq // tq
    return pl.pallas_call(
        functools.partial(_attn_kernel, half=hd2 // 2, scale=(hd2 // 2) ** -0.5, post_scale=post_scale),
        grid=(nseq, heads, nq),
        in_specs=[
            pl.BlockSpec(memory_space=pltpu.SMEM),
            pl.BlockSpec((tq, hd2), lambda s, h, i: (s * nq + i, h)),
            pl.BlockSpec((lk, hd2), lambda s, h, i: (s, h)),
            pl.BlockSpec((lk, hd2), lambda s, h, i: (s, h)),
            pl.BlockSpec((1, hd2), lambda s, h, i: (0, 0)),
        ],
        out_specs=pl.BlockSpec((tq, hd2), lambda s, h, i: (s * nq + i, h)),
        out_shape=jax.ShapeDtypeStruct(q.shape, BF16),
        compiler_params=_params(("parallel", "parallel", "parallel")),
        name="diff_attention",
    )(lam.reshape(1), q, k, v, sub_norm.reshape(1, hd2))


def _causal_masks(t, d):
    sign = jnp.where(d == 0, 1, -1)
    r = lax.broadcasted_iota(jnp.int32, (t, t), 0)
    s = lax.broadcasted_iota(jnp.int32, (t, t), 1)
    diff = (s - r) * sign
    return diff <= 0, diff >= 0


def _chunk_row(s, d, c, nc):
    return s * nc + jnp.where(d == 0, c, nc - 1 - c)


def _mlstm_kernel(*refs, t, nc, has_init, emit_state):
    refs = list(refs)
    q_ref, k_ref, v_ref, gcol_ref, grow_ref = refs[:5]
    pos = 5
    if has_init:
        c0_ref, n0_ref, m0_ref = refs[pos:pos + 3]
        pos += 3
    h_ref = refs[pos]
    pos += 1
    if emit_state:
        co_ref, no_ref, mo_ref = refs[pos:pos + 3]
        pos += 3
    c_s, n_s, m_s = refs[pos:pos + 3]
    d = pl.program_id(1)
    c = pl.program_id(3)

    @pl.when(c == 0)
    def _():
        if has_init:
            c_s[...] = c0_ref[...]
            n_s[...] = n0_ref[...]
            m_s[...] = m0_ref[...]
        else:
            c_s[...] = jnp.zeros_like(c_s)
            n_s[...] = jnp.zeros_like(n_s)
            m_s[...] = jnp.zeros_like(m_s)

    q = q_ref[...]
    k = k_ref[...]
    v = v_ref[...]
    i_col = gcol_ref[:, 0:1]
    f_col = gcol_ref[:, 1:2]
    i_row = grow_ref[0:1, :]
    f_row = grow_ref[1:2, :]
    causal, causal_t = _causal_masks(t, d)
    b_col = jnp.sum(jnp.where(causal, f_row, 0.0), axis=1, keepdims=True)
    b_row = jnp.sum(jnp.where(causal_t, f_col, 0.0), axis=0, keepdims=True)
    dlog = jnp.where(causal, b_col - b_row + i_row, -jnp.inf)
    m_prev = m_s[...]
    inter = b_col + m_prev
    mt = jnp.maximum(inter, jnp.max(dlog, axis=1, keepdims=True))
    w_intra = jnp.exp(dlog - mt)
    w_inter = jnp.exp(inter - mt)
    cm = c_s[...]
    nv = n_s[...]
    sm = lax.dot_general(q, k, _NT, preferred_element_type=F32) * w_intra
    num = _dot(sm.astype(BF16), v) + w_inter * _dot(q, cm.astype(BF16))
    den = jnp.sum(sm, axis=1, keepdims=True) + w_inter * jnp.sum(q.astype(F32) * nv, axis=1, keepdims=True)
    h_ref[...] = num / jnp.maximum(jnp.abs(den), jnp.exp(-mt))

    rowid = lax.broadcasted_iota(jnp.int32, (t, 1), 0)
    end_row = jnp.where(d == 0, t - 1, 0)
    m_new = jnp.sum(jnp.where(rowid == end_row, mt, 0.0), axis=0, keepdims=True)
    b_last = jnp.sum(f_row, axis=1, keepdims=True)
    kw = k.astype(F32) * jnp.exp(b_last - b_col + i_col - m_new)
    decay = jnp.exp(b_last + m_prev - m_new)
    c_new = decay * cm + lax.dot_general(kw.astype(BF16), v, _TN, preferred_element_type=F32)
    n_new = decay * nv + jnp.sum(kw, axis=0, keepdims=True)
    c_s[...] = c_new
    n_s[...] = n_new
    m_s[...] = m_new
    if emit_state:
        @pl.when(c == nc - 1)
        def _():
            co_ref[...] = c_new
            no_ref[...] = n_new
            mo_ref[...] = m_new


def mlstm_scan(q, k, v, gcol, grow, init, *, nseq, length, heads, emit_state):
    rows = q.shape[0]
    dk = q.shape[1] // heads
    dv = v.shape[1] // heads
    t = _tile(length, SCAN_CHUNK, 8)
    nc = length // t
    rc = lambda s, d, h, c: _chunk_row(s, d, c, nc)
    in_specs = [
        pl.BlockSpec((t, dk), lambda s, d, h, c: (rc(s, d, h, c), h)),
        pl.BlockSpec((t, dk), lambda s, d, h, c: (rc(s, d, h, c), h)),
        pl.BlockSpec((t, dv), lambda s, d, h, c: (rc(s, d, h, c), h)),
        pl.BlockSpec((None, None, t, 2), lambda s, d, h, c: (d, h, rc(s, d, h, c), 0)),
        pl.BlockSpec((None, None, None, 2, t), lambda s, d, h, c: (d, h, rc(s, d, h, c), 0, 0)),
    ]
    args = [q, k, v, gcol, grow]
    st_specs = [
        pl.BlockSpec((None, None, None, dk, dv), lambda s, d, h, c: (s, d, h, 0, 0)),
        pl.BlockSpec((None, None, None, 1, dk), lambda s, d, h, c: (s, d, h, 0, 0)),
        pl.BlockSpec((None, None, None, 1, 1), lambda s, d, h, c: (s, d, h, 0, 0)),
    ]
    if init is not None:
        in_specs += st_specs
        args += list(init)
    out_specs = [pl.BlockSpec((None, t, dv), lambda s, d, h, c: (d, rc(s, d, h, c), h))]
    out_shape = [jax.ShapeDtypeStruct((2, rows, heads * dv), F32)]
    if emit_state:
        out_specs += st_specs
        out_shape += [jax.ShapeDtypeStruct((nseq, 2, heads, dk, dv), F32),
                      jax.ShapeDtypeStruct((nseq, 2, heads, 1, dk), F32),
                      jax.ShapeDtypeStruct((nseq, 2, heads, 1, 1), F32)]
    return pl.pallas_call(
        functools.partial(_mlstm_kernel, t=t, nc=nc, has_init=init is not None, emit_state=emit_state),
        grid=(nseq, 2, heads, nc),
        in_specs=in_specs,
        out_specs=out_specs,
        out_shape=out_shape,
        scratch_shapes=[pltpu.VMEM((dk, dv), F32), pltpu.VMEM((1, dk), F32), pltpu.VMEM((1, 1), F32)],
        compiler_params=_params(("parallel", "parallel", "parallel", "arbitrary")),
        name="mlstm_scan",
    )(*args)


def _ssd_kernel(*refs, t, nc, hpg, p, has_init, emit_state):
    refs = list(refs)
    x_ref, b_ref, c_ref, acol_ref, arow_ref, dtcol_ref = refs[:6]
    pos = 6
    if has_init:
        s0_ref = refs[pos]
        pos += 1
    y_ref = refs[pos]
    pos += 1
    if emit_state:
        so_ref = refs[pos]
        pos += 1
    st = refs[pos]
    d = pl.program_id(1)
    c = pl.program_id(3)
    width = hpg * p

    @pl.when(c == 0)
    def _():
        if has_init:
            st[...] = s0_ref[...].T
        else:
            st[...] = jnp.zeros_like(st)

    causal, causal_t = _causal_masks(t, d)
    lane_head = lax.broadcasted_iota(jnp.int32, (1, width), 1) // p
    lane_pair = lax.broadcasted_iota(jnp.int32, (1, 2 * p), 1)
    bm = b_ref[...]
    cmat = c_ref[...]
    dt_e = jnp.zeros((t, width), F32)
    for e in range(hpg):
        dt_e = jnp.where(lane_head == e, dtcol_ref[:, e:e + 1], dt_e)
    xd = x_ref[...] * dt_e
    xdb = xd.astype(BF16)
    cb = lax.dot_general(cmat, bm, _NT, preferred_element_type=F32)
    cum_e = jnp.zeros((t, width), F32)
    tot_e = jnp.zeros((1, width), F32)
    y_pairs = []
    for e in range(hpg):
        a_row = arow_ref[e:e + 1, :]
        a_col = acol_ref[:, e:e + 1]
        cum_col = jnp.sum(jnp.where(causal, a_row, 0.0), axis=1, keepdims=True)
        cum_row = jnp.sum(jnp.where(causal_t, a_col, 0.0), axis=0, keepdims=True)
        wmat = (cb * jnp.exp(jnp.where(causal, cum_col - cum_row, -jnp.inf))).astype(BF16)
        j = e // 2
        yp = _dot(wmat, xdb[:, j * 2 * p:(j + 1) * 2 * p])
        if e % 2 == 0:
            y_even = yp
        else:
            y_pairs.append(jnp.where(lane_pair < p, y_even, yp))
        cum_e = jnp.where(lane_head == e, cum_col, cum_e)
        tot_e = jnp.where(lane_head == e, jnp.sum(a_row, axis=1, keepdims=True), tot_e)
    s_prev = st[...]
    y_off = _dot(cmat, s_prev.astype(BF16)) * jnp.exp(cum_e)
    y_ref[...] = jnp.concatenate(y_pairs, axis=1) + y_off
    xw = (xd * jnp.exp(tot_e - cum_e)).astype(BF16)
    s_new = jnp.exp(tot_e) * s_prev + lax.dot_general(bm, xw, _TN, preferred_element_type=F32)
    st[...] = s_new
    if emit_state:
        @pl.when(c == nc - 1)
        def _():
            so_ref[...] = s_new.T


def ssd_scan(x, bmat, cmat, acol, arow, dtcol, init, *, nseq, length, groups, hpg, emit_state):
    rows, inner = x.shape
    width = inner // groups
    p = width // hpg
    ns = bmat.shape[1] // groups
    t = _tile(length, SCAN_CHUNK, 8)
    nc = length // t
    rc = lambda s, d, g, c: _chunk_row(s, d, c, nc)
    in_specs = [
        pl.BlockSpec((t, width), lambda s, d, g, c: (rc(s, d, g, c), g)),
        pl.BlockSpec((t, ns), lambda s, d, g, c: (rc(s, d, g, c), g)),
        pl.BlockSpec((t, ns), lambda s, d, g, c: (rc(s, d, g, c), g)),
        pl.BlockSpec((None, None, t, hpg), lambda s, d, g, c: (d, g, rc(s, d, g, c), 0)),
        pl.BlockSpec((None, None, None, hpg, t), lambda s, d, g, c: (d, g, rc(s, d, g, c), 0, 0)),
        pl.BlockSpec((None, None, t, hpg), lambda s, d, g, c: (d, g, rc(s, d, g, c), 0)),
    ]
    args = [x, bmat, cmat, acol, arow, dtcol]
    st_spec = pl.BlockSpec((None, None, None, width, ns), lambda s, d, g, c: (s, d, g, 0, 0))
    if init is not None:
        in_specs.append(st_spec)
        args.append(init)
    out_specs = [pl.BlockSpec((None, t, width), lambda s, d, g, c: (d, rc(s, d, g, c), g))]
    out_shape = [jax.ShapeDtypeStruct((2, rows, inner), F32)]
    if emit_state:
        out_specs.append(st_spec)
        out_shape.append(jax.ShapeDtypeStruct((nseq, 2, groups, width, ns), F32))
    return pl.pallas_call(
        functools.partial(_ssd_kernel, t=t, nc=nc, hpg=hpg, p=p, has_init=init is not None, emit_state=emit_state),
        grid=(nseq, 2, groups, nc),
        in_specs=in_specs,
        out_specs=out_specs,
        out_shape=out_shape,
        scratch_shapes=[pltpu.VMEM((ns, width), F32)],
        compiler_params=_params(("parallel", "parallel", "parallel", "arbitrary")),
        name="ssd_scan",
    )(*args)


def _hyena_kernel(u_ref, f_ref, g_ref, tp_ref, tq_ref, bias_ref, xm_ref, o_ref, *, length):
    u = u_ref[...]
    a = _dot(f_ref[...], u.astype(BF16))
    a_sw = jnp.concatenate([a[length:], a[:length]], axis=0)
    y = a * tp_ref[...] + a_sw * tq_ref[...]
    conv = _dot(g_ref[...], y.astype(BF16))
    o_ref[...] = ((conv + u * bias_ref[...]) * xm_ref[...]).astype(o_ref.dtype)


def hyena_conv(u, fmat, gmat, tp, tq, bias, xm, *, nseq, length, out_dtype):
    rows, d = u.shape
    tc = _tile(d, 256, 128)
    n2 = 2 * length
    return pl.pallas_call(
        functools.partial(_hyena_kernel, length=length),
        grid=(d // tc, nseq),
        in_specs=[
            pl.BlockSpec((length, tc), lambda j, s: (s, j)),
            pl.BlockSpec((n2, length), lambda j, s: (0, 0)),
            pl.BlockSpec((length, n2), lambda j, s: (0, 0)),
            pl.BlockSpec((n2, tc), lambda j, s: (0, j)),
            pl.BlockSpec((n2, tc), lambda j, s: (0, j)),
            pl.BlockSpec((1, tc), lambda j, s: (0, j)),
            pl.BlockSpec((length, tc), lambda j, s: (s, j)),
        ],
        out_specs=pl.BlockSpec((length, tc), lambda j, s: (s, j)),
        out_shape=jax.ShapeDtypeStruct((rows, d), out_dtype),
        compiler_params=_params(("parallel", "parallel")),
        name="hyena_conv",
    )(u, fmat, gmat, tp, tq, bias.reshape(1, d), xm)


def _dft_matrices(length):
    n2 = 2 * length
    r = jnp.arange(n2, dtype=jnp.int32)
    kfreq = jnp.where(r <= length, r, r - length)
    ang = ((kfreq[:, None] * r[None, :]) % n2).astype(F32) * (2.0 * math.pi / n2)
    full = jnp.where((r > length)[:, None], -jnp.sin(ang), jnp.cos(ang))
    wk = jnp.where((r == 0) | (r == length), 1.0, 2.0) / n2
    inv = (full[:, :length] * wk[:, None]).T
    return full, full[:, :length].astype(BF16), inv.astype(BF16)


def _hyena_filter_spectra(length, f_w1, f_b1, f_w2, f_b2, f_w3, f_freq, d):
    hp = lax.Precision.HIGHEST
    t = jnp.linspace(0.0, 1.0, length, dtype=F32)[:, None]
    bands = (HYENA_EMB - 1) // 2
    f = jnp.linspace(1e-4, bands - 1, bands, dtype=F32)
    w = 2.0 * math.pi * jnp.arange(length, dtype=F32)[:, None] / length
    z = jnp.concatenate([t, jnp.cos(f * w), -jnp.sin(f * w)], axis=-1)
    hid = jnp.sin(f_freq * (jnp.dot(z, f_w1, precision=hp) + f_b1))
    hid = jnp.sin(f_freq * (jnp.dot(hid, f_w2, precision=hp) + f_b2))
    filt = matmul(hid, f_w3, x3=True, tk=hid.shape[1], name="hyena_filter").reshape(length, 2, 2, d)
    min_decay = math.log(HYENA_TARGET) / HYENA_DECAY_LONG
    max_decay = math.log(HYENA_TARGET) / HYENA_DECAY_SHORT
    deltas = jnp.linspace(min_decay, max_decay, d, dtype=F32)
    filt = filt * jnp.exp(-t * jnp.abs(deltas))[:, None, None, :]
    full, fmat, gmat = _dft_matrices(length)
    tps, tqs = [], []
    for o in range(2):
        h_fwd, h_bwd = filt[:, o, 0], filt[:, o, 1]
        taps = jnp.concatenate([h_fwd, jnp.zeros((1, d), F32), h_bwd[:0:-1]], axis=0)
        taps = taps / jnp.sum(jnp.abs(taps), axis=0, keepdims=True)
        spec = matmul(full, taps, x3=True, name="hyena_filter_dft")
        top, bot = spec[:length], spec[length:]
        zero = jnp.zeros((1, d), F32)
        im = jnp.concatenate([zero, bot[1:]], axis=0)
        tps.append(jnp.concatenate([top, bot[0:1], top[1:]], axis=0))
        tqs.append(jnp.concatenate([-im, im], axis=0))
    return fmat, gmat, tps, tqs


def _router_kernel(l_ref, b_ref, gates_ref, dest_ref, cnt_ref, pre_s, *, n_grp, epg, tb, chunk):
    n_exp = n_grp * epg
    rows = l_ref.shape[1]
    lg = l_ref[...] + b_ref[...]
    g = lg[n_exp:n_exp + n_grp]
    ge = jnp.exp(g - jnp.max(g, axis=0, keepdims=True))
    gp = ge / jnp.sum(ge, axis=0, keepdims=True)
    gpm = jnp.max(gp, axis=0, keepdims=True)
    gi = lax.broadcasted_iota(jnp.int32, gp.shape, 0)
    gsel = jnp.min(jnp.where(gp == gpm, gi, n_grp), axis=0, keepdims=True)
    e_in = lg[0:epg]
    for q in range(1, n_grp):
        e_in = jnp.where(gsel == q, lg[q * epg:(q + 1) * epg], e_in)
    ee = jnp.exp(e_in - jnp.max(e_in, axis=0, keepdims=True))
    ep = ee / jnp.sum(ee, axis=0, keepdims=True)
    ei = lax.broadcasted_iota(jnp.int32, ep.shape, 0)
    p1 = jnp.max(ep, axis=0, keepdims=True)
    s1 = jnp.min(jnp.where(ep == p1, ei, epg), axis=0, keepdims=True)
    ep2 = jnp.where(ei == s1, -1.0, ep)
    p2 = jnp.max(ep2, axis=0, keepdims=True)
    s2 = jnp.min(jnp.where(ep2 == p2, ei, epg), axis=0, keepdims=True)
    den = p1 + p2
    gates_ref[0:1, :] = gpm * p1 / den
    gates_ref[1:2, :] = gpm * p2 / den

    eio = lax.broadcasted_iota(jnp.int32, (n_exp, rows), 0)
    oh0 = eio == gsel * epg + s1
    oh1 = eio == gsel * epg + s2
    cnt = oh0.astype(F32) + oh1.astype(F32)
    ca = lax.broadcasted_iota(jnp.int32, (chunk, chunk), 0)
    cb = lax.broadcasted_iota(jnp.int32, (chunk, chunk), 1)
    tri = (ca < cb).astype(BF16)
    carry = jnp.zeros((n_exp, 1), F32)
    for c0 in range(0, rows, chunk):
        c = cnt[:, c0:c0 + chunk]
        pre_s[:, c0:c0 + chunk] = _dot(c.astype(BF16), tri) + carry
        carry = carry + jnp.sum(c, axis=1, keepdims=True)
    nblk = jnp.floor((carry + (tb - 1)) * (1.0 / tb))
    sa = lax.broadcasted_iota(jnp.int32, (n_exp, n_exp), 0)
    sb = lax.broadcasted_iota(jnp.int32, (n_exp, n_exp), 1)
    nblk_row = jnp.sum(jnp.where(sa == sb, nblk, 0.0), axis=0, keepdims=True)
    blk_start = jnp.sum(jnp.where(sb < sa, nblk_row, 0.0), axis=1, keepdims=True)
    slot = blk_start * tb + pre_s[...]
    dest_ref[0:1, :] = jnp.sum(jnp.where(oh0, slot, 0.0), axis=0, keepdims=True).astype(jnp.int32)
    dest_ref[1:2, :] = jnp.sum(jnp.where(oh1, slot, 0.0), axis=0, keepdims=True).astype(jnp.int32)
    cnt_ref[...] = carry.astype(jnp.int32)


def moe_route(logits_t, bias, *, n_grp, epg, tb):
    nr, rows = logits_t.shape
    n_exp = n_grp * epg
    assert tb & (tb - 1) == 0
    return pl.pallas_call(
        functools.partial(_router_kernel, n_grp=n_grp, epg=epg, tb=tb, chunk=_tile(rows, 512, 128)),
        out_shape=[jax.ShapeDtypeStruct((MOE_TOP_K, rows), F32),
                   jax.ShapeDtypeStruct((MOE_TOP_K, rows), jnp.int32),
                   jax.ShapeDtypeStruct((n_exp, 1), jnp.int32)],
        scratch_shapes=[pltpu.VMEM((n_exp, rows), F32)],
        compiler_params=pltpu.CompilerParams(vmem_limit_bytes=VMEM_LIMIT),
        name="moe_route",
    )(logits_t, bias)


def _slab_at(ref, row, n, slab):
    return ref.at[pl.ds(pl.multiple_of(row * slab, slab), n * slab)]


def _dispatch_kernel(dest_ref, cnt_ref, start_ref, nu_ref, h_ref, xs_ref, zero_s, sems, zsem, *,
                     rows, tm, tb, n_exp, n_blocks, slab, nsteps):
    i = pl.program_id(0)

    def zero_copy(row):
        return pltpu.make_async_copy(zero_s, _slab_at(xs_ref, row, tb, slab), zsem)

    @pl.when(i == 0)
    def _():
        zero_s[...] = jnp.zeros_like(zero_s)
        for wait in (False, True):
            def go(row):
                if wait:
                    zero_copy(row).wait()
                else:
                    zero_copy(row).start()

            def last_block(e, carry):
                @pl.when(cnt_ref[e] > 0)
                def _():
                    go(start_ref[e] + ((cnt_ref[e] - 1) & ~(tb - 1)))
                return carry

            def tail_block(b, carry):
                go(b * tb)
                return carry

            lax.fori_loop(0, n_exp, last_block, 0)
            lax.fori_loop(nu_ref[0], n_blocks, tail_block, 0)

    def row_copy(tok, slot, sem):
        return pltpu.make_async_copy(_slab_at(h_ref, tok, 1, slab), _slab_at(xs_ref, slot, 1, slab), sem)

    def issue(r, carry):
        tok = i * tm + r
        row_copy(tok, dest_ref[tok], sems.at[i % 2]).start()
        row_copy(tok, dest_ref[rows + tok], sems.at[i % 2]).start()
        return carry

    def drain(sem):
        def body(r, carry):
            row_copy(0, 0, sem).wait()
            row_copy(0, 0, sem).wait()
            return carry
        lax.fori_loop(0, tm, body, 0)

    lax.fori_loop(0, tm, issue, 0)

    @pl.when(i > 0)
    def _():
        drain(sems.at[(i + 1) % 2])

    @pl.when(i == nsteps - 1)
    def _():
        drain(sems.at[i % 2])


def _combine_kernel(dest_ref, x_ref, g2_ref, gt_ref, ys_ref, o_ref, buf, sem, *, rows, tm, slab):
    base = pl.program_id(0) * tm

    def row_copy(k, r, slot):
        return pltpu.make_async_copy(_slab_at(ys_ref, slot, 1, slab), _slab_at(buf.at[k], r, 1, slab), sem)

    def issue(r, carry):
        row_copy(0, r, dest_ref[base + r]).start()
        row_copy(1, r, dest_ref[rows + base + r]).start()
        return carry

    def drain(r, carry):
        row_copy(0, 0, 0).wait()
        row_copy(0, 0, 0).wait()
        return carry

    lax.fori_loop(0, tm, issue, 0)
    lax.fori_loop(0, tm, drain, 0)
    g0 = gt_ref[:, 0:1]
    g1 = gt_ref[:, 1:2]
    for s in range(slab):
        cols = slice(s * LANES, (s + 1) * LANES)
        y = g0 * buf[0, pl.ds(s, tm, stride=slab), :] + g1 * buf[1, pl.ds(s, tm, stride=slab), :]
        o_ref[:, cols] = x_ref[:, cols] + g2_ref[:, cols] * y


def _moe_up_kernel(be_ref, nu_ref, x_ref, wg_ref, wu_ref, o_ref, *, fchunk, slab):
    used = pl.program_id(0) < nu_ref[0]

    @pl.when(used)
    def _():
        x = _load_slab(x_ref, o_ref.shape[0], slab).astype(BF16)
        for f0 in range(0, o_ref.shape[1], fchunk):
            g = _dot(x, wg_ref[:, f0:f0 + fchunk].astype(BF16))
            u = _dot(x, wu_ref[:, f0:f0 + fchunk].astype(BF16))
            o_ref[:, f0:f0 + fchunk] = (g / (1.0 + jnp.exp(-g)) * u).astype(o_ref.dtype)

    @pl.when(jnp.logical_not(used))
    def _():
        o_ref[...] = jnp.zeros_like(o_ref)


def _moe_down_kernel(be_ref, nu_ref, h_ref, wd_ref, o_ref, *, slab):
    used = pl.program_id(0) < nu_ref[0]

    @pl.when(used)
    def _():
        _store_slab(o_ref, _dot(h_ref[...], wd_ref[...].astype(BF16)), slab)

    @pl.when(jnp.logical_not(used))
    def _():
        o_ref[...] = jnp.zeros_like(o_ref)


def moe_experts(x_slots, block_e, n_used, w_gate, w_up, w_down, layer):
    d, ff = w_gate.shape[-2:]
    slab = d // LANES
    n_slots = x_slots.shape[0] // slab
    tb = MOE_ROW_BLOCK
    n_blocks = n_slots // tb
    blk = lambda b, be, nu: (jnp.minimum(b, nu[0] - 1), 0)
    out_blk = lambda b, be, nu: (b, 0)
    wsel = lambda b, be, nu: (layer, be[b], 0, 0)
    hmid = pl.pallas_call(
        functools.partial(_moe_up_kernel, fchunk=_tile(ff, 256, 128), slab=slab),
        grid_spec=pltpu.PrefetchScalarGridSpec(
            num_scalar_prefetch=2, grid=(n_blocks,),
            in_specs=[pl.BlockSpec((tb * slab, LANES), blk),
                      pl.BlockSpec((None, None, d, ff), wsel),
                      pl.BlockSpec((None, None, d, ff), wsel)],
            out_specs=pl.BlockSpec((tb, ff), out_blk)),
        out_shape=jax.ShapeDtypeStruct((n_slots, ff), BF16),
        compiler_params=_params(("arbitrary",)),
        name="moe_gate_up",
    )(block_e, n_used, x_slots, w_gate, w_up)
    return pl.pallas_call(
        functools.partial(_moe_down_kernel, slab=slab),
        grid_spec=pltpu.PrefetchScalarGridSpec(
            num_scalar_prefetch=2, grid=(n_blocks,),
            in_specs=[pl.BlockSpec((tb, ff), blk),
                      pl.BlockSpec((None, None, ff, d), wsel)],
            out_specs=pl.BlockSpec((tb * slab, LANES), out_blk)),
        out_shape=jax.ShapeDtypeStruct((n_slots * slab, LANES), F32),
        compiler_params=_params(("arbitrary",)),
        name="moe_down",
    )(block_e, n_used, hmid, w_down)


def moe_router_weights(w_rg, b_rg, w_re, b_re, layer):
    w = jnp.concatenate([w_re[layer], w_rg[layer]], axis=-1).T
    b = jnp.concatenate([b_re[layer], b_rg[layer]])[:, None]
    pad = -w.shape[0] % 8
    return jnp.pad(w, ((0, pad), (0, 0))), jnp.pad(b, ((0, pad), (0, 0)))


def hier_moe_residual(x, h, logits_t, bias, g2, row_map, w_gate, w_up, w_down, layer, *, n_grp, tm):
    rows, d = x.shape
    slab = d // LANES
    nsteps = rows // tm
    n_exp = w_gate.shape[1]
    tb = MOE_ROW_BLOCK
    n_pair = rows * MOE_TOP_K
    n_blocks = -(-(n_pair + n_exp * (tb - 1)) // tb)
    gates_t, dest_t, counts = moe_route(logits_t, bias, n_grp=n_grp, epg=n_exp // n_grp, tb=tb)
    counts = counts[:, 0]
    blocks = (counts + tb - 1) // tb
    blk_end = jnp.cumsum(blocks)
    starts = ((blk_end - blocks) * tb).astype(jnp.int32)
    n_used = blk_end[-1].astype(jnp.int32)
    block_e = jnp.minimum(jnp.searchsorted(blk_end, jnp.arange(n_blocks), side="right"), n_exp - 1)
    block_e = jnp.where(jnp.arange(n_blocks) < n_used, block_e, block_e[n_used - 1]).astype(jnp.int32)
    dest = dest_t.reshape(-1)

    x_slots = pl.pallas_call(
        functools.partial(_dispatch_kernel, rows=rows, tm=tm, tb=tb, n_exp=n_exp, n_blocks=n_blocks,
                          slab=slab, nsteps=nsteps),
        grid_spec=pltpu.PrefetchScalarGridSpec(
            num_scalar_prefetch=4, grid=(nsteps,),
            in_specs=[pl.BlockSpec(memory_space=pl.ANY)],
            out_specs=pl.BlockSpec(memory_space=pl.ANY),
            scratch_shapes=[pltpu.VMEM((tb * slab, LANES), F32), pltpu.SemaphoreType.DMA((2,)),
                            pltpu.SemaphoreType.DMA]),
        out_shape=jax.ShapeDtypeStruct((n_blocks * tb * slab, LANES), F32),
        compiler_params=_params(("arbitrary",)),
        name="moe_dispatch",
    )(dest, counts, starts, n_used.reshape(1), h)
    y_slots = moe_experts(x_slots, block_e, n_used.reshape(1), w_gate, w_up, w_down, layer)
    return pl.pallas_call(
        functools.partial(_combine_kernel, rows=rows, tm=tm, slab=slab),
        grid_spec=pltpu.PrefetchScalarGridSpec(
            num_scalar_prefetch=1, grid=(nsteps,),
            in_specs=[pl.BlockSpec((tm, d), lambda i, *_: (i, 0)),
                      pl.BlockSpec((None, 1, d), lambda i, *_: (row_map(i * tm), 0, 0)),
                      pl.BlockSpec((tm, MOE_TOP_K), lambda i, *_: (i, 0)),
                      pl.BlockSpec(memory_space=pl.ANY)],
            out_specs=pl.BlockSpec((tm, d), lambda i, *_: (i, 0)),
            scratch_shapes=[pltpu.VMEM((MOE_TOP_K, tm * slab, LANES), F32), pltpu.SemaphoreType.DMA]),
        out_shape=jax.ShapeDtypeStruct((rows, d), F32),
        compiler_params=_params(("arbitrary",)),
        name="moe_combine",
    )(dest, x, g2, gates_t.T, y_slots)


def _per_seq(fn, x, n_ctx, seq, dec_seq):
    c = x.shape[-1]
    a = fn(x[:n_ctx].reshape(-1, seq, c)).reshape(n_ctx, -1)
    b = fn(x[n_ctx:].reshape(-1, dec_seq, c)).reshape(x.shape[0] - n_ctx, -1)
    return jnp.concatenate([a, b], axis=0)


def _short_conv(x, w, b):
    length = x.shape[1]
    prev = jnp.pad(x, ((0, 0), (1, 0), (0, 0)))[:, :length]
    nxt = jnp.pad(x, ((0, 0), (0, 1), (0, 0)))[:, 1:]
    return prev * w[0] + x * w[1] + nxt * w[2] + b


def _rope_tables(length, hd):
    rows = length // ROPE_GRID_W
    row = jnp.broadcast_to(jnp.arange(rows, dtype=F32)[:, None], (rows, ROPE_GRID_W)).reshape(length)
    col = jnp.broadcast_to(jnp.arange(ROPE_GRID_W, dtype=F32)[None, :], (rows, ROPE_GRID_W)).reshape(length)
    axis_dim = hd // 2
    inv_freq = ROPE_BASE ** (-jnp.arange(0, axis_dim, 2, dtype=F32) / axis_dim)
    ang_r = row[:, None] * inv_freq
    ang_c = col[:, None] * inv_freq
    return jnp.cos(ang_r), jnp.sin(ang_r), jnp.cos(ang_c), jnp.sin(ang_c)


def _axial_rope(x, tables):
    cos_r, sin_r, cos_c, sin_c = tables

    def rope_1d(y, cos, sin):
        y1, y2 = jnp.split(y, 2, axis=-1)
        cos = cos[:, None, None, :]
        sin = sin[:, None, None, :]
        return jnp.concatenate([y1 * cos - y2 * sin, y2 * cos + y1 * sin], axis=-1)

    x_row, x_col = jnp.split(x, 2, axis=-1)
    return jnp.concatenate([rope_1d(x_row, cos_r, sin_r), rope_1d(x_col, cos_c, sin_c)], axis=-1)


def _rms(x, w):
    return x * lax.rsqrt(jnp.mean(jnp.square(x), axis=-1, keepdims=True) + NORM_EPS) * w


def kernel(x_prompt, x_sample, cache_attn_k, cache_attn_v, state_mlstm_C, state_mlstm_n, state_mlstm_m, state_ssd, c, c_ctx, norm_mix, norm_ffn, w_mod, b_mod, attn_w_in, attn_q_norm, attn_k_norm, attn_lambda, attn_sub_norm, attn_w_out, mlstm_w_in, mlstm_conv_w, mlstm_conv_b, mlstm_w_gate, mlstm_b_gate, mlstm_head_norm, mlstm_w_out, ssd_w_in, ssd_conv_w, ssd_conv_b, ssd_dt_bias, ssd_a_log, ssd_d_skip, ssd_norm, ssd_w_out, hyena_w_in, hyena_conv_w, hyena_conv_b, hyena_f_w1, hyena_f_b1, hyena_f_w2, hyena_f_b2, hyena_f_w3, hyena_f_freq, hyena_skip_bias, hyena_w_out, moe_w_group, moe_b_group, moe_w_expert, moe_b_expert, moe_w_gate, moe_w_up, moe_w_down):
    batch, seq, dm = x_prompt.shape
    dec_batch, dec_seq, _ = x_sample.shape
    depth = norm_mix.shape[0]
    n_ctx = batch * seq
    n_lat = dec_batch * dec_seq
    rows = n_ctx + n_lat
    past = cache_attn_k.shape[2]
    a_heads, a_hd = cache_attn_k.shape[3], cache_attn_k.shape[5]
    m_heads, m_dk, m_dv = state_mlstm_C.shape[3:]
    s_heads, s_p, s_n = state_ssd.shape[3:]
    s_inner = s_heads * s_p
    s_groups = (ssd_conv_w.shape[-1] - s_inner) // (2 * s_n)
    s_hpg = s_heads // s_groups

    def row_map(r):
        return jnp.where(r < n_ctx, 0, 1 + (r - n_ctx) // dec_seq)

    per_seq = functools.partial(_per_seq, n_ctx=n_ctx, seq=seq, dec_seq=dec_seq)
    row_unit = math.gcd(n_ctx, dec_seq)
    tm_norm = _tile(row_unit, 256, 8)
    tm_res = _tile(row_unit, 1024, 8)

    n_cond = 1 + dec_batch
    cond = jnp.concatenate([c_ctx[None, :], c], axis=0)
    cond = jnp.pad(jax.nn.silu(cond), ((0, -n_cond % 8), (0, 0))).astype(BF16)

    x = jnp.concatenate([x_prompt.reshape(n_ctx, dm), x_sample.reshape(n_lat, dm)], axis=0)
    outs = {}
    for i in range(depth):
        kind, j = i % 4, i // 4
        mod = matmul(cond, w_mod, w_idx=i, bias=b_mod[i][None, :], tm=8, tn=1024, tk=2048, name="adaln_mod")
        sh1, sc1, g1, sh2, sc2, g2 = [mod[:, None, q * dm:(q + 1) * dm] for q in range(6)]
        (h1,) = norm_mod(x, norm_mix[i], sc1, sh1, row_map, [BF16], tm_norm)

        if kind == 0:
            lam_init = 0.8 - 0.6 * math.exp(-0.3 * i)
            qkv = matmul(h1, attn_w_in, w_idx=j, name="attn_in")
            shp = (rows, a_heads, 2, a_hd)
            q = _rms(qkv[:, :dm].reshape(shp), attn_q_norm[j])
            k = _rms(qkv[:, dm:2 * dm].reshape(shp), attn_k_norm[j])
            v = qkv[:, 2 * dm:]
            outs["k"] = k[:n_ctx].reshape(batch, 1, seq, a_heads, 2, a_hd)
            outs["v"] = v[:n_ctx].reshape(batch, 1, seq, a_heads, 2 * a_hd)
            tables = _rope_tables(dec_seq, a_hd)
            rope = lambda y: _axial_rope(y.reshape(dec_batch, dec_seq, a_heads, 2, a_hd), tables).reshape(n_lat, dm)
            q_lat, k_lat = rope(q[n_ctx:]), rope(k[n_ctx:])
            lv = attn_lambda[j]
            lam = jnp.exp(jnp.sum(lv[0] * lv[1])) - jnp.exp(jnp.sum(lv[2] * lv[3])) + lam_init
            k_all = jnp.concatenate([cache_attn_k[:, j].reshape(dec_batch, past, dm),
                                     k_lat.reshape(dec_batch, dec_seq, dm)], axis=1).reshape(-1, dm)
            v_all = jnp.concatenate([cache_attn_v[:, j].reshape(dec_batch, past, dm),
                                     v[n_ctx:].reshape(dec_batch, dec_seq, dm)], axis=1).reshape(-1, dm)
            o_ctx = diff_attention(q[:n_ctx].reshape(n_ctx, dm).astype(BF16), k[:n_ctx].reshape(n_ctx, dm).astype(BF16),
                                   v[:n_ctx].astype(BF16), lam, attn_sub_norm[j], nseq=batch, lq=seq, lk=seq,
                                   heads=a_heads, post_scale=1.0 - lam_init)
            o_lat = diff_attention(q_lat.astype(BF16), k_all.astype(BF16), v_all.astype(BF16), lam, attn_sub_norm[j],
                                   nseq=dec_batch, lq=dec_seq, lk=past + dec_seq, heads=a_heads,
                                   post_scale=1.0 - lam_init)
            mix_in, w_out = jnp.concatenate([o_ctx, o_lat], axis=0), attn_w_out

        elif kind == 1:
            qk_w = 2 * m_heads * m_dk
            v_w = m_heads * m_dv
            proj = matmul(h1, mlstm_w_in, w_idx=j, name="mlstm_in")
            qk = jax.nn.silu(per_seq(lambda y: _short_conv(y, mlstm_conv_w[j], mlstm_conv_b[j]), proj[:, :qk_w]))
            q = qk[:, :qk_w // 2].astype(BF16)
            k = (qk[:, qk_w // 2:] * (m_dk ** -0.5)).astype(BF16)
            v = proj[:, qk_w:qk_w + v_w].astype(BF16)
            o_gate = jax.nn.sigmoid(proj[:, qk_w + v_w:])
            g = matmul(h1, mlstm_w_gate, w_idx=j, x3=True, bias=mlstm_b_gate[j][None, :], tm=512, name="mlstm_gate")
            g = g.reshape(rows, 2, 2, m_heads)
            gates = jnp.stack([g[:, :, 0], jax.nn.log_sigmoid(g[:, :, 1])], axis=-1)
            gcol = gates.transpose(1, 2, 0, 3)

            def scan(lo, hi, nseq, length, init, emit):
                t = _tile(length, SCAN_CHUNK, 8)
                grow = gcol[:, :, lo:hi].reshape(2, m_heads, (hi - lo) // t, t, 2).transpose(0, 1, 2, 4, 3)
                return mlstm_scan(q[lo:hi], k[lo:hi], v[lo:hi], gcol[:, :, lo:hi], grow, init,
                                  nseq=nseq, length=length, heads=m_heads, emit_state=emit)

            h_ctx, c_f, n_f, m_f = scan(0, n_ctx, batch, seq, None, True)
            init = (state_mlstm_C[:, j], state_mlstm_n[:, j][:, :, :, None, :], state_mlstm_m[:, j][:, :, :, None, None])
            (h_lat,) = scan(n_ctx, rows, dec_batch, dec_seq, init, False)
            outs["C"] = c_f[:, None]
            outs["n"] = n_f[:, None, :, :, 0]
            outs["m"] = m_f[:, None, :, :, 0, 0]
            hsum = jnp.concatenate([h_ctx[0] + h_ctx[1], h_lat[0] + h_lat[1]], axis=0)
            y = _rms(hsum.reshape(rows, m_heads, m_dv), mlstm_head_norm[j]).reshape(rows, v_w) * o_gate
            mix_in, w_out = y.astype(BF16), mlstm_w_out

        elif kind == 2:
            gn = s_groups * s_n
            zx = matmul(h1, ssd_w_in, w_idx=j, n=2 * s_inner + 2 * gn, tn=512, name="ssd_in")
            dt_raw = matmul(h1, ssd_w_in, w_idx=j, n0=2 * s_inner + 2 * gn, n=2 * s_heads, x3=True, tm=512, name="ssd_dt")
            z = zx[:, :s_inner]
            xbc = jax.nn.silu(per_seq(lambda y: _short_conv(y, ssd_conv_w[j], ssd_conv_b[j]), zx[:, s_inner:]))
            xs = xbc[:, :s_inner]
            bmat = xbc[:, s_inner:s_inner + gn].astype(BF16)
            cmat = xbc[:, s_inner + gn:].astype(BF16)
            dt = jax.nn.softplus(dt_raw.reshape(rows, 2, s_heads) + ssd_dt_bias[j])
            a = dt * (-jnp.exp(ssd_a_log[j]))
            to_col = lambda y: y.reshape(rows, 2, s_groups, s_hpg).transpose(1, 2, 0, 3)
            acol, dtcol = to_col(a), to_col(dt)

            def scan(lo, hi, nseq, length, init, emit):
                t = _tile(length, SCAN_CHUNK, 8)
                arow = acol[:, :, lo:hi].reshape(2, s_groups, (hi - lo) // t, t, s_hpg).transpose(0, 1, 2, 4, 3)
                return ssd_scan(xs[lo:hi], bmat[lo:hi], cmat[lo:hi], acol[:, :, lo:hi], arow, dtcol[:, :, lo:hi], init,
                                nseq=nseq, length=length, groups=s_groups, hpg=s_hpg, emit_state=emit)

            y_ctx, s_f = scan(0, n_ctx, batch, seq, None, True)
            init = state_ssd[:, j].reshape(dec_batch, 2, s_groups, s_hpg * s_p, s_n)
            (y_lat,) = scan(n_ctx, rows, dec_batch, dec_seq, init, False)
            outs["ssd"] = s_f.reshape(batch, 1, 2, s_heads, s_p, s_n)
            y = jnp.concatenate([y_ctx[0] + y_ctx[1], y_lat[0] + y_lat[1]], axis=0)
            y = y + (xs.reshape(rows, s_heads, s_p) * ssd_d_skip[j][:, None]).reshape(rows, s_inner)
            y = _rms(y * jax.nn.silu(z), ssd_norm[j])
            mix_in, w_out = y.astype(BF16), ssd_w_out

        else:
            proj = matmul(h1, hyena_w_in, w_idx=j, name="hyena_in")
            proj = per_seq(lambda y: _short_conv(y, hyena_conv_w[j], hyena_conv_b[j]), proj)
            v, x1, x2 = proj[:, :dm], proj[:, dm:2 * dm], proj[:, 2 * dm:]
            parts = []
            for lo, hi, nseq, length in ((0, n_ctx, batch, seq), (n_ctx, rows, dec_batch, dec_seq)):
                fmat, gmat, tps, tqs = _hyena_filter_spectra(length, hyena_f_w1[j], hyena_f_b1[j], hyena_f_w2[j],
                                                             hyena_f_b2[j], hyena_f_w3[j], hyena_f_freq[j], dm)
                y = hyena_conv(v[lo:hi], fmat, gmat, tps[0], tqs[0], hyena_skip_bias[j][0], x1[lo:hi],
                               nseq=nseq, length=length, out_dtype=F32)
                y = hyena_conv(y, fmat, gmat, tps[1], tqs[1], hyena_skip_bias[j][1], x2[lo:hi],
                               nseq=nseq, length=length, out_dtype=BF16)
                parts.append(y)
            mix_in, w_out = jnp.concatenate(parts, axis=0), hyena_w_out

        x = matmul(mix_in, w_out, w_idx=j, resid=x, gate=g1, row_map=row_map, tm=tm_res, name="mixer_out")
        w_rt, b_rt = moe_router_weights(moe_w_group, moe_b_group, moe_w_expert, moe_b_expert, i)
        h2, logits_t = norm_mod_router(x, norm_ffn[i], sc2, sh2, w_rt, row_map, tm_norm)
        x = hier_moe_residual(x, h2, logits_t, b_rt, g2, row_map, moe_w_gate, moe_w_up, moe_w_down, i,
                              n_grp=moe_w_group.shape[-1], tm=tm_norm)

    y_prompt = x[:n_ctx].reshape(batch, seq, dm)
    y_sample = x[n_ctx:].reshape(dec_batch, dec_seq, dm)
    return (y_prompt, y_sample, outs["k"], outs["v"], outs["C"], outs["n"], outs["m"], outs["ssd"])
```

```python
import functools
import math

import jax
import jax.numpy as jnp
from jax import lax
from jax.experimental import pallas as pl
from jax.experimental.pallas import tpu as pltpu

F32 = jnp.float32
BF16 = jnp.bfloat16
LANES = 128
NORM_EPS = 1e-6
ROPE_GRID_W = 64
ROPE_BASE = 10000.0
HYENA_EMB = 33
HYENA_DECAY_SHORT = 0.3
HYENA_DECAY_LONG = 1.5
HYENA_TARGET = 1e-2
MOE_TOP_K = 2
MOE_ROW_BLOCK = 256
SCAN_CHUNK = 256
VMEM_LIMIT = 56 * 1024 * 1024

_NT = (((1,), (1,)), ((), ()))
_TN = (((0,), (0,)), ((), ()))


def _tile(n, pref, align):
    if n <= pref:
        return n
    t = (pref // align) * align
    while t >= align:
        if n % t == 0:
            return t
        t -= align
    return n


def _params(sem):
    return pltpu.CompilerParams(dimension_semantics=sem, vmem_limit_bytes=VMEM_LIMIT)


def _dot(a, b):
    return jnp.dot(a, b, preferred_element_type=F32)


def _split_bf16(a):
    hi = a.astype(BF16)
    lo = (a - hi.astype(F32)).astype(BF16)
    return hi, lo


def _silu(x):
    return x / (1.0 + jnp.exp(-x))


def _mm_kernel(*refs, nk, x3, n_extra, n_out, epi):
    x_ref, w_ref = refs[:2]
    extra = refs[2:2 + n_extra]
    o_refs = refs[2 + n_extra:2 + n_extra + n_out]
    acc_ref = refs[-1]
    k = pl.program_id(2)

    @pl.when(k == 0)
    def _():
        acc_ref[...] = jnp.zeros_like(acc_ref)

    if x3:
        xh, xl = _split_bf16(x_ref[...].astype(F32))
        wh, wl = _split_bf16(w_ref[...].astype(F32))
        acc_ref[...] += _dot(xh, wh) + _dot(xh, wl) + _dot(xl, wh)
    else:
        acc_ref[...] += _dot(x_ref[...].astype(BF16), w_ref[...].astype(BF16))

    @pl.when(k == nk - 1)
    def _():
        for o_ref, val in zip(o_refs, epi(acc_ref[...], extra, pl.program_id(0))):
            o_ref[...] = val.astype(o_ref.dtype)


def _epi_plain(r, extra, i):
    return (r,)


def _epi_bias(r, extra, i):
    return (r + extra[0][...],)


def _epi_resid(r, extra, i):
    return (extra[0][...] + extra[1][...] * r,)


def _epi_sigmoid(r, extra, i):
    return (1.0 / (1.0 + jnp.exp(-r)),)


def _epi_two(r, extra, i):
    return (r, r)


def matmul(x, w, *, w_idx=None, n0=0, n=None, tm=1024, tn=1024, tk=1024, out_dtypes=(F32,), x3=False,
           extras=(), epi=_epi_plain, name="matmul"):
    m, kdim = x.shape
    n = w.shape[-1] if n is None else n
    tm = _tile(m, tm, 8)
    tk = _tile(kdim, tk, LANES)
    tn = _tile(n, tn, LANES)
    assert n0 % tn == 0 and m % tm == 0 and kdim % tk == 0 and n % tn == 0
    j0 = n0 // tn
    nk = kdim // tk
    if w.ndim == 3:
        w_spec = pl.BlockSpec((None, tk, tn), lambda i, j, k: (w_idx, k, j + j0))
    else:
        w_spec = pl.BlockSpec((tk, tn), lambda i, j, k: (k, j + j0))
    in_specs = [pl.BlockSpec((tm, tk), lambda i, j, k: (i, k)), w_spec]
    args = [x, w]
    for arr, shape, imap in extras:
        in_specs.append(pl.BlockSpec(shape, lambda i, j, k, imap=imap: imap(i, j)))
        args.append(arr)
    outs = pl.pallas_call(
        functools.partial(_mm_kernel, nk=nk, x3=x3, n_extra=len(extras), n_out=len(out_dtypes), epi=epi),
        grid=(m // tm, n // tn, nk),
        in_specs=in_specs,
        out_specs=[pl.BlockSpec((tm, tn), lambda i, j, k: (i, j)) for _ in out_dtypes],
        out_shape=[jax.ShapeDtypeStruct((m, n), dt) for dt in out_dtypes],
        scratch_shapes=[pltpu.VMEM((tm, tn), F32)],
        compiler_params=_params(("parallel", "parallel", "arbitrary")),
        name=name,
    )(*args)
    return outs[0] if len(out_dtypes) == 1 else outs


def _row_vec(v, tn, col0=0):
    return (v.reshape(1, -1), (1, tn), lambda i, j, c=col0 // tn: (0, j + c))


def _make_epi_conv(tm, seq_of_tile, act):
    def epi(r, extra, i):
        taps = extra[0][...]
        length = seq_of_tile(i)
        pos = lax.broadcasted_iota(jnp.int32, (tm, 1), 0) & (length - 1)
        prev = jnp.where(pos == 0, 0.0, pltpu.roll(r, 1, 0))
        nxt = jnp.where(pos == length - 1, 0.0, pltpu.roll(r, tm - 1, 0))
        y = prev * taps[0:1] + r * taps[1:2] + nxt * taps[2:3] + extra[1][...]
        if act:
            y = _silu(y)
        return (y * extra[2][...],)
    return epi


def _make_epi_qk(tm, tn, hd, n_ctx, n_out):
    def epi(r, extra, i):
        ga = lax.broadcasted_iota(jnp.int32, (LANES, LANES), 0) // hd
        gb = lax.broadcasted_iota(jnp.int32, (LANES, LANES), 1) // hd
        avg = jnp.where(ga == gb, 1.0 / hd, 0.0).astype(BF16)
        lane = lax.broadcasted_iota(jnp.int32, (1, LANES), 1)
        first_half = (lane & (hd // 2 - 1)) < hd // 4
        rope_on = i * tm >= n_ctx
        cos = extra[1][...]
        sin = extra[2][...]
        rot, plain = [], []
        for c0 in range(0, tn, LANES):
            xc = r[:, c0:c0 + LANES]
            hi, lo = _split_bf16(xc * xc)
            y = xc * lax.rsqrt(_dot(hi, avg) + _dot(lo, avg) + NORM_EPS) * extra[0][:, c0:c0 + LANES]
            partner = jnp.where(first_half, pltpu.roll(y, LANES - hd // 4, 1), pltpu.roll(y, hd // 4, 1))
            rot.append(jnp.where(rope_on, y * cos + partner * sin, y))
            plain.append(y)
        outs = (jnp.concatenate(rot, axis=1),)
        if n_out == 2:
            outs += (jnp.concatenate(plain, axis=1),)
        return outs
    return epi


def _norm_mod_kernel(x_ref, w_ref, sc_ref, sh_ref, o_ref):
    x = x_ref[...]
    y = x * lax.rsqrt(jnp.mean(x * x, axis=-1, keepdims=True) + NORM_EPS) * w_ref[...]
    o_ref[...] = (y * (1.0 + sc_ref[...]) + sh_ref[...]).astype(o_ref.dtype)


def norm_mod(x, w, scale, shift, row_map, tm):
    rows, d = x.shape
    row = pl.BlockSpec((tm, d), lambda i: (i, 0))
    mod = pl.BlockSpec((None, 1, d), lambda i: (row_map(i * tm), 0, 0))
    return pl.pallas_call(
        _norm_mod_kernel,
        grid=(rows // tm,),
        in_specs=[row, pl.BlockSpec((1, d), lambda i: (0, 0)), mod, mod],
        out_specs=row,
        out_shape=jax.ShapeDtypeStruct((rows, d), BF16),
        compiler_params=_params(("parallel",)),
        name="norm_mod",
    )(x, w.reshape(1, d), scale, shift)


def _store_slab(ref, val, slab):
    n = val.shape[0]
    for s in range(slab):
        ref[pl.ds(s, n, stride=slab), :] = val[:, s * LANES:(s + 1) * LANES]


def _load_slab(ref, n, slab):
    return jnp.concatenate([ref[pl.ds(s, n, stride=slab), :] for s in range(slab)], axis=1)


def _slab_at(ref, row, n, slab):
    return ref.at[pl.ds(pl.multiple_of(row * slab, slab), n * slab)]


def _norm_router_kernel(x_ref, w_ref, sc_ref, sh_ref, wr_ref, h_ref, lt_ref, *, slab):
    x = x_ref[...]
    y = x * lax.rsqrt(jnp.mean(x * x, axis=-1, keepdims=True) + NORM_EPS) * w_ref[...]
    h = y * (1.0 + sc_ref[...]) + sh_ref[...]
    _store_slab(h_ref, h, slab)
    hh, hl = _split_bf16(h)
    wh, wl = _split_bf16(wr_ref[...])
    nt = lambda a, b: lax.dot_general(a, b, _NT, preferred_element_type=F32)
    lt_ref[...] = nt(wh, hh) + nt(wh, hl) + nt(wl, hh)


def norm_mod_router(x, w, scale, shift, w_router_t, row_map, tm):
    rows, d = x.shape
    nr = w_router_t.shape[0]
    slab = d // LANES
    row = pl.BlockSpec((tm, d), lambda i: (i, 0))
    mod = pl.BlockSpec((None, 1, d), lambda i: (row_map(i * tm), 0, 0))
    return pl.pallas_call(
        functools.partial(_norm_router_kernel, slab=slab),
        grid=(rows // tm,),
        in_specs=[row, pl.BlockSpec((1, d), lambda i: (0, 0)), mod, mod, pl.BlockSpec((nr, d), lambda i: (0, 0))],
        out_specs=[pl.BlockSpec((tm * slab, LANES), lambda i: (i, 0)), pl.BlockSpec((nr, tm), lambda i: (0, i))],
        out_shape=[jax.ShapeDtypeStruct((rows * slab, LANES), F32), jax.ShapeDtypeStruct((nr, rows), F32)],
        compiler_params=_params(("parallel",)),
        name="norm_mod_router",
    )(x, w.reshape(1, d), scale, shift, w_router_t)


def _attn_kernel(lam_ref, q_ref, k_ref, v_ref, *rest, half, scale, post_scale, has_cache):
    if has_cache:
        kc_ref, vc_ref, sn_ref, o_ref = rest
    else:
        sn_ref, o_ref = rest
    q = q_ref[...].astype(F32)
    k = k_ref[...]
    lane = lax.broadcasted_iota(jnp.int32, q.shape, 1)
    nt = lambda a, b: lax.dot_general(a, b, _NT, preferred_element_type=F32)
    if has_cache:
        kc = kc_ref[...].astype(BF16)

    def probs(qm):
        s = nt(qm, k) * scale
        m = jnp.max(s, axis=-1, keepdims=True)
        if not has_cache:
            e = jnp.exp(s - m)
            return e / jnp.sum(e, axis=-1, keepdims=True), None
        sc = nt(qm, kc) * scale
        m = jnp.maximum(m, jnp.max(sc, axis=-1, keepdims=True))
        e = jnp.exp(s - m)
        ec = jnp.exp(sc - m)
        den = jnp.sum(e, axis=-1, keepdims=True) + jnp.sum(ec, axis=-1, keepdims=True)
        return e / den, ec / den

    p1, p1c = probs(jnp.where(lane < half, q, 0.0).astype(BF16))
    p2, p2c = probs(jnp.where(lane >= half, q, 0.0).astype(BF16))
    lam = lam_ref[0]
    o = _dot((p1 - lam * p2).astype(BF16), v_ref[...])
    if has_cache:
        o = o + _dot((p1c - lam * p2c).astype(BF16), vc_ref[...].astype(BF16))
    o = o * lax.rsqrt(jnp.mean(o * o, axis=-1, keepdims=True) + NORM_EPS) * sn_ref[...] * post_scale
    o_ref[...] = o.astype(o_ref.dtype)


def diff_attention(q, k, v, cache, lam, sub_norm, *, row0, nseq, length, heads, post_scale):
    hd2 = q.shape[1] // heads
    tq = _tile(length, 256, 8)
    nq = length // tq
    in_specs = [
        pl.BlockSpec(memory_space=pltpu.SMEM),
        pl.BlockSpec((tq, hd2), lambda s, h, i: (row0 // tq + s * nq + i, h)),
        pl.BlockSpec((length, hd2), lambda s, h, i: (row0 // length + s, h)),
        pl.BlockSpec((length, hd2), lambda s, h, i: (row0 // length + s, h)),
    ]
    args = [lam.reshape(1), q, k, v]
    if cache is not None:
        past = cache[0].shape[0] // nseq
        in_specs += [pl.BlockSpec((past, hd2), lambda s, h, i: (s, h))] * 2
        args += list(cache)
    in_specs.append(pl.BlockSpec((1, hd2), lambda s, h, i: (0, 0)))
    args.append(sub_norm.reshape(1, hd2))
    return pl.pallas_call(
        functools.partial(_attn_kernel, half=hd2 // 2, scale=(hd2 // 2) ** -0.5, post_scale=post_scale,
                          has_cache=cache is not None),
        grid=(nseq, heads, nq),
        in_specs=in_specs,
        out_specs=pl.BlockSpec((tq, hd2), lambda s, h, i: (s * nq + i, h)),
        out_shape=jax.ShapeDtypeStruct((nseq * length, q.shape[1]), BF16),
        compiler_params=_params(("parallel", "parallel", "parallel")),
        name="diff_attention",
    )(*args)


def _causal_masks(t, d):
    sign = jnp.where(d == 0, 1, -1)
    r = lax.broadcasted_iota(jnp.int32, (t, t), 0)
    s = lax.broadcasted_iota(jnp.int32, (t, t), 1)
    diff = (s - r) * sign
    return diff <= 0, diff >= 0


def _chunk_row(s, d, c, nc):
    return s * nc + jnp.where(d == 0, c, nc - 1 - c)


def _mlstm_kernel(*refs, t, nc, has_init, emit_state):
    refs = list(refs)
    q_ref, k_ref, v_ref, gcol_ref, grow_ref = refs[:5]
    pos = 5
    if has_init:
        c0_ref, n0_ref, m0_ref = refs[pos:pos + 3]
        pos += 3
    h_ref = refs[pos]
    pos += 1
    if emit_state:
        co_ref, no_ref, mo_ref = refs[pos:pos + 3]
        pos += 3
    c_s, n_s, m_s = refs[pos:pos + 3]
    d = pl.program_id(1)
    c = pl.program_id(3)

    @pl.when(c == 0)
    def _():
        if has_init:
            c_s[...] = c0_ref[...]
            n_s[...] = n0_ref[...]
            m_s[...] = m0_ref[...]
        else:
            c_s[...] = jnp.zeros_like(c_s)
            n_s[...] = jnp.zeros_like(n_s)
            m_s[...] = jnp.zeros_like(m_s)

    q = q_ref[...]
    k = k_ref[...]
    v = v_ref[...]
    i_col = gcol_ref[:, 0:1]
    f_col = gcol_ref[:, 1:2]
    i_row = grow_ref[0:1, :]
    f_row = grow_ref[1:2, :]
    causal, causal_t = _causal_masks(t, d)
    b_col = jnp.sum(jnp.where(causal, f_row, 0.0), axis=1, keepdims=True)
    b_row = jnp.sum(jnp.where(causal_t, f_col, 0.0), axis=0, keepdims=True)
    dlog = jnp.where(causal, b_col - b_row + i_row, -jnp.inf)
    m_prev = m_s[...]
    inter = b_col + m_prev
    mt = jnp.maximum(inter, jnp.max(dlog, axis=1, keepdims=True))
    w_intra = jnp.exp(dlog - mt)
    w_inter = jnp.exp(inter - mt)
    cm = c_s[...]
    nv = n_s[...]
    sm = lax.dot_general(q, k, _NT, preferred_element_type=F32) * w_intra
    num = _dot(sm.astype(BF16), v) + w_inter * _dot(q, cm.astype(BF16))
    den = jnp.sum(sm, axis=1, keepdims=True) + w_inter * jnp.sum(q.astype(F32) * nv, axis=1, keepdims=True)
    h_ref[...] = num / jnp.maximum(jnp.abs(den), jnp.exp(-mt))

    rowid = lax.broadcasted_iota(jnp.int32, (t, 1), 0)
    end_row = jnp.where(d == 0, t - 1, 0)
    m_new = jnp.sum(jnp.where(rowid == end_row, mt, 0.0), axis=0, keepdims=True)
    b_last = jnp.sum(f_row, axis=1, keepdims=True)
    kw = k.astype(F32) * jnp.exp(b_last - b_col + i_col - m_new)
    decay = jnp.exp(b_last + m_prev - m_new)
    c_new = decay * cm + lax.dot_general(kw.astype(BF16), v, _TN, preferred_element_type=F32)
    n_new = decay * nv + jnp.sum(kw, axis=0, keepdims=True)
    c_s[...] = c_new
    n_s[...] = n_new
    m_s[...] = m_new
    if emit_state:
        @pl.when(c == nc - 1)
        def _():
            co_ref[...] = c_new
            no_ref[...] = n_new
            mo_ref[...] = m_new


def mlstm_scan(qk, v, gcol, grow, init, *, row0, nseq, length, heads, emit_state):
    n = nseq * length
    dk = qk.shape[1] // (2 * heads)
    dv = v.shape[1] // heads
    t = _tile(length, SCAN_CHUNK, 8)
    nc = length // t
    rc = lambda s, d, h, c: _chunk_row(s, d, c, nc)
    r0 = row0 // t
    in_specs = [
        pl.BlockSpec((t, dk), lambda s, d, h, c: (r0 + rc(s, d, h, c), h)),
        pl.BlockSpec((t, dk), lambda s, d, h, c: (r0 + rc(s, d, h, c), heads + h)),
        pl.BlockSpec((t, dv), lambda s, d, h, c: (r0 + rc(s, d, h, c), h)),
        pl.BlockSpec((None, None, t, 2), lambda s, d, h, c: (d, h, rc(s, d, h, c), 0)),
        pl.BlockSpec((None, None, None, 2, t), lambda s, d, h, c: (d, h, rc(s, d, h, c), 0, 0)),
    ]
    args = [qk, qk, v, gcol, grow]
    st_specs = [
        pl.BlockSpec((None, None, None, dk, dv), lambda s, d, h, c: (s, d, h, 0, 0)),
        pl.BlockSpec((None, None, None, 1, dk), lambda s, d, h, c: (s, d, h, 0, 0)),
        pl.BlockSpec((None, None, None, 1, 1), lambda s, d, h, c: (s, d, h, 0, 0)),
    ]
    if init is not None:
        in_specs += st_specs
        args += list(init)
    out_specs = [pl.BlockSpec((None, t, dv), lambda s, d, h, c: (d, rc(s, d, h, c), h))]
    out_shape = [jax.ShapeDtypeStruct((2, n, heads * dv), F32)]
    if emit_state:
        out_specs += st_specs
        out_shape += [jax.ShapeDtypeStruct((nseq, 2, heads, dk, dv), F32),
                      jax.ShapeDtypeStruct((nseq, 2, heads, 1, dk), F32),
                      jax.ShapeDtypeStruct((nseq, 2, heads, 1, 1), F32)]
    return pl.pallas_call(
        functools.partial(_mlstm_kernel, t=t, nc=nc, has_init=init is not None, emit_state=emit_state),
        grid=(nseq, 2, heads, nc),
        in_specs=in_specs,
        out_specs=out_specs,
        out_shape=out_shape,
        scratch_shapes=[pltpu.VMEM((dk, dv), F32), pltpu.VMEM((1, dk), F32), pltpu.VMEM((1, 1), F32)],
        compiler_params=_params(("parallel", "parallel", "parallel", "arbitrary")),
        name="mlstm_scan",
    )(*args)


def _mlstm_post_kernel(h_ref, g_ref, w_ref, o_ref):
    hs = h_ref[0] + h_ref[1]
    y = hs * lax.rsqrt(jnp.mean(hs * hs, axis=-1, keepdims=True) + NORM_EPS) * w_ref[...]
    o_ref[...] = (y * g_ref[...]).astype(o_ref.dtype)


def mlstm_post(h, o_gate, head_norm, *, row0, heads):
    _, n, width = h.shape
    dv = width // heads
    tm = _tile(n, 512, 8)
    return pl.pallas_call(
        _mlstm_post_kernel,
        grid=(n // tm, heads),
        in_specs=[pl.BlockSpec((2, tm, dv), lambda i, hh: (0, i, hh)),
                  pl.BlockSpec((tm, dv), lambda i, hh: (row0 // tm + i, hh)),
                  pl.BlockSpec((1, dv), lambda i, hh: (0, 0))],
        out_specs=pl.BlockSpec((tm, dv), lambda i, hh: (i, hh)),
        out_shape=jax.ShapeDtypeStruct((n, width), BF16),
        compiler_params=_params(("parallel", "parallel")),
        name="mlstm_post",
    )(h, o_gate, head_norm.reshape(1, dv))


def _ssd_kernel(*refs, t, nc, hpg, p, has_init, emit_state):
    refs = list(refs)
    x_ref, b_ref, c_ref, acol_ref, arow_ref, dtcol_ref = refs[:6]
    pos = 6
    if has_init:
        s0_ref = refs[pos]
        pos += 1
    y_ref = refs[pos]
    pos += 1
    if emit_state:
        so_ref = refs[pos]
        pos += 1
    st = refs[pos]
    d = pl.program_id(1)
    c = pl.program_id(3)
    width = hpg * p

    @pl.when(c == 0)
    def _():
        if has_init:
            st[...] = s0_ref[...].T
        else:
            st[...] = jnp.zeros_like(st)

    causal, causal_t = _causal_masks(t, d)
    lane_head = lax.broadcasted_iota(jnp.int32, (1, width), 1) // p
    lane_pair = lax.broadcasted_iota(jnp.int32, (1, 2 * p), 1)
    bm = b_ref[...]
    cmat = c_ref[...]
    dt_e = jnp.zeros((t, width), F32)
    for e in range(hpg):
        dt_e = jnp.where(lane_head == e, dtcol_ref[:, e:e + 1], dt_e)
    xd = x_ref[...] * dt_e
    xdb = xd.astype(BF16)
    cb = lax.dot_general(cmat, bm, _NT, preferred_element_type=F32)
    cum_e = jnp.zeros((t, width), F32)
    tot_e = jnp.zeros((1, width), F32)
    y_pairs = []
    for e in range(hpg):
        a_row = arow_ref[e:e + 1, :]
        a_col = acol_ref[:, e:e + 1]
        cum_col = jnp.sum(jnp.where(causal, a_row, 0.0), axis=1, keepdims=True)
        cum_row = jnp.sum(jnp.where(causal_t, a_col, 0.0), axis=0, keepdims=True)
        wmat = (cb * jnp.exp(jnp.where(causal, cum_col - cum_row, -jnp.inf))).astype(BF16)
        j = e // 2
        yp = _dot(wmat, xdb[:, j * 2 * p:(j + 1) * 2 * p])
        if e % 2 == 0:
            y_even = yp
        else:
            y_pairs.append(jnp.where(lane_pair < p, y_even, yp))
        cum_e = jnp.where(lane_head == e, cum_col, cum_e)
        tot_e = jnp.where(lane_head == e, jnp.sum(a_row, axis=1, keepdims=True), tot_e)
    s_prev = st[...]
    y_off = _dot(cmat, s_prev.astype(BF16)) * jnp.exp(cum_e)
    y_ref[...] = jnp.concatenate(y_pairs, axis=1) + y_off
    xw = (xd * jnp.exp(tot_e - cum_e)).astype(BF16)
    s_new = jnp.exp(tot_e) * s_prev + lax.dot_general(bm, xw, _TN, preferred_element_type=F32)
    st[...] = s_new
    if emit_state:
        @pl.when(c == nc - 1)
        def _():
            so_ref[...] = s_new.T


def ssd_scan(x, bc, acol, arow, dtcol, init, *, row0, nseq, length, groups, hpg, emit_state):
    n = nseq * length
    inner = x.shape[1]
    width = inner // groups
    p = width // hpg
    ns = bc.shape[1] // (2 * groups)
    t = _tile(length, SCAN_CHUNK, 8)
    nc = length // t
    rc = lambda s, d, g, c: _chunk_row(s, d, c, nc)
    r0 = row0 // t
    in_specs = [
        pl.BlockSpec((t, width), lambda s, d, g, c: (r0 + rc(s, d, g, c), g)),
        pl.BlockSpec((t, ns), lambda s, d, g, c: (r0 + rc(s, d, g, c), g)),
        pl.BlockSpec((t, ns), lambda s, d, g, c: (r0 + rc(s, d, g, c), groups + g)),
        pl.BlockSpec((None, None, t, hpg), lambda s, d, g, c: (d, g, rc(s, d, g, c), 0)),
        pl.BlockSpec((None, None, None, hpg, t), lambda s, d, g, c: (d, g, rc(s, d, g, c), 0, 0)),
        pl.BlockSpec((None, None, t, hpg), lambda s, d, g, c: (d, g, rc(s, d, g, c), 0)),
    ]
    args = [x, bc, bc, acol, arow, dtcol]
    st_spec = pl.BlockSpec((None, None, None, width, ns), lambda s, d, g, c: (s, d, g, 0, 0))
    if init is not None:
        in_specs.append(st_spec)
        args.append(init)
    out_specs = [pl.BlockSpec((None, t, width), lambda s, d, g, c: (d, rc(s, d, g, c), g))]
    out_shape = [jax.ShapeDtypeStruct((2, n, inner), F32)]
    if emit_state:
        out_specs.append(st_spec)
        out_shape.append(jax.ShapeDtypeStruct((nseq, 2, groups, width, ns), F32))
    return pl.pallas_call(
        functools.partial(_ssd_kernel, t=t, nc=nc, hpg=hpg, p=p, has_init=init is not None, emit_state=emit_state),
        grid=(nseq, 2, groups, nc),
        in_specs=in_specs,
        out_specs=out_specs,
        out_shape=out_shape,
        scratch_shapes=[pltpu.VMEM((ns, width), F32)],
        compiler_params=_params(("parallel", "parallel", "parallel", "arbitrary")),
        name="ssd_scan",
    )(*args)


def _ssd_post_kernel(y_ref, x_ref, z_ref, dsk_ref, w_ref, o_ref):
    y = (y_ref[0] + y_ref[1] + x_ref[...] * dsk_ref[...]) * _silu(z_ref[...])
    o_ref[...] = (y * lax.rsqrt(jnp.mean(y * y, axis=-1, keepdims=True) + NORM_EPS) * w_ref[...]).astype(o_ref.dtype)


def ssd_post(y, x, z, d_skip_cols, norm_w, *, row0):
    _, n, inner = y.shape
    tm = _tile(n, 128, 8)
    row = pl.BlockSpec((tm, inner), lambda i: (row0 // tm + i, 0))
    vec = pl.BlockSpec((1, inner), lambda i: (0, 0))
    return pl.pallas_call(
        _ssd_post_kernel,
        grid=(n // tm,),
        in_specs=[pl.BlockSpec((2, tm, inner), lambda i: (0, i, 0)), row, row, vec, vec],
        out_specs=pl.BlockSpec((tm, inner), lambda i: (i, 0)),
        out_shape=jax.ShapeDtypeStruct((n, inner), BF16),
        compiler_params=_params(("parallel",)),
        name="ssd_post",
    )(y, x, z, d_skip_cols.reshape(1, inner), norm_w.reshape(1, inner))


def _hyena_kernel(u_ref, f_ref, g_ref, tp_ref, tq_ref, bias_ref, xm_ref, o_ref, *, length):
    u = u_ref[...]
    a = _dot(f_ref[...], u.astype(BF16))
    a_sw = jnp.concatenate([a[length:], a[:length]], axis=0)
    y = a * tp_ref[...] + a_sw * tq_ref[...]
    conv = _dot(g_ref[...], y.astype(BF16))
    o_ref[...] = ((conv + u * bias_ref[...]) * xm_ref[...]).astype(o_ref.dtype)


def hyena_conv(u, u_at, fmat, gmat, tp, tq, bias, xm, xm_at, *, nseq, length, d, out_dtype):
    tc = _tile(d, 256, LANES)
    n2 = 2 * length
    (u_r0, u_c0), (x_r0, x_c0) = u_at, xm_at
    return pl.pallas_call(
        functools.partial(_hyena_kernel, length=length),
        grid=(d // tc, nseq),
        in_specs=[
            pl.BlockSpec((length, tc), lambda j, s: (u_r0 // length + s, u_c0 // tc + j)),
            pl.BlockSpec((n2, length), lambda j, s: (0, 0)),
            pl.BlockSpec((length, n2), lambda j, s: (0, 0)),
            pl.BlockSpec((n2, tc), lambda j, s: (0, j)),
            pl.BlockSpec((n2, tc), lambda j, s: (0, j)),
            pl.BlockSpec((1, tc), lambda j, s: (0, j)),
            pl.BlockSpec((length, tc), lambda j, s: (x_r0 // length + s, x_c0 // tc + j)),
        ],
        out_specs=pl.BlockSpec((length, tc), lambda j, s: (s, j)),
        out_shape=jax.ShapeDtypeStruct((nseq * length, d), out_dtype),
        compiler_params=_params(("parallel", "parallel")),
        name="hyena_conv",
    )(u, fmat, gmat, tp, tq, bias.reshape(1, d), xm)


def _dft_matrices(length):
    n2 = 2 * length
    r = jnp.arange(n2, dtype=jnp.int32)
    kfreq = jnp.where(r <= length, r, r - length)
    ang = ((kfreq[:, None] * r[None, :]) % n2).astype(F32) * (2.0 * math.pi / n2)
    full = jnp.where((r > length)[:, None], -jnp.sin(ang), jnp.cos(ang))
    wk = jnp.where((r == 0) | (r == length), 1.0, 2.0) / n2
    inv = (full[:, :length] * wk[:, None]).T
    return full, full[:, :length].astype(BF16), inv.astype(BF16)


def _hyena_filter_spectra(length, f_w1, f_b1, f_w2, f_b2, f_w3, f_freq, d):
    hp = lax.Precision.HIGHEST
    t = jnp.linspace(0.0, 1.0, length, dtype=F32)[:, None]
    bands = (HYENA_EMB - 1) // 2
    f = jnp.linspace(1e-4, bands - 1, bands, dtype=F32)
    w = 2.0 * math.pi * jnp.arange(length, dtype=F32)[:, None] / length
    z = jnp.concatenate([t, jnp.cos(f * w), -jnp.sin(f * w)], axis=-1)
    hid = jnp.sin(f_freq * (jnp.dot(z, f_w1, precision=hp) + f_b1))
    hid = jnp.sin(f_freq * (jnp.dot(hid, f_w2, precision=hp) + f_b2))
    filt = matmul(hid, f_w3, x3=True, tk=hid.shape[1], name="hyena_filter").reshape(length, 2, 2, d)
    min_decay = math.log(HYENA_TARGET) / HYENA_DECAY_LONG
    max_decay = math.log(HYENA_TARGET) / HYENA_DECAY_SHORT
    deltas = jnp.linspace(min_decay, max_decay, d, dtype=F32)
    filt = filt * jnp.exp(-t * jnp.abs(deltas))[:, None, None, :]
    full, fmat, gmat = _dft_matrices(length)
    tps, tqs = [], []
    for o in range(2):
        h_fwd, h_bwd = filt[:, o, 0], filt[:, o, 1]
        taps = jnp.concatenate([h_fwd, jnp.zeros((1, d), F32), h_bwd[:0:-1]], axis=0)
        taps = taps / jnp.sum(jnp.abs(taps), axis=0, keepdims=True)
        spec = matmul(full, taps, x3=True, name="hyena_filter_dft")
        top, bot = spec[:length], spec[length:]
        zero = jnp.zeros((1, d), F32)
        im = jnp.concatenate([zero, bot[1:]], axis=0)
        tps.append(jnp.concatenate([top, bot[0:1], top[1:]], axis=0))
        tqs.append(jnp.concatenate([-im, im], axis=0))
    return fmat, gmat, tps, tqs


def _router_kernel(l_ref, b_ref, gates_ref, dest_ref, cnt_ref, pre_s, *, n_grp, epg, tb, chunk):
    n_exp = n_grp * epg
    rows = l_ref.shape[1]
    lg = l_ref[...] + b_ref[...]
    g = lg[n_exp:n_exp + n_grp]
    ge = jnp.exp(g - jnp.max(g, axis=0, keepdims=True))
    gp = ge / jnp.sum(ge, axis=0, keepdims=True)
    gpm = jnp.max(gp, axis=0, keepdims=True)
    gi = lax.broadcasted_iota(jnp.int32, gp.shape, 0)
    gsel = jnp.min(jnp.where(gp == gpm, gi, n_grp), axis=0, keepdims=True)
    e_in = lg[0:epg]
    for q in range(1, n_grp):
        e_in = jnp.where(gsel == q, lg[q * epg:(q + 1) * epg], e_in)
    ee = jnp.exp(e_in - jnp.max(e_in, axis=0, keepdims=True))
    ep = ee / jnp.sum(ee, axis=0, keepdims=True)
    ei = lax.broadcasted_iota(jnp.int32, ep.shape, 0)
    p1 = jnp.max(ep, axis=0, keepdims=True)
    s1 = jnp.min(jnp.where(ep == p1, ei, epg), axis=0, keepdims=True)
    ep2 = jnp.where(ei == s1, -1.0, ep)
    p2 = jnp.max(ep2, axis=0, keepdims=True)
    s2 = jnp.min(jnp.where(ep2 == p2, ei, epg), axis=0, keepdims=True)
    den = p1 + p2
    gates_ref[0:1, :] = gpm * p1 / den
    gates_ref[1:2, :] = gpm * p2 / den

    eio = lax.broadcasted_iota(jnp.int32, (n_exp, rows), 0)
    oh0 = eio == gsel * epg + s1
    oh1 = eio == gsel * epg + s2
    cnt = oh0.astype(F32) + oh1.astype(F32)
    ca = lax.broadcasted_iota(jnp.int32, (chunk, chunk), 0)
    cb = lax.broadcasted_iota(jnp.int32, (chunk, chunk), 1)
    tri = (ca < cb).astype(BF16)
    carry = jnp.zeros((n_exp, 1), F32)
    for c0 in range(0, rows, chunk):
        c = cnt[:, c0:c0 + chunk]
        pre_s[:, c0:c0 + chunk] = _dot(c.astype(BF16), tri) + carry
        carry = carry + jnp.sum(c, axis=1, keepdims=True)
    nblk = jnp.floor((carry + (tb - 1)) * (1.0 / tb))
    sa = lax.broadcasted_iota(jnp.int32, (n_exp, n_exp), 0)
    sb = lax.broadcasted_iota(jnp.int32, (n_exp, n_exp), 1)
    nblk_row = jnp.sum(jnp.where(sa == sb, nblk, 0.0), axis=0, keepdims=True)
    blk_start = jnp.sum(jnp.where(sb < sa, nblk_row, 0.0), axis=1, keepdims=True)
    slot = blk_start * tb + pre_s[...]
    dest_ref[0:1, :] = jnp.sum(jnp.where(oh0, slot, 0.0), axis=0, keepdims=True).astype(jnp.int32)
    dest_ref[1:2, :] = jnp.sum(jnp.where(oh1, slot, 0.0), axis=0, keepdims=True).astype(jnp.int32)
    cnt_ref[...] = carry.astype(jnp.int32)


def moe_route(logits_t, bias, *, n_grp, epg, tb):
    nr, rows = logits_t.shape
    n_exp = n_grp * epg
    assert tb & (tb - 1) == 0
    return pl.pallas_call(
        functools.partial(_router_kernel, n_grp=n_grp, epg=epg, tb=tb, chunk=_tile(rows, 512, LANES)),
        out_shape=[jax.ShapeDtypeStruct((MOE_TOP_K, rows), F32),
                   jax.ShapeDtypeStruct((MOE_TOP_K, rows), jnp.int32),
                   jax.ShapeDtypeStruct((n_exp, 1), jnp.int32)],
        scratch_shapes=[pltpu.VMEM((n_exp, rows), F32)],
        compiler_params=pltpu.CompilerParams(vmem_limit_bytes=VMEM_LIMIT),
        name="moe_route",
    )(logits_t, bias)


def _dispatch_kernel(dest_ref, cnt_ref, start_ref, nu_ref, h_ref, xs_ref, zero_s, sem, zsem, *,
                     rows, tm, tb, n_exp, n_blocks, slab):
    i = pl.program_id(0)

    def zero_copy(row):
        return pltpu.make_async_copy(zero_s, _slab_at(xs_ref, row, tb, slab), zsem)

    @pl.when(i == 0)
    def _():
        zero_s[...] = jnp.zeros_like(zero_s)
        for wait in (False, True):
            def go(row):
                if wait:
                    zero_copy(row).wait()
                else:
                    zero_copy(row).start()

            def last_block(e, carry):
                @pl.when(cnt_ref[e] > 0)
                def _():
                    go(start_ref[e] + ((cnt_ref[e] - 1) & ~(tb - 1)))
                return carry

            def tail_block(b, carry):
                go(b * tb)
                return carry

            lax.fori_loop(0, n_exp, last_block, 0)
            lax.fori_loop(nu_ref[0], n_blocks, tail_block, 0)

    def row_copy(r, slot):
        return pltpu.make_async_copy(_slab_at(h_ref, r, 1, slab), _slab_at(xs_ref, slot, 1, slab), sem)

    def issue(r, carry):
        row_copy(r, dest_ref[i * tm + r]).start()
        row_copy(r, dest_ref[rows + i * tm + r]).start()
        return carry

    def drain(r, carry):
        row_copy(0, 0).wait()
        row_copy(0, 0).wait()
        return carry

    lax.fori_loop(0, tm, issue, 0)
    lax.fori_loop(0, tm, drain, 0)


def _combine_kernel(dest_ref, x_ref, g2_ref, gt_ref, ys_ref, o_ref, buf, sem, *, rows, tm, slab):
    base = pl.program_id(0) * tm

    def row_copy(k, r, slot):
        return pltpu.make_async_copy(_slab_at(ys_ref, slot, 1, slab), _slab_at(buf.at[k], r, 1, slab), sem)

    def issue(r, carry):
        row_copy(0, r, dest_ref[base + r]).start()
        row_copy(1, r, dest_ref[rows + base + r]).start()
        return carry

    def drain(r, carry):
        row_copy(0, 0, 0).wait()
        row_copy(0, 0, 0).wait()
        return carry

    lax.fori_loop(0, tm, issue, 0)
    lax.fori_loop(0, tm, drain, 0)
    g0 = gt_ref[:, 0:1]
    g1 = gt_ref[:, 1:2]
    for s in range(slab):
        cols = slice(s * LANES, (s + 1) * LANES)
        y = g0 * buf[0, pl.ds(s, tm, stride=slab), :] + g1 * buf[1, pl.ds(s, tm, stride=slab), :]
        o_ref[:, cols] = x_ref[:, cols] + g2_ref[:, cols] * y


def _moe_up_kernel(be_ref, nu_ref, x_ref, wg_ref, wu_ref, o_ref, *, fchunk, slab):
    used = pl.program_id(0) < nu_ref[0]

    @pl.when(used)
    def _():
        x = _load_slab(x_ref, o_ref.shape[0], slab).astype(BF16)
        for f0 in range(0, o_ref.shape[1], fchunk):
            g = _dot(x, wg_ref[:, f0:f0 + fchunk].astype(BF16))
            u = _dot(x, wu_ref[:, f0:f0 + fchunk].astype(BF16))
            o_ref[:, f0:f0 + fchunk] = (_silu(g) * u).astype(o_ref.dtype)

    @pl.when(jnp.logical_not(used))
    def _():
        o_ref[...] = jnp.zeros_like(o_ref)


def _moe_down_kernel(be_ref, nu_ref, h_ref, wd_ref, o_ref, *, slab):
    used = pl.program_id(0) < nu_ref[0]

    @pl.when(used)
    def _():
        _store_slab(o_ref, _dot(h_ref[...], wd_ref[...].astype(BF16)), slab)

    @pl.when(jnp.logical_not(used))
    def _():
        o_ref[...] = jnp.zeros_like(o_ref)


def moe_experts(x_slots, block_e, n_used, w_gate, w_up, w_down, layer):
    d, ff = w_gate.shape[-2:]
    slab = d // LANES
    n_slots = x_slots.shape[0] // slab
    tb = MOE_ROW_BLOCK
    n_blocks = n_slots // tb
    blk = lambda b, be, nu: (jnp.minimum(b, nu[0] - 1), 0)
    out_blk = lambda b, be, nu: (b, 0)
    wsel = lambda b, be, nu: (layer, be[b], 0, 0)
    hmid = pl.pallas_call(
        functools.partial(_moe_up_kernel, fchunk=_tile(ff, 256, LANES), slab=slab),
        grid_spec=pltpu.PrefetchScalarGridSpec(
            num_scalar_prefetch=2, grid=(n_blocks,),
            in_specs=[pl.BlockSpec((tb * slab, LANES), blk),
                      pl.BlockSpec((None, None, d, ff), wsel),
                      pl.BlockSpec((None, None, d, ff), wsel)],
            out_specs=pl.BlockSpec((tb, ff), out_blk)),
        out_shape=jax.ShapeDtypeStruct((n_slots, ff), BF16),
        compiler_params=_params(("arbitrary",)),
        name="moe_gate_up",
    )(block_e, n_used, x_slots, w_gate, w_up)
    return pl.pallas_call(
        functools.partial(_moe_down_kernel, slab=slab),
        grid_spec=pltpu.PrefetchScalarGridSpec(
            num_scalar_prefetch=2, grid=(n_blocks,),
            in_specs=[pl.BlockSpec((tb, ff), blk),
                      pl.BlockSpec((None, None, ff, d), wsel)],
            out_specs=pl.BlockSpec((tb * slab, LANES), out_blk)),
        out_shape=jax.ShapeDtypeStruct((n_slots * slab, LANES), F32),
        compiler_params=_params(("arbitrary",)),
        name="moe_down",
    )(block_e, n_used, hmid, w_down)


def moe_router_weights(w_rg, b_rg, w_re, b_re, layer):
    w = jnp.concatenate([w_re[layer], w_rg[layer]], axis=-1).T
    b = jnp.concatenate([b_re[layer], b_rg[layer]])[:, None]
    pad = -w.shape[0] % 8
    return jnp.pad(w, ((0, pad), (0, 0))), jnp.pad(b, ((0, pad), (0, 0)))


def hier_moe_residual(x, h, logits_t, bias, g2, row_map, w_gate, w_up, w_down, layer, *, n_grp, tm):
    rows, d = x.shape
    slab = d // LANES
    nsteps = rows // tm
    n_exp = w_gate.shape[1]
    tb = MOE_ROW_BLOCK
    n_pair = rows * MOE_TOP_K
    n_blocks = -(-(n_pair + n_exp * (tb - 1)) // tb)
    gates_t, dest_t, counts = moe_route(logits_t, bias, n_grp=n_grp, epg=n_exp // n_grp, tb=tb)
    counts = counts[:, 0]
    blocks = (counts + tb - 1) // tb
    blk_end = jnp.cumsum(blocks)
    starts = ((blk_end - blocks) * tb).astype(jnp.int32)
    n_used = blk_end[-1].astype(jnp.int32)
    block_e = jnp.minimum(jnp.searchsorted(blk_end, jnp.arange(n_blocks), side="right"), n_exp - 1)
    block_e = jnp.where(jnp.arange(n_blocks) < n_used, block_e, block_e[n_used - 1]).astype(jnp.int32)
    dest = dest_t.reshape(-1)

    x_slots = pl.pallas_call(
        functools.partial(_dispatch_kernel, rows=rows, tm=tm, tb=tb, n_exp=n_exp, n_blocks=n_blocks, slab=slab),
        grid_spec=pltpu.PrefetchScalarGridSpec(
            num_scalar_prefetch=4, grid=(nsteps,),
            in_specs=[pl.BlockSpec((tm * slab, LANES), lambda i, *_: (i, 0))],
            out_specs=pl.BlockSpec(memory_space=pl.ANY),
            scratch_shapes=[pltpu.VMEM((tb * slab, LANES), F32), pltpu.SemaphoreType.DMA,
                            pltpu.SemaphoreType.DMA]),
        out_shape=jax.ShapeDtypeStruct((n_blocks * tb * slab, LANES), F32),
        compiler_params=_params(("arbitrary",)),
        name="moe_dispatch",
    )(dest, counts, starts, n_used.reshape(1), h)
    y_slots = moe_experts(x_slots, block_e, n_used.reshape(1), w_gate, w_up, w_down, layer)
    return pl.pallas_call(
        functools.partial(_combine_kernel, rows=rows, tm=tm, slab=slab),
        grid_spec=pltpu.PrefetchScalarGridSpec(
            num_scalar_prefetch=1, grid=(nsteps,),
            in_specs=[pl.BlockSpec((tm, d), lambda i, *_: (i, 0)),
                      pl.BlockSpec((None, 1, d), lambda i, *_: (row_map(i * tm), 0, 0)),
                      pl.BlockSpec((tm, MOE_TOP_K), lambda i, *_: (i, 0)),
                      pl.BlockSpec(memory_space=pl.ANY)],
            out_specs=pl.BlockSpec((tm, d), lambda i, *_: (i, 0)),
            scratch_shapes=[pltpu.VMEM((MOE_TOP_K, tm * slab, LANES), F32), pltpu.SemaphoreType.DMA]),
        out_shape=jax.ShapeDtypeStruct((rows, d), F32),
        compiler_params=_params(("arbitrary",)),
        name="moe_combine",
    )(dest, x, g2, gates_t.T, y_slots)


def _rope_lane_tables(length, hd, n_rows):
    grid_rows = length // ROPE_GRID_W
    row = jnp.broadcast_to(jnp.arange(grid_rows, dtype=F32)[:, None], (grid_rows, ROPE_GRID_W)).reshape(length)
    col = jnp.broadcast_to(jnp.arange(ROPE_GRID_W, dtype=F32)[None, :], (grid_rows, ROPE_GRID_W)).reshape(length)
    axis_dim = hd // 2
    inv_freq = ROPE_BASE ** (-jnp.arange(0, axis_dim, 2, dtype=F32) / axis_dim)
    ang_r = row[:, None] * inv_freq
    ang_c = col[:, None] * inv_freq
    cos = jnp.concatenate([jnp.cos(ang_r)] * 2 + [jnp.cos(ang_c)] * 2, axis=1)
    sin = jnp.concatenate([-jnp.sin(ang_r), jnp.sin(ang_r), -jnp.sin(ang_c), jnp.sin(ang_c)], axis=1)
    reps = (n_rows // length, 2 * LANES // (2 * hd))
    return jnp.tile(cos, reps), jnp.tile(sin, reps)


def kernel(x_prompt, x_sample, cache_attn_k, cache_attn_v, state_mlstm_C, state_mlstm_n, state_mlstm_m, state_ssd, c, c_ctx, norm_mix, norm_ffn, w_mod, b_mod, attn_w_in, attn_q_norm, attn_k_norm, attn_lambda, attn_sub_norm, attn_w_out, mlstm_w_in, mlstm_conv_w, mlstm_conv_b, mlstm_w_gate, mlstm_b_gate, mlstm_head_norm, mlstm_w_out, ssd_w_in, ssd_conv_w, ssd_conv_b, ssd_dt_bias, ssd_a_log, ssd_d_skip, ssd_norm, ssd_w_out, hyena_w_in, hyena_conv_w, hyena_conv_b, hyena_f_w1, hyena_f_b1, hyena_f_w2, hyena_f_b2, hyena_f_w3, hyena_f_freq, hyena_skip_bias, hyena_w_out, moe_w_group, moe_b_group, moe_w_expert, moe_b_expert, moe_w_gate, moe_w_up, moe_w_down):
    batch, seq, dm = x_prompt.shape
    dec_batch, dec_seq, _ = x_sample.shape
    depth = norm_mix.shape[0]
    n_ctx = batch * seq
    n_lat = dec_batch * dec_seq
    rows = n_ctx + n_lat
    past = cache_attn_k.shape[2]
    a_heads, a_hd = cache_attn_k.shape[3], cache_attn_k.shape[5]
    m_heads, m_dk, m_dv = state_mlstm_C.shape[3:]
    s_heads, s_p, s_n = state_ssd.shape[3:]
    s_inner = s_heads * s_p
    s_groups = (ssd_conv_w.shape[-1] - s_inner) // (2 * s_n)
    s_hpg = s_heads // s_groups

    def row_map(r):
        return jnp.where(r < n_ctx, 0, 1 + (r - n_ctx) // dec_seq)

    row_unit = math.gcd(n_ctx, dec_seq)
    tm_norm = _tile(row_unit, 256, 8)
    tm_res = _tile(row_unit, 1024, 8)
    tm_seq = math.lcm(seq, dec_seq)
    assert n_ctx % tm_seq == 0 and n_lat % tm_seq == 0
    assert seq & (seq - 1) == 0 and dec_seq & (dec_seq - 1) == 0
    seq_of_tile = lambda i: jnp.where(i * tm_seq < n_ctx, seq, dec_seq)
    parts = ((0, batch, seq), (n_ctx, dec_batch, dec_seq))
    ones = lambda n: jnp.ones((n,), F32)

    tn_of = lambda n: _tile(n, 1024, LANES)

    def conv_extras(taps, bias, scale, col0):
        tn = tn_of(scale.shape[0])
        assert col0 % tn == 0
        return [(taps, (3, tn), lambda i, j, c=col0 // tn: (0, j + c)), _row_vec(bias, tn, col0), _row_vec(scale, tn)]

    n_cond = 1 + dec_batch
    cond = jnp.concatenate([c_ctx[None, :], c], axis=0)
    cond = jnp.pad(jax.nn.silu(cond), ((0, -n_cond % 8), (0, 0))).astype(BF16)

    x = jnp.concatenate([x_prompt.reshape(n_ctx, dm), x_sample.reshape(n_lat, dm)], axis=0)
    outs = {}
    for i in range(depth):
        kind, j = i % 4, i // 4
        mod = matmul(cond, w_mod, w_idx=i, tm=8, tn=1024, tk=2048, extras=[_row_vec(b_mod[i], 1024)],
                     epi=_epi_bias, name="adaln_mod")
        sh1, sc1, g1, sh2, sc2, g2 = [mod[:, None, q * dm:(q + 1) * dm] for q in range(6)]
        h1 = norm_mod(x, norm_mix[i], sc1, sh1, row_map, tm_norm)

        if kind == 0:
            lam_init = 0.8 - 0.6 * math.exp(-0.3 * i)
            tn = tn_of(dm)
            cos, sin = _rope_lane_tables(dec_seq, a_hd, tm_seq)
            tabs = [(cos, (tm_seq, LANES), lambda i_, j_: (0, 0)), (sin, (tm_seq, LANES), lambda i_, j_: (0, 0))]
            rep = dm // a_hd
            q = matmul(h1, attn_w_in, w_idx=j, n=dm, tm=tm_seq, tn=tn, out_dtypes=(BF16,),
                       extras=[_row_vec(jnp.tile(attn_q_norm[j], rep), tn)] + tabs,
                       epi=_make_epi_qk(tm_seq, tn, a_hd, n_ctx, 1), name="attn_in_q")
            k, k_plain = matmul(h1, attn_w_in, w_idx=j, n0=dm, n=dm, tm=tm_seq, tn=tn, out_dtypes=(BF16, F32),
                                extras=[_row_vec(jnp.tile(attn_k_norm[j], rep), tn)] + tabs,
                                epi=_make_epi_qk(tm_seq, tn, a_hd, n_ctx, 2), name="attn_in_k")
            v, v_f32 = matmul(h1, attn_w_in, w_idx=j, n0=2 * dm, n=dm, tm=tm_seq, tn=tn, out_dtypes=(BF16, F32),
                              epi=_epi_two, name="attn_in_v")
            outs["k"] = k_plain[:n_ctx].reshape(batch, 1, seq, a_heads, 2, a_hd)
            outs["v"] = v_f32[:n_ctx].reshape(batch, 1, seq, a_heads, 2 * a_hd)
            lv = attn_lambda[j]
            lam = jnp.exp(jnp.sum(lv[0] * lv[1])) - jnp.exp(jnp.sum(lv[2] * lv[3])) + lam_init
            cache = (cache_attn_k[:, j].reshape(dec_batch * past, dm), cache_attn_v[:, j].reshape(dec_batch * past, dm))
            o_parts = [diff_attention(q, k, v, cache if row0 else None, lam, attn_sub_norm[j], row0=row0, nseq=nseq,
                                      length=length, heads=a_heads, post_scale=1.0 - lam_init)
                       for row0, nseq, length in parts]
            mix_in, w_out = jnp.concatenate(o_parts, axis=0), attn_w_out

        elif kind == 1:
            qk_w = 2 * m_heads * m_dk
            v_w = m_heads * m_dv
            k_scale = jnp.concatenate([ones(qk_w // 2), jnp.full((qk_w // 2,), m_dk ** -0.5, F32)])
            qk = matmul(h1, mlstm_w_in, w_idx=j, n=qk_w, tm=tm_seq, out_dtypes=(BF16,),
                        extras=conv_extras(mlstm_conv_w[j], mlstm_conv_b[j], k_scale, 0),
                        epi=_make_epi_conv(tm_seq, seq_of_tile, True), name="mlstm_in_qk")
            v = matmul(h1, mlstm_w_in, w_idx=j, n0=qk_w, n=v_w, out_dtypes=(BF16,), name="mlstm_in_v")
            o_gate = matmul(h1, mlstm_w_in, w_idx=j, n0=qk_w + v_w, n=v_w, epi=_epi_sigmoid, name="mlstm_in_o")
            g = matmul(h1, mlstm_w_gate, w_idx=j, x3=True, tm=512, extras=[_row_vec(mlstm_b_gate[j], 4 * m_heads)],
                       epi=_epi_bias, name="mlstm_gate")
            g = g.reshape(rows, 2, 2, m_heads)
            gates = jnp.stack([g[:, :, 0], jax.nn.log_sigmoid(g[:, :, 1])], axis=-1)
            gcol = gates.transpose(1, 2, 0, 3)
            init = (state_mlstm_C[:, j], state_mlstm_n[:, j][:, :, :, None, :], state_mlstm_m[:, j][:, :, :, None, None])
            y_parts = []
            for row0, nseq, length in parts:
                t = _tile(length, SCAN_CHUNK, 8)
                n = nseq * length
                gc = gcol[:, :, row0:row0 + n]
                grow = gc.reshape(2, m_heads, n // t, t, 2).transpose(0, 1, 2, 4, 3)
                res = mlstm_scan(qk, v, gc, grow, init if row0 else None, row0=row0, nseq=nseq, length=length,
                                 heads=m_heads, emit_state=not row0)
                if not row0:
                    outs["C"] = res[1][:, None]
                    outs["n"] = res[2][:, None, :, :, 0]
                    outs["m"] = res[3][:, None, :, :, 0, 0]
                y_parts.append(mlstm_post(res[0], o_gate, mlstm_head_norm[j], row0=row0, heads=m_heads))
            mix_in, w_out = jnp.concatenate(y_parts, axis=0), mlstm_w_out

        elif kind == 2:
            gn = s_groups * s_n
            conv = _make_epi_conv(tm_seq, seq_of_tile, True)
            z = matmul(h1, ssd_w_in, w_idx=j, n=s_inner, name="ssd_in_z")
            xs = matmul(h1, ssd_w_in, w_idx=j, n0=s_inner, n=s_inner, tm=tm_seq,
                        extras=conv_extras(ssd_conv_w[j], ssd_conv_b[j], ones(s_inner), 0),
                        epi=conv, name="ssd_in_x")
            bc = matmul(h1, ssd_w_in, w_idx=j, n0=2 * s_inner, n=2 * gn, tm=tm_seq, out_dtypes=(BF16,),
                        extras=conv_extras(ssd_conv_w[j], ssd_conv_b[j], ones(2 * gn), s_inner),
                        epi=conv, name="ssd_in_bc")
            dt_raw = matmul(h1, ssd_w_in, w_idx=j, n0=2 * s_inner + 2 * gn, n=2 * s_heads, x3=True, tm=512, name="ssd_dt")
            dt = jax.nn.softplus(dt_raw.reshape(rows, 2, s_heads) + ssd_dt_bias[j])
            a = dt * (-jnp.exp(ssd_a_log[j]))
            to_col = lambda y: y.reshape(rows, 2, s_groups, s_hpg).transpose(1, 2, 0, 3)
            acol, dtcol = to_col(a), to_col(dt)
            init = state_ssd[:, j].reshape(dec_batch, 2, s_groups, s_hpg * s_p, s_n)
            d_skip_cols = jnp.repeat(ssd_d_skip[j], s_p)
            y_parts = []
            for row0, nseq, length in parts:
                t = _tile(length, SCAN_CHUNK, 8)
                n = nseq * length
                ac = acol[:, :, row0:row0 + n]
                arow = ac.reshape(2, s_groups, n // t, t, s_hpg).transpose(0, 1, 2, 4, 3)
                res = ssd_scan(xs, bc, ac, arow, dtcol[:, :, row0:row0 + n], init if row0 else None, row0=row0,
                               nseq=nseq, length=length, groups=s_groups, hpg=s_hpg, emit_state=not row0)
                if not row0:
                    outs["ssd"] = res[1].reshape(batch, 1, 2, s_heads, s_p, s_n)
                y_parts.append(ssd_post(res[0], xs, z, d_skip_cols, ssd_norm[j], row0=row0))
            mix_in, w_out = jnp.concatenate(y_parts, axis=0), ssd_w_out

        else:
            proj = matmul(h1, hyena_w_in, w_idx=j, tm=tm_seq,
                          extras=conv_extras(hyena_conv_w[j], hyena_conv_b[j], ones(3 * dm), 0),
                          epi=_make_epi_conv(tm_seq, seq_of_tile, False), name="hyena_in")
            y_parts = []
            for row0, nseq, length in parts:
                fmat, gmat, tps, tqs = _hyena_filter_spectra(length, hyena_f_w1[j], hyena_f_b1[j], hyena_f_w2[j],
                                                             hyena_f_b2[j], hyena_f_w3[j], hyena_f_freq[j], dm)
                y = hyena_conv(proj, (row0, 0), fmat, gmat, tps[0], tqs[0], hyena_skip_bias[j][0], proj, (row0, dm),
                               nseq=nseq, length=length, d=dm, out_dtype=F32)
                y = hyena_conv(y, (0, 0), fmat, gmat, tps[1], tqs[1], hyena_skip_bias[j][1], proj, (row0, 2 * dm),
                               nseq=nseq, length=length, d=dm, out_dtype=BF16)
                y_parts.append(y)
            mix_in, w_out = jnp.concatenate(y_parts, axis=0), hyena_w_out

        tn = tn_of(dm)
        x = matmul(mix_in, w_out, w_idx=j, tm=tm_res, tn=tn,
                   extras=[(x, (tm_res, tn), lambda i_, j_: (i_, j_)),
                           (g1, (None, 1, tn), lambda i_, j_: (row_map(i_ * tm_res), 0, j_))],
                   epi=_epi_resid, name="mixer_out")
        w_rt, b_rt = moe_router_weights(moe_w_group, moe_b_group, moe_w_expert, moe_b_expert, i)
        h2, logits_t = norm_mod_router(x, norm_ffn[i], sc2, sh2, w_rt, row_map, tm_norm)
        x = hier_moe_residual(x, h2, logits_t, b_rt, g2, row_map, moe_w_gate, moe_w_up, moe_w_down, i,
                              n_grp=moe_w_group.shape[-1], tm=tm_norm)

    y_prompt = x[:n_ctx].reshape(batch, seq, dm)
    y_sample = x[n_ctx:].reshape(dec_batch, dec_seq, dm)
    return (y_prompt, y_sample, outs["k"], outs["v"], outs["C"], outs["n"], outs["m"], outs["ssd"])
```

```python
import functools
import math

import jax
import jax.numpy as jnp
from jax import lax
from jax.experimental import pallas as pl
from jax.experimental.pallas import tpu as pltpu

F32 = jnp.float32
BF16 = jnp.bfloat16
U32 = jnp.uint32
LANES = 128
NORM_EPS = 1e-6
ROPE_GRID_W = 64
ROPE_BASE = 10000.0
HYENA_EMB = 33
HYENA_DECAY_SHORT = 0.3
HYENA_DECAY_LONG = 1.5
HYENA_TARGET = 1e-2
MOE_TOP_K = 2
MOE_ROW_BLOCK = 512
MLSTM_CHUNK = 256
SSD_CHUNK = 256
VMEM_LIMIT = 56 * 1024 * 1024

_NT = (((1,), (1,)), ((), ()))
_TN = (((0,), (0,)), ((), ()))


def _tile(n, pref, align):
    if n <= pref:
        return n
    t = (pref // align) * align
    while t >= align:
        if n % t == 0:
            return t
        t -= align
    return n


def _params(sem):
    return pltpu.CompilerParams(dimension_semantics=sem, vmem_limit_bytes=VMEM_LIMIT)


def _dot(a, b):
    return jnp.dot(a, b, preferred_element_type=F32)


def _split_bf16(a):
    hi = a.astype(BF16)
    lo = (a - hi.astype(F32)).astype(BF16)
    return hi, lo


def _silu(x):
    return x / (1.0 + jnp.exp(-x))


def _mm_kernel(*refs, nk, x3, n_extra, n_out, epi):
    x_ref, w_ref = refs[:2]
    extra = refs[2:2 + n_extra]
    o_refs = refs[2 + n_extra:2 + n_extra + n_out]

    if x3:
        xh, xl = _split_bf16(x_ref[...].astype(F32))
        wh, wl = _split_bf16(w_ref[...].astype(F32))
        part = _dot(xh, wh) + _dot(xh, wl) + _dot(xl, wh)
    else:
        part = _dot(x_ref[...].astype(BF16), w_ref[...].astype(BF16))

    def finish(r):
        for o_ref, val in zip(o_refs, epi(r, extra, pl.program_id(0))):
            o_ref[...] = val.astype(o_ref.dtype)

    if nk == 1:
        finish(part)
        return
    acc_ref = refs[-1]
    k = pl.program_id(2)

    @pl.when(k == 0)
    def _():
        acc_ref[...] = part

    @pl.when(k > 0)
    def _():
        acc_ref[...] += part

    @pl.when(k == nk - 1)
    def _():
        finish(acc_ref[...])


def _epi_plain(r, extra, i):
    return (r,)


def _epi_bias(r, extra, i):
    return (r + extra[0][...],)


def _epi_resid(r, extra, i):
    return (extra[0][...] + extra[1][...] * r,)


def _epi_sigmoid(r, extra, i):
    return (1.0 / (1.0 + jnp.exp(-r)),)


def _epi_two(r, extra, i):
    return (r, r)


def matmul(x, w, *, w_idx=None, n0=0, n=None, tm=2048, tn=512, tk=2048, out_dtypes=(F32,), x3=False,
           extras=(), epi=_epi_plain, name="matmul"):
    m, kdim = x.shape
    n = w.shape[-1] if n is None else n
    tm = _tile(m, tm, 8)
    tk = _tile(kdim, tk, LANES)
    tn = _tile(n, tn, LANES)
    assert n0 % tn == 0 and m % tm == 0 and kdim % tk == 0 and n % tn == 0
    j0 = n0 // tn
    nk = kdim // tk
    if w.ndim == 3:
        w_spec = pl.BlockSpec((None, tk, tn), lambda i, j, k: (w_idx, k, j + j0))
    else:
        w_spec = pl.BlockSpec((tk, tn), lambda i, j, k: (k, j + j0))
    in_specs = [pl.BlockSpec((tm, tk), lambda i, j, k: (i, k)), w_spec]
    args = [x, w]
    for arr, shape, imap in extras:
        in_specs.append(pl.BlockSpec(shape, lambda i, j, k, imap=imap: imap(i, j)))
        args.append(arr)
    outs = pl.pallas_call(
        functools.partial(_mm_kernel, nk=nk, x3=x3, n_extra=len(extras), n_out=len(out_dtypes), epi=epi),
        grid=(m // tm, n // tn, nk),
        in_specs=in_specs,
        out_specs=[pl.BlockSpec((tm, tn), lambda i, j, k: (i, j)) for _ in out_dtypes],
        out_shape=[jax.ShapeDtypeStruct((m, n), dt) for dt in out_dtypes],
        scratch_shapes=[pltpu.VMEM((tm, tn), F32)] if nk > 1 else [],
        compiler_params=_params(("parallel", "parallel", "arbitrary")),
        name=name,
    )(*args)
    return outs[0] if len(out_dtypes) == 1 else outs


def _row_vec(v, tn, col0=0):
    return (v.reshape(1, -1), (1, tn), lambda i, j, c=col0 // tn: (0, j + c))


def _make_epi_conv(tm, seq_of_tile, act):
    def epi(r, extra, i):
        taps = extra[0][...]
        length = seq_of_tile(i)
        pos = lax.broadcasted_iota(jnp.int32, (tm, 1), 0) & (length - 1)
        prev = jnp.where(pos == 0, 0.0, pltpu.roll(r, 1, 0))
        nxt = jnp.where(pos == length - 1, 0.0, pltpu.roll(r, tm - 1, 0))
        y = prev * taps[0:1] + r * taps[1:2] + nxt * taps[2:3] + extra[1][...]
        if act:
            y = _silu(y)
        return (y * extra[2][...],)
    return epi


def _make_epi_qk(tm, tn, hd, n_ctx, n_out):
    def epi(r, extra, i):
        ga = lax.broadcasted_iota(jnp.int32, (LANES, LANES), 0) // hd
        gb = lax.broadcasted_iota(jnp.int32, (LANES, LANES), 1) // hd
        avg = jnp.where(ga == gb, 1.0 / hd, 0.0).astype(BF16)
        lane = lax.broadcasted_iota(jnp.int32, (1, LANES), 1)
        first_half = (lane & (hd // 2 - 1)) < hd // 4
        rope_on = i * tm >= n_ctx
        cos = extra[1][...]
        sin = extra[2][...]
        rot, plain = [], []
        for c0 in range(0, tn, LANES):
            xc = r[:, c0:c0 + LANES]
            hi, lo = _split_bf16(xc * xc)
            y = xc * lax.rsqrt(_dot(hi, avg) + _dot(lo, avg) + NORM_EPS) * extra[0][:, c0:c0 + LANES]
            partner = jnp.where(first_half, pltpu.roll(y, LANES - hd // 4, 1), pltpu.roll(y, hd // 4, 1))
            rot.append(jnp.where(rope_on, y * cos + partner * sin, y))
            plain.append(y)
        outs = (jnp.concatenate(rot, axis=1),)
        if n_out == 2:
            outs += (jnp.concatenate(plain, axis=1),)
        return outs
    return epi


def _norm_mod_kernel(x_ref, w_ref, sc_ref, sh_ref, o_ref):
    x = x_ref[...]
    y = x * lax.rsqrt(jnp.mean(x * x, axis=-1, keepdims=True) + NORM_EPS) * w_ref[...]
    o_ref[...] = (y * (1.0 + sc_ref[...]) + sh_ref[...]).astype(o_ref.dtype)


def norm_mod(x, w, scale, shift, row_map, tm):
    rows, d = x.shape
    row = pl.BlockSpec((tm, d), lambda i: (i, 0))
    mod = pl.BlockSpec((None, 1, d), lambda i: (row_map(i * tm), 0, 0))
    return pl.pallas_call(
        _norm_mod_kernel,
        grid=(rows // tm,),
        in_specs=[row, pl.BlockSpec((1, d), lambda i: (0, 0)), mod, mod],
        out_specs=row,
        out_shape=jax.ShapeDtypeStruct((rows, d), BF16),
        compiler_params=_params(("parallel",)),
        name="norm_mod",
    )(x, w.reshape(1, d), scale, shift)


def _store_slab(ref, val, slab):
    n = val.shape[0]
    for s in range(slab):
        ref[pl.ds(s, n, stride=slab), :] = val[:, s * LANES:(s + 1) * LANES]


def _load_slab(ref, n, slab):
    return jnp.concatenate([ref[pl.ds(s, n, stride=slab), :] for s in range(slab)], axis=1)


def _pack_bf16_pairs(x):
    c = x.shape[1] // 2
    hi = lax.bitcast_convert_type(x[:, :c].astype(BF16).astype(F32), U32)
    lo = lax.bitcast_convert_type(x[:, c:].astype(BF16).astype(F32), U32)
    return hi | (lo >> 16)


def _unpack_hi(p):
    return lax.bitcast_convert_type(p & jnp.uint32(0xFFFF0000), F32)


def _unpack_lo(p):
    return lax.bitcast_convert_type(p << 16, F32)


def _slab_at(ref, row, n, slab):
    return ref.at[pl.ds(pl.multiple_of(row * slab, slab), n * slab)]


def _norm_router_kernel(x_ref, w_ref, sc_ref, sh_ref, wr_ref, h_ref, lt_ref, *, slab):
    x = x_ref[...]
    y = x * lax.rsqrt(jnp.mean(x * x, axis=-1, keepdims=True) + NORM_EPS) * w_ref[...]
    h = y * (1.0 + sc_ref[...]) + sh_ref[...]
    _store_slab(h_ref, _pack_bf16_pairs(h), slab)
    hh, hl = _split_bf16(h)
    wh, wl = _split_bf16(wr_ref[...])
    nt = lambda a, b: lax.dot_general(a, b, _NT, preferred_element_type=F32)
    lt_ref[...] = nt(wh, hh) + nt(wh, hl) + nt(wl, hh)


def norm_mod_router(x, w, scale, shift, w_router_t, row_map, tm):
    rows, d = x.shape
    nr = w_router_t.shape[0]
    slab = d // 2 // LANES
    row = pl.BlockSpec((tm, d), lambda i: (i, 0))
    mod = pl.BlockSpec((None, 1, d), lambda i: (row_map(i * tm), 0, 0))
    return pl.pallas_call(
        functools.partial(_norm_router_kernel, slab=slab),
        grid=(rows // tm,),
        in_specs=[row, pl.BlockSpec((1, d), lambda i: (0, 0)), mod, mod, pl.BlockSpec((nr, d), lambda i: (0, 0))],
        out_specs=[pl.BlockSpec((tm * slab, LANES), lambda i: (i, 0)), pl.BlockSpec((nr, tm), lambda i: (0, i))],
        out_shape=[jax.ShapeDtypeStruct((rows * slab, LANES), U32), jax.ShapeDtypeStruct((nr, rows), F32)],
        compiler_params=_params(("parallel",)),
        name="norm_mod_router",
    )(x, w.reshape(1, d), scale, shift, w_router_t)


def _attn_kernel(lam_ref, q_ref, k_ref, v_ref, *rest, half, scale, post_scale, has_cache):
    if has_cache:
        kc_ref, vc_ref, sn_ref, o_ref = rest
    else:
        sn_ref, o_ref = rest
    q = q_ref[...].astype(F32) * scale
    k = k_ref[...]
    lane = lax.broadcasted_iota(jnp.int32, q.shape, 1)
    nt = lambda a, b: lax.dot_general(a, b, _NT, preferred_element_type=F32)
    if has_cache:
        kc = kc_ref[...].astype(BF16)

    def exps(qm):
        s = nt(qm, k)
        m = jnp.max(s, axis=-1, keepdims=True)
        if not has_cache:
            e = jnp.exp(s - m)
            return e, None, 1.0 / jnp.sum(e, axis=-1, keepdims=True)
        sc = nt(qm, kc)
        m = jnp.maximum(m, jnp.max(sc, axis=-1, keepdims=True))
        e = jnp.exp(s - m)
        ec = jnp.exp(sc - m)
        return e, ec, 1.0 / (jnp.sum(e, axis=-1, keepdims=True) + jnp.sum(ec, axis=-1, keepdims=True))

    e1, e1c, r1 = exps(jnp.where(lane < half, q, 0.0).astype(BF16))
    e2, e2c, r2 = exps(jnp.where(lane >= half, q, 0.0).astype(BF16))
    r2 = lam_ref[0] * r2
    o = _dot((e1 * r1 - e2 * r2).astype(BF16), v_ref[...])
    if has_cache:
        o = o + _dot((e1c * r1 - e2c * r2).astype(BF16), vc_ref[...].astype(BF16))
    o = o * lax.rsqrt(jnp.mean(o * o, axis=-1, keepdims=True) + NORM_EPS) * sn_ref[...] * post_scale
    o_ref[...] = o.astype(o_ref.dtype)


def diff_attention(q, k, v, cache, lam, sub_norm, *, row0, nseq, length, heads, post_scale):
    hd2 = q.shape[1] // heads
    tq = _tile(length, 256, 8)
    nq = length // tq
    in_specs = [
        pl.BlockSpec(memory_space=pltpu.SMEM),
        pl.BlockSpec((tq, hd2), lambda s, h, i: (row0 // tq + s * nq + i, h)),
        pl.BlockSpec((length, hd2), lambda s, h, i: (row0 // length + s, h)),
        pl.BlockSpec((length, hd2), lambda s, h, i: (row0 // length + s, h)),
    ]
    args = [lam.reshape(1), q, k, v]
    if cache is not None:
        past = cache[0].shape[0] // nseq
        in_specs += [pl.BlockSpec((past, hd2), lambda s, h, i: (s, h))] * 2
        args += list(cache)
    in_specs.append(pl.BlockSpec((1, hd2), lambda s, h, i: (0, 0)))
    args.append(sub_norm.reshape(1, hd2))
    return pl.pallas_call(
        functools.partial(_attn_kernel, half=hd2 // 2, scale=(hd2 // 2) ** -0.5, post_scale=post_scale,
                          has_cache=cache is not None),
        grid=(nseq, heads, nq),
        in_specs=in_specs,
        out_specs=pl.BlockSpec((tq, hd2), lambda s, h, i: (s * nq + i, h)),
        out_shape=jax.ShapeDtypeStruct((nseq * length, q.shape[1]), BF16),
        compiler_params=_params(("parallel", "parallel", "parallel")),
        name="diff_attention",
    )(*args)


def _causal_masks(t, d):
    sign = jnp.where(d == 0, 1, -1)
    r = lax.broadcasted_iota(jnp.int32, (t, t), 0)
    s = lax.broadcasted_iota(jnp.int32, (t, t), 1)
    diff = (s - r) * sign
    return diff <= 0, diff >= 0


def _chunk_row(s, d, c, nc):
    return s * nc + jnp.where(d == 0, c, nc - 1 - c)


def _mlstm_kernel(*refs, t, nc, has_init, emit_state):
    refs = list(refs)
    q_ref, k_ref, v_ref, gcol_ref, grow_ref = refs[:5]
    pos = 5
    if has_init:
        c0_ref, n0_ref, m0_ref = refs[pos:pos + 3]
        pos += 3
    h_ref = refs[pos]
    pos += 1
    if emit_state:
        co_ref, no_ref, mo_ref = refs[pos:pos + 3]
        pos += 3
    c_s, n_s, m_s = refs[pos:pos + 3]
    d = pl.program_id(1)
    c = pl.program_id(3)

    @pl.when(c == 0)
    def _():
        if has_init:
            c_s[...] = c0_ref[...]
            n_s[...] = n0_ref[...]
            m_s[...] = m0_ref[...]
        else:
            c_s[...] = jnp.zeros_like(c_s)
            n_s[...] = jnp.zeros_like(n_s)
            m_s[...] = jnp.zeros_like(m_s)

    q = q_ref[...]
    k = k_ref[...]
    v = v_ref[...]
    i_col = gcol_ref[:, 0:1]
    f_col = gcol_ref[:, 1:2]
    i_row = grow_ref[0:1, :]
    f_row = grow_ref[1:2, :]
    causal, causal_t = _causal_masks(t, d)
    b_col = jnp.sum(jnp.where(causal, f_row, 0.0), axis=1, keepdims=True)
    b_row = jnp.sum(jnp.where(causal_t, f_col, 0.0), axis=0, keepdims=True)
    dlog = jnp.where(causal, b_col - b_row + i_row, -jnp.inf)
    m_prev = m_s[...]
    inter = b_col + m_prev
    mt = jnp.maximum(inter, jnp.max(dlog, axis=1, keepdims=True))
    w_intra = jnp.exp(dlog - mt)
    w_inter = jnp.exp(inter - mt)
    cm = c_s[...]
    nv = n_s[...]
    sm = lax.dot_general(q, k, _NT, preferred_element_type=F32) * w_intra
    num = _dot(sm.astype(BF16), v) + w_inter * _dot(q, cm.astype(BF16))
    den = jnp.sum(sm, axis=1, keepdims=True) + w_inter * jnp.sum(q.astype(F32) * nv, axis=1, keepdims=True)
    h_ref[...] = num / jnp.maximum(jnp.abs(den), jnp.exp(-mt))

    rowid = lax.broadcasted_iota(jnp.int32, (t, 1), 0)
    end_row = jnp.where(d == 0, t - 1, 0)
    m_new = jnp.sum(jnp.where(rowid == end_row, mt, 0.0), axis=0, keepdims=True)
    b_last = jnp.sum(f_row, axis=1, keepdims=True)
    kw = k.astype(F32) * jnp.exp(b_last - b_col + i_col - m_new)
    decay = jnp.exp(b_last + m_prev - m_new)
    c_new = decay * cm + lax.dot_general(kw.astype(BF16), v, _TN, preferred_element_type=F32)
    n_new = decay * nv + jnp.sum(kw, axis=0, keepdims=True)
    c_s[...] = c_new
    n_s[...] = n_new
    m_s[...] = m_new
    if emit_state:
        @pl.when(c == nc - 1)
        def _():
            co_ref[...] = c_new
            no_ref[...] = n_new
            mo_ref[...] = m_new


def mlstm_scan(qk, v, gcol, grow, init, *, row0, nseq, length, heads, emit_state):
    n = nseq * length
    dk = qk.shape[1] // (2 * heads)
    dv = v.shape[1] // heads
    t = _tile(length, MLSTM_CHUNK, 8)
    nc = length // t
    rc = lambda s, d, h, c: _chunk_row(s, d, c, nc)
    r0 = row0 // t
    in_specs = [
        pl.BlockSpec((t, dk), lambda s, d, h, c: (r0 + rc(s, d, h, c), h)),
        pl.BlockSpec((t, dk), lambda s, d, h, c: (r0 + rc(s, d, h, c), heads + h)),
        pl.BlockSpec((t, dv), lambda s, d, h, c: (r0 + rc(s, d, h, c), h)),
        pl.BlockSpec((None, None, t, 2), lambda s, d, h, c: (d, h, rc(s, d, h, c), 0)),
        pl.BlockSpec((None, None, None, 2, t), lambda s, d, h, c: (d, h, rc(s, d, h, c), 0, 0)),
    ]
    args = [qk, qk, v, gcol, grow]
    st_specs = [
        pl.BlockSpec((None, None, None, dk, dv), lambda s, d, h, c: (s, d, h, 0, 0)),
        pl.BlockSpec((None, None, None, 1, dk), lambda s, d, h, c: (s, d, h, 0, 0)),
        pl.BlockSpec((None, None, None, 1, 1), lambda s, d, h, c: (s, d, h, 0, 0)),
    ]
    if init is not None:
        in_specs += st_specs
        args += list(init)
    out_specs = [pl.BlockSpec((None, t, dv), lambda s, d, h, c: (d, rc(s, d, h, c), h))]
    out_shape = [jax.ShapeDtypeStruct((2, n, heads * dv), F32)]
    if emit_state:
        out_specs += st_specs
        out_shape += [jax.ShapeDtypeStruct((nseq, 2, heads, dk, dv), F32),
                      jax.ShapeDtypeStruct((nseq, 2, heads, 1, dk), F32),
                      jax.ShapeDtypeStruct((nseq, 2, heads, 1, 1), F32)]
    return pl.pallas_call(
        functools.partial(_mlstm_kernel, t=t, nc=nc, has_init=init is not None, emit_state=emit_state),
        grid=(nseq, 2, heads, nc),
        in_specs=in_specs,
        out_specs=out_specs,
        out_shape=out_shape,
        scratch_shapes=[pltpu.VMEM((dk, dv), F32), pltpu.VMEM((1, dk), F32), pltpu.VMEM((1, 1), F32)],
        compiler_params=_params(("parallel", "parallel", "parallel", "arbitrary")),
        name="mlstm_scan",
    )(*args)


def _mlstm_post_kernel(h_ref, g_ref, w_ref, o_ref):
    hs = h_ref[0] + h_ref[1]
    y = hs * lax.rsqrt(jnp.mean(hs * hs, axis=-1, keepdims=True) + NORM_EPS) * w_ref[...]
    o_ref[...] = (y * g_ref[...]).astype(o_ref.dtype)


def mlstm_post(h, o_gate, head_norm, *, row0, heads):
    _, n, width = h.shape
    dv = width // heads
    tm = _tile(n, 512, 8)
    return pl.pallas_call(
        _mlstm_post_kernel,
        grid=(n // tm, heads),
        in_specs=[pl.BlockSpec((2, tm, dv), lambda i, hh: (0, i, hh)),
                  pl.BlockSpec((tm, dv), lambda i, hh: (row0 // tm + i, hh)),
                  pl.BlockSpec((1, dv), lambda i, hh: (0, 0))],
        out_specs=pl.BlockSpec((tm, dv), lambda i, hh: (i, hh)),
        out_shape=jax.ShapeDtypeStruct((n, width), BF16),
        compiler_params=_params(("parallel", "parallel")),
        name="mlstm_post",
    )(h, o_gate, head_norm.reshape(1, dv))


def _ssd_kernel(*refs, t, nc, hpg, p, has_init, emit_state):
    refs = list(refs)
    x_ref, b_ref, c_ref, acol_ref, arow_ref, dtcol_ref = refs[:6]
    pos = 6
    if has_init:
        s0_ref = refs[pos]
        pos += 1
    y_ref = refs[pos]
    pos += 1
    if emit_state:
        so_ref = refs[pos]
        pos += 1
    st = refs[pos]
    d = pl.program_id(1)
    c = pl.program_id(3)
    width = hpg * p

    @pl.when(c == 0)
    def _():
        if has_init:
            st[...] = s0_ref[...].T
        else:
            st[...] = jnp.zeros_like(st)

    causal, causal_t = _causal_masks(t, d)
    lane_head = lax.broadcasted_iota(jnp.int32, (1, width), 1) // p
    lane_pair = lax.broadcasted_iota(jnp.int32, (1, 2 * p), 1)
    bm = b_ref[...]
    cmat = c_ref[...]
    dt_e = jnp.zeros((t, width), F32)
    for e in range(hpg):
        dt_e = jnp.where(lane_head == e, dtcol_ref[:, e:e + 1], dt_e)
    xd = x_ref[...] * dt_e
    xdb = xd.astype(BF16)
    cb = lax.dot_general(cmat, bm, _NT, preferred_element_type=F32)
    cum_e = jnp.zeros((t, width), F32)
    tot_e = jnp.zeros((1, width), F32)
    y_pairs = []
    for e in range(hpg):
        a_row = arow_ref[e:e + 1, :]
        a_col = acol_ref[:, e:e + 1]
        cum_col = jnp.sum(jnp.where(causal, a_row, 0.0), axis=1, keepdims=True)
        cum_row = jnp.sum(jnp.where(causal_t, a_col, 0.0), axis=0, keepdims=True)
        wmat = (cb * jnp.exp(jnp.where(causal, cum_col - cum_row, -jnp.inf))).astype(BF16)
        j = e // 2
        yp = _dot(wmat, xdb[:, j * 2 * p:(j + 1) * 2 * p])
        if e % 2 == 0:
            y_even = yp
        else:
            y_pairs.append(jnp.where(lane_pair < p, y_even, yp))
        cum_e = jnp.where(lane_head == e, cum_col, cum_e)
        tot_e = jnp.where(lane_head == e, jnp.sum(a_row, axis=1, keepdims=True), tot_e)
    s_prev = st[...]
    y_off = _dot(cmat, s_prev.astype(BF16)) * jnp.exp(cum_e)
    y_ref[...] = jnp.concatenate(y_pairs, axis=1) + y_off
    xw = (xd * jnp.exp(tot_e - cum_e)).astype(BF16)
    s_new = jnp.exp(tot_e) * s_prev + lax.dot_general(bm, xw, _TN, preferred_element_type=F32)
    st[...] = s_new
    if emit_state:
        @pl.when(c == nc - 1)
        def _():
            so_ref[...] = s_new.T


def ssd_scan(x, bc, acol, arow, dtcol, init, *, row0, nseq, length, groups, hpg, emit_state):
    n = nseq * length
    inner = x.shape[1]
    width = inner // groups
    p = width // hpg
    ns = bc.shape[1] // (2 * groups)
    t = _tile(length, SSD_CHUNK, 8)
    nc = length // t
    rc = lambda s, d, g, c: _chunk_row(s, d, c, nc)
    r0 = row0 // t
    in_specs = [
        pl.BlockSpec((t, width), lambda s, d, g, c: (r0 + rc(s, d, g, c), g)),
        pl.BlockSpec((t, ns), lambda s, d, g, c: (r0 + rc(s, d, g, c), g)),
        pl.BlockSpec((t, ns), lambda s, d, g, c: (r0 + rc(s, d, g, c), groups + g)),
        pl.BlockSpec((None, None, t, hpg), lambda s, d, g, c: (d, g, rc(s, d, g, c), 0)),
        pl.BlockSpec((None, None, None, hpg, t), lambda s, d, g, c: (d, g, rc(s, d, g, c), 0, 0)),
        pl.BlockSpec((None, None, t, hpg), lambda s, d, g, c: (d, g, rc(s, d, g, c), 0)),
    ]
    args = [x, bc, bc, acol, arow, dtcol]
    st_spec = pl.BlockSpec((None, None, None, width, ns), lambda s, d, g, c: (s, d, g, 0, 0))
    if init is not None:
        in_specs.append(st_spec)
        args.append(init)
    out_specs = [pl.BlockSpec((None, t, width), lambda s, d, g, c: (d, rc(s, d, g, c), g))]
    out_shape = [jax.ShapeDtypeStruct((2, n, inner), F32)]
    if emit_state:
        out_specs.append(st_spec)
        out_shape.append(jax.ShapeDtypeStruct((nseq, 2, groups, width, ns), F32))
    return pl.pallas_call(
        functools.partial(_ssd_kernel, t=t, nc=nc, hpg=hpg, p=p, has_init=init is not None, emit_state=emit_state),
        grid=(nseq, 2, groups, nc),
        in_specs=in_specs,
        out_specs=out_specs,
        out_shape=out_shape,
        scratch_shapes=[pltpu.VMEM((ns, width), F32)],
        compiler_params=_params(("parallel", "parallel", "parallel", "arbitrary")),
        name="ssd_scan",
    )(*args)


def _ssd_post_kernel(y_ref, x_ref, z_ref, dsk_ref, w_ref, o_ref):
    y = (y_ref[0] + y_ref[1] + x_ref[...] * dsk_ref[...]) * _silu(z_ref[...])
    o_ref[...] = (y * lax.rsqrt(jnp.mean(y * y, axis=-1, keepdims=True) + NORM_EPS) * w_ref[...]).astype(o_ref.dtype)


def ssd_post(y, x, z, d_skip_cols, norm_w, *, row0):
    _, n, inner = y.shape
    tm = _tile(n, 128, 8)
    row = pl.BlockSpec((tm, inner), lambda i: (row0 // tm + i, 0))
    vec = pl.BlockSpec((1, inner), lambda i: (0, 0))
    return pl.pallas_call(
        _ssd_post_kernel,
        grid=(n // tm,),
        in_specs=[pl.BlockSpec((2, tm, inner), lambda i: (0, i, 0)), row, row, vec, vec],
        out_specs=pl.BlockSpec((tm, inner), lambda i: (i, 0)),
        out_shape=jax.ShapeDtypeStruct((n, inner), BF16),
        compiler_params=_params(("parallel",)),
        name="ssd_post",
    )(y, x, z, d_skip_cols.reshape(1, inner), norm_w.reshape(1, inner))


def _hyena_kernel(u_ref, f_ref, g_ref, tp_ref, tq_ref, bias_ref, xm_ref, o_ref, *, length):
    u = u_ref[...]
    a = _dot(f_ref[...], u.astype(BF16))
    a_sw = jnp.concatenate([a[length:], a[:length]], axis=0)
    y = a * tp_ref[...] + a_sw * tq_ref[...]
    conv = _dot(g_ref[...], y.astype(BF16))
    o_ref[...] = ((conv + u * bias_ref[...]) * xm_ref[...]).astype(o_ref.dtype)


def hyena_conv(u, u_at, fmat, gmat, tp, tq, bias, xm, xm_at, *, nseq, length, d, out_dtype):
    tc = _tile(d, 256, LANES)
    n2 = 2 * length
    (u_r0, u_c0), (x_r0, x_c0) = u_at, xm_at
    return pl.pallas_call(
        functools.partial(_hyena_kernel, length=length),
        grid=(d // tc, nseq),
        in_specs=[
            pl.BlockSpec((length, tc), lambda j, s: (u_r0 // length + s, u_c0 // tc + j)),
            pl.BlockSpec((n2, length), lambda j, s: (0, 0)),
            pl.BlockSpec((length, n2), lambda j, s: (0, 0)),
            pl.BlockSpec((n2, tc), lambda j, s: (0, j)),
            pl.BlockSpec((n2, tc), lambda j, s: (0, j)),
            pl.BlockSpec((1, tc), lambda j, s: (0, j)),
            pl.BlockSpec((length, tc), lambda j, s: (x_r0 // length + s, x_c0 // tc + j)),
        ],
        out_specs=pl.BlockSpec((length, tc), lambda j, s: (s, j)),
        out_shape=jax.ShapeDtypeStruct((nseq * length, d), out_dtype),
        compiler_params=_params(("parallel", "parallel")),
        name="hyena_conv",
    )(u, fmat, gmat, tp, tq, bias.reshape(1, d), xm)


def _dft_matrices(length):
    n2 = 2 * length
    r = jnp.arange(n2, dtype=jnp.int32)
    kfreq = jnp.where(r <= length, r, r - length)
    ang = ((kfreq[:, None] * r[None, :]) % n2).astype(F32) * (2.0 * math.pi / n2)
    full = jnp.where((r > length)[:, None], -jnp.sin(ang), jnp.cos(ang))
    wk = jnp.where((r == 0) | (r == length), 1.0, 2.0) / n2
    inv = (full[:, :length] * wk[:, None]).T
    return full, full[:, :length].astype(BF16), inv.astype(BF16)


def _hyena_filter_spectra(length, f_w1, f_b1, f_w2, f_b2, f_w3, f_freq, d):
    hp = lax.Precision.HIGHEST
    t = jnp.linspace(0.0, 1.0, length, dtype=F32)[:, None]
    bands = (HYENA_EMB - 1) // 2
    f = jnp.linspace(1e-4, bands - 1, bands, dtype=F32)
    w = 2.0 * math.pi * jnp.arange(length, dtype=F32)[:, None] / length
    z = jnp.concatenate([t, jnp.cos(f * w), -jnp.sin(f * w)], axis=-1)
    hid = jnp.sin(f_freq * (jnp.dot(z, f_w1, precision=hp) + f_b1))
    hid = jnp.sin(f_freq * (jnp.dot(hid, f_w2, precision=hp) + f_b2))
    filt = matmul(hid, f_w3, x3=True, tk=hid.shape[1], name="hyena_filter").reshape(length, 2, 2, d)
    min_decay = math.log(HYENA_TARGET) / HYENA_DECAY_LONG
    max_decay = math.log(HYENA_TARGET) / HYENA_DECAY_SHORT
    deltas = jnp.linspace(min_decay, max_decay, d, dtype=F32)
    filt = filt * jnp.exp(-t * jnp.abs(deltas))[:, None, None, :]
    full, fmat, gmat = _dft_matrices(length)
    tps, tqs = [], []
    for o in range(2):
        h_fwd, h_bwd = filt[:, o, 0], filt[:, o, 1]
        taps = jnp.concatenate([h_fwd, jnp.zeros((1, d), F32), h_bwd[:0:-1]], axis=0)
        taps = taps / jnp.sum(jnp.abs(taps), axis=0, keepdims=True)
        spec = matmul(full, taps, x3=True, tm=512, name="hyena_filter_dft")
        top, bot = spec[:length], spec[length:]
        zero = jnp.zeros((1, d), F32)
        im = jnp.concatenate([zero, bot[1:]], axis=0)
        tps.append(jnp.concatenate([top, bot[0:1], top[1:]], axis=0))
        tqs.append(jnp.concatenate([-im, im], axis=0))
    return fmat, gmat, tps, tqs


def _router_kernel(l_ref, b_ref, gates_ref, dest_ref, cnt_ref, pre_s, *, n_grp, epg, tb, chunk):
    n_exp = n_grp * epg
    rows = l_ref.shape[1]
    lg = l_ref[...] + b_ref[...]
    g = lg[n_exp:n_exp + n_grp]
    ge = jnp.exp(g - jnp.max(g, axis=0, keepdims=True))
    gp = ge / jnp.sum(ge, axis=0, keepdims=True)
    gpm = jnp.max(gp, axis=0, keepdims=True)
    gi = lax.broadcasted_iota(jnp.int32, gp.shape, 0)
    gsel = jnp.min(jnp.where(gp == gpm, gi, n_grp), axis=0, keepdims=True)
    e_in = lg[0:epg]
    for q in range(1, n_grp):
        e_in = jnp.where(gsel == q, lg[q * epg:(q + 1) * epg], e_in)
    ee = jnp.exp(e_in - jnp.max(e_in, axis=0, keepdims=True))
    ep = ee / jnp.sum(ee, axis=0, keepdims=True)
    ei = lax.broadcasted_iota(jnp.int32, ep.shape, 0)
    p1 = jnp.max(ep, axis=0, keepdims=True)
    s1 = jnp.min(jnp.where(ep == p1, ei, epg), axis=0, keepdims=True)
    ep2 = jnp.where(ei == s1, -1.0, ep)
    p2 = jnp.max(ep2, axis=0, keepdims=True)
    s2 = jnp.min(jnp.where(ep2 == p2, ei, epg), axis=0, keepdims=True)
    den = p1 + p2
    gates_ref[0:1, :] = gpm * p1 / den
    gates_ref[1:2, :] = gpm * p2 / den

    eio = lax.broadcasted_iota(jnp.int32, (n_exp, rows), 0)
    oh0 = eio == gsel * epg + s1
    oh1 = eio == gsel * epg + s2
    cnt = oh0.astype(F32) + oh1.astype(F32)
    ca = lax.broadcasted_iota(jnp.int32, (chunk, chunk), 0)
    cb = lax.broadcasted_iota(jnp.int32, (chunk, chunk), 1)
    tri = (ca < cb).astype(BF16)
    carry = jnp.zeros((n_exp, 1), F32)
    for c0 in range(0, rows, chunk):
        c = cnt[:, c0:c0 + chunk]
        pre_s[:, c0:c0 + chunk] = _dot(c.astype(BF16), tri) + carry
        carry = carry + jnp.sum(c, axis=1, keepdims=True)
    nblk = jnp.floor((carry + (tb - 1)) * (1.0 / tb))
    sa = lax.broadcasted_iota(jnp.int32, (n_exp, n_exp), 0)
    sb = lax.broadcasted_iota(jnp.int32, (n_exp, n_exp), 1)
    nblk_row = jnp.sum(jnp.where(sa == sb, nblk, 0.0), axis=0, keepdims=True)
    blk_start = jnp.sum(jnp.where(sb < sa, nblk_row, 0.0), axis=1, keepdims=True)
    slot = blk_start * tb + pre_s[...]
    dest_ref[0:1, :] = jnp.sum(jnp.where(oh0, slot, 0.0), axis=0, keepdims=True).astype(jnp.int32)
    dest_ref[1:2, :] = jnp.sum(jnp.where(oh1, slot, 0.0), axis=0, keepdims=True).astype(jnp.int32)
    cnt_ref[...] = carry.astype(jnp.int32)


def moe_route(logits_t, bias, *, n_grp, epg, tb):
    nr, rows = logits_t.shape
    n_exp = n_grp * epg
    assert tb & (tb - 1) == 0
    return pl.pallas_call(
        functools.partial(_router_kernel, n_grp=n_grp, epg=epg, tb=tb, chunk=_tile(rows, 512, LANES)),
        out_shape=[jax.ShapeDtypeStruct((MOE_TOP_K, rows), F32),
                   jax.ShapeDtypeStruct((MOE_TOP_K, rows), jnp.int32),
                   jax.ShapeDtypeStruct((n_exp, 1), jnp.int32)],
        scratch_shapes=[pltpu.VMEM((n_exp, rows), F32)],
        compiler_params=pltpu.CompilerParams(vmem_limit_bytes=VMEM_LIMIT),
        name="moe_route",
    )(logits_t, bias)


def _dispatch_kernel(dest_ref, cnt_ref, start_ref, nu_ref, h_ref, xs_ref, zero_s, sem, zsem, *,
                     rows, tm, tb, n_exp, n_blocks, slab):
    i = pl.program_id(0)

    def zero_copy(row):
        return pltpu.make_async_copy(zero_s, _slab_at(xs_ref, row, tb, slab), zsem)

    @pl.when(i == 0)
    def _():
        zero_s[...] = jnp.zeros_like(zero_s)
        for wait in (False, True):
            def go(row):
                if wait:
                    zero_copy(row).wait()
                else:
                    zero_copy(row).start()

            def last_block(e, carry):
                @pl.when(cnt_ref[e] > 0)
                def _():
                    go(start_ref[e] + ((cnt_ref[e] - 1) & ~(tb - 1)))
                return carry

            def tail_block(b, carry):
                go(b * tb)
                return carry

            lax.fori_loop(0, n_exp, last_block, 0)
            lax.fori_loop(nu_ref[0], n_blocks, tail_block, 0)

    def row_copy(r, slot):
        return pltpu.make_async_copy(_slab_at(h_ref, r, 1, slab), _slab_at(xs_ref, slot, 1, slab), sem)

    def issue(r, carry):
        row_copy(r, dest_ref[i * tm + r]).start()
        row_copy(r, dest_ref[rows + i * tm + r]).start()
        return carry

    def drain(r, carry):
        row_copy(0, 0).wait()
        row_copy(0, 0).wait()
        return carry

    lax.fori_loop(0, tm, issue, 0)
    lax.fori_loop(0, tm, drain, 0)


def _combine_kernel(dest_ref, x_ref, g2_ref, gt_ref, ys_ref, o_ref, buf, sem, *, rows, tm, slab):
    base = pl.program_id(0) * tm

    def row_copy(k, r, slot):
        return pltpu.make_async_copy(_slab_at(ys_ref, slot, 1, slab), _slab_at(buf.at[k], r, 1, slab), sem)

    def issue(r, carry):
        row_copy(0, r, dest_ref[base + r]).start()
        row_copy(1, r, dest_ref[rows + base + r]).start()
        return carry

    def drain(r, carry):
        row_copy(0, 0, 0).wait()
        row_copy(0, 0, 0).wait()
        return carry

    lax.fori_loop(0, tm, issue, 0)
    lax.fori_loop(0, tm, drain, 0)
    g0 = gt_ref[:, 0:1]
    g1 = gt_ref[:, 1:2]
    half = slab * LANES
    for s in range(slab):
        p0 = buf[0, pl.ds(s, tm, stride=slab), :]
        p1 = buf[1, pl.ds(s, tm, stride=slab), :]
        for unpack, c0 in ((_unpack_hi, s * LANES), (_unpack_lo, half + s * LANES)):
            cols = slice(c0, c0 + LANES)
            o_ref[:, cols] = x_ref[:, cols] + g2_ref[:, cols] * (g0 * unpack(p0) + g1 * unpack(p1))


def _moe_up_kernel(be_ref, nu_ref, x_ref, wg_ref, wu_ref, o_ref, *, fchunk, slab):
    used = pl.program_id(0) < nu_ref[0]

    @pl.when(used)
    def _():
        p = _load_slab(x_ref, o_ref.shape[0], slab)
        x = jnp.concatenate([_unpack_hi(p), _unpack_lo(p)], axis=1).astype(BF16)
        for f0 in range(0, o_ref.shape[1], fchunk):
            g = _dot(x, wg_ref[:, f0:f0 + fchunk].astype(BF16))
            u = _dot(x, wu_ref[:, f0:f0 + fchunk].astype(BF16))
            o_ref[:, f0:f0 + fchunk] = (_silu(g) * u).astype(o_ref.dtype)

    @pl.when(jnp.logical_not(used))
    def _():
        o_ref[...] = jnp.zeros_like(o_ref)


def _moe_down_kernel(be_ref, nu_ref, h_ref, wd_ref, o_ref, *, slab):
    used = pl.program_id(0) < nu_ref[0]

    @pl.when(used)
    def _():
        _store_slab(o_ref, _pack_bf16_pairs(_dot(h_ref[...], wd_ref[...].astype(BF16))), slab)

    @pl.when(jnp.logical_not(used))
    def _():
        o_ref[...] = jnp.zeros_like(o_ref)


def moe_experts(x_slots, block_e, n_used, w_gate, w_up, w_down, layer):
    d, ff = w_gate.shape[-2:]
    slab = d // 2 // LANES
    n_slots = x_slots.shape[0] // slab
    tb = MOE_ROW_BLOCK
    n_blocks = n_slots // tb
    blk = lambda b, be, nu: (jnp.minimum(b, nu[0] - 1), 0)
    out_blk = lambda b, be, nu: (b, 0)
    wsel = lambda b, be, nu: (layer, be[b], 0, 0)
    hmid = pl.pallas_call(
        functools.partial(_moe_up_kernel, fchunk=_tile(ff, 256, LANES), slab=slab),
        grid_spec=pltpu.PrefetchScalarGridSpec(
            num_scalar_prefetch=2, grid=(n_blocks,),
            in_specs=[pl.BlockSpec((tb * slab, LANES), blk),
                      pl.BlockSpec((None, None, d, ff), wsel),
                      pl.BlockSpec((None, None, d, ff), wsel)],
            out_specs=pl.BlockSpec((tb, ff), out_blk)),
        out_shape=jax.ShapeDtypeStruct((n_slots, ff), BF16),
        compiler_params=_params(("arbitrary",)),
        name="moe_gate_up",
    )(block_e, n_used, x_slots, w_gate, w_up)
    return pl.pallas_call(
        functools.partial(_moe_down_kernel, slab=slab),
        grid_spec=pltpu.PrefetchScalarGridSpec(
            num_scalar_prefetch=2, grid=(n_blocks,),
            in_specs=[pl.BlockSpec((tb, ff), blk),
                      pl.BlockSpec((None, None, ff, d), wsel)],
            out_specs=pl.BlockSpec((tb * slab, LANES), out_blk)),
        out_shape=jax.ShapeDtypeStruct((n_slots * slab, LANES), U32),
        compiler_params=_params(("arbitrary",)),
        name="moe_down",
    )(block_e, n_used, hmid, w_down)


def moe_router_weights(w_rg, b_rg, w_re, b_re, layer):
    w = jnp.concatenate([w_re[layer], w_rg[layer]], axis=-1).T
    b = jnp.concatenate([b_re[layer], b_rg[layer]])[:, None]
    pad = -w.shape[0] % 8
    return jnp.pad(w, ((0, pad), (0, 0))), jnp.pad(b, ((0, pad), (0, 0)))


def hier_moe_residual(x, h, logits_t, bias, g2, row_map, w_gate, w_up, w_down, layer, *, n_grp, tm):
    rows, d = x.shape
    slab = d // 2 // LANES
    nsteps = rows // tm
    n_exp = w_gate.shape[1]
    tb = MOE_ROW_BLOCK
    n_pair = rows * MOE_TOP_K
    n_blocks = -(-(n_pair + n_exp * (tb - 1)) // tb)
    gates_t, dest_t, counts = moe_route(logits_t, bias, n_grp=n_grp, epg=n_exp // n_grp, tb=tb)
    counts = counts[:, 0]
    blocks = (counts + tb - 1) // tb
    blk_end = jnp.cumsum(blocks)
    starts = ((blk_end - blocks) * tb).astype(jnp.int32)
    n_used = blk_end[-1].astype(jnp.int32)
    block_e = jnp.minimum(jnp.searchsorted(blk_end, jnp.arange(n_blocks), side="right"), n_exp - 1)
    block_e = jnp.where(jnp.arange(n_blocks) < n_used, block_e, block_e[n_used - 1]).astype(jnp.int32)
    dest = dest_t.reshape(-1)

    x_slots = pl.pallas_call(
        functools.partial(_dispatch_kernel, rows=rows, tm=tm, tb=tb, n_exp=n_exp, n_blocks=n_blocks, slab=slab),
        grid_spec=pltpu.PrefetchScalarGridSpec(
            num_scalar_prefetch=4, grid=(nsteps,),
            in_specs=[pl.BlockSpec((tm * slab, LANES), lambda i, *_: (i, 0))],
            out_specs=pl.BlockSpec(memory_space=pl.ANY),
            scratch_shapes=[pltpu.VMEM((tb * slab, LANES), U32), pltpu.SemaphoreType.DMA,
                            pltpu.SemaphoreType.DMA]),
        out_shape=jax.ShapeDtypeStruct((n_blocks * tb * slab, LANES), U32),
        compiler_params=_params(("arbitrary",)),
        name="moe_dispatch",
    )(dest, counts, starts, n_used.reshape(1), h)
    y_slots = moe_experts(x_slots, block_e, n_used.reshape(1), w_gate, w_up, w_down, layer)
    return pl.pallas_call(
        functools.partial(_combine_kernel, rows=rows, tm=tm, slab=slab),
        grid_spec=pltpu.PrefetchScalarGridSpec(
            num_scalar_prefetch=1, grid=(nsteps,),
            in_specs=[pl.BlockSpec((tm, d), lambda i, *_: (i, 0)),
                      pl.BlockSpec((None, 1, d), lambda i, *_: (row_map(i * tm), 0, 0)),
                      pl.BlockSpec((tm, MOE_TOP_K), lambda i, *_: (i, 0)),
                      pl.BlockSpec(memory_space=pl.ANY)],
            out_specs=pl.BlockSpec((tm, d), lambda i, *_: (i, 0)),
            scratch_shapes=[pltpu.VMEM((MOE_TOP_K, tm * slab, LANES), U32), pltpu.SemaphoreType.DMA]),
        out_shape=jax.ShapeDtypeStruct((rows, d), F32),
        compiler_params=_params(("arbitrary",)),
        name="moe_combine",
    )(dest, x, g2, gates_t.T, y_slots)


def _rope_lane_tables(length, hd, n_rows):
    grid_rows = length // ROPE_GRID_W
    row = jnp.broadcast_to(jnp.arange(grid_rows, dtype=F32)[:, None], (grid_rows, ROPE_GRID_W)).reshape(length)
    col = jnp.broadcast_to(jnp.arange(ROPE_GRID_W, dtype=F32)[None, :], (grid_rows, ROPE_GRID_W)).reshape(length)
    axis_dim = hd // 2
    inv_freq = ROPE_BASE ** (-jnp.arange(0, axis_dim, 2, dtype=F32) / axis_dim)
    ang_r = row[:, None] * inv_freq
    ang_c = col[:, None] * inv_freq
    cos = jnp.concatenate([jnp.cos(ang_r)] * 2 + [jnp.cos(ang_c)] * 2, axis=1)
    sin = jnp.concatenate([-jnp.sin(ang_r), jnp.sin(ang_r), -jnp.sin(ang_c), jnp.sin(ang_c)], axis=1)
    reps = (n_rows // length, 2 * LANES // (2 * hd))
    return jnp.tile(cos, reps), jnp.tile(sin, reps)


def kernel(x_prompt, x_sample, cache_attn_k, cache_attn_v, state_mlstm_C, state_mlstm_n, state_mlstm_m, state_ssd, c, c_ctx, norm_mix, norm_ffn, w_mod, b_mod, attn_w_in, attn_q_norm, attn_k_norm, attn_lambda, attn_sub_norm, attn_w_out, mlstm_w_in, mlstm_conv_w, mlstm_conv_b, mlstm_w_gate, mlstm_b_gate, mlstm_head_norm, mlstm_w_out, ssd_w_in, ssd_conv_w, ssd_conv_b, ssd_dt_bias, ssd_a_log, ssd_d_skip, ssd_norm, ssd_w_out, hyena_w_in, hyena_conv_w, hyena_conv_b, hyena_f_w1, hyena_f_b1, hyena_f_w2, hyena_f_b2, hyena_f_w3, hyena_f_freq, hyena_skip_bias, hyena_w_out, moe_w_group, moe_b_group, moe_w_expert, moe_b_expert, moe_w_gate, moe_w_up, moe_w_down):
    batch, seq, dm = x_prompt.shape
    dec_batch, dec_seq, _ = x_sample.shape
    depth = norm_mix.shape[0]
    n_ctx = batch * seq
    n_lat = dec_batch * dec_seq
    rows = n_ctx + n_lat
    past = cache_attn_k.shape[2]
    a_heads, a_hd = cache_attn_k.shape[3], cache_attn_k.shape[5]
    m_heads, m_dk, m_dv = state_mlstm_C.shape[3:]
    s_heads, s_p, s_n = state_ssd.shape[3:]
    s_inner = s_heads * s_p
    s_groups = (ssd_conv_w.shape[-1] - s_inner) // (2 * s_n)
    s_hpg = s_heads // s_groups

    def row_map(r):
        return jnp.where(r < n_ctx, 0, 1 + (r - n_ctx) // dec_seq)

    row_unit = math.gcd(n_ctx, dec_seq)
    tm_norm = _tile(row_unit, 256, 8)
    tm_res = _tile(row_unit, 1024, 8)
    seq_lcm = math.lcm(seq, dec_seq)
    pass_unit = math.gcd(n_ctx, n_lat)
    assert pass_unit % seq_lcm == 0 and seq & (seq - 1) == 0 and dec_seq & (dec_seq - 1) == 0
    tm_seq = seq_lcm * max(k for k in range(1, max(2, 2048 // seq_lcm + 1)) if pass_unit % (seq_lcm * k) == 0)
    seq_of_tile = lambda i: jnp.where(i * tm_seq < n_ctx, seq, dec_seq)
    parts = ((0, batch, seq), (n_ctx, dec_batch, dec_seq))
    ones = lambda n: jnp.ones((n,), F32)

    tn_of = lambda n: _tile(n, 512, LANES)

    def conv_extras(taps, bias, scale, col0):
        tn = tn_of(scale.shape[0])
        assert col0 % tn == 0
        return [(taps, (3, tn), lambda i, j, c=col0 // tn: (0, j + c)), _row_vec(bias, tn, col0), _row_vec(scale, tn)]

    n_cond = 1 + dec_batch
    cond = jnp.concatenate([c_ctx[None, :], c], axis=0)
    cond = jnp.pad(jax.nn.silu(cond), ((0, -n_cond % 8), (0, 0))).astype(BF16)

    x = jnp.concatenate([x_prompt.reshape(n_ctx, dm), x_sample.reshape(n_lat, dm)], axis=0)
    outs = {}
    for i in range(depth):
        kind, j = i % 4, i // 4
        mod = matmul(cond, w_mod, w_idx=i, tm=8, tn=1024, tk=2048, extras=[_row_vec(b_mod[i], 1024)],
                     epi=_epi_bias, name="adaln_mod")
        sh1, sc1, g1, sh2, sc2, g2 = [mod[:, None, q * dm:(q + 1) * dm] for q in range(6)]
        h1 = norm_mod(x, norm_mix[i], sc1, sh1, row_map, tm_norm)

        if kind == 0:
            lam_init = 0.8 - 0.6 * math.exp(-0.3 * i)
            tn = tn_of(dm)
            cos, sin = _rope_lane_tables(dec_seq, a_hd, tm_seq)
            tabs = [(cos, (tm_seq, LANES), lambda i_, j_: (0, 0)), (sin, (tm_seq, LANES), lambda i_, j_: (0, 0))]
            rep = dm // a_hd
            q = matmul(h1, attn_w_in, w_idx=j, n=dm, tm=tm_seq, tn=tn, out_dtypes=(BF16,),
                       extras=[_row_vec(jnp.tile(attn_q_norm[j], rep), tn)] + tabs,
                       epi=_make_epi_qk(tm_seq, tn, a_hd, n_ctx, 1), name="attn_in_q")
            k, k_plain = matmul(h1, attn_w_in, w_idx=j, n0=dm, n=dm, tm=tm_seq, tn=tn, out_dtypes=(BF16, F32),
                                extras=[_row_vec(jnp.tile(attn_k_norm[j], rep), tn)] + tabs,
                                epi=_make_epi_qk(tm_seq, tn, a_hd, n_ctx, 2), name="attn_in_k")
            v, v_f32 = matmul(h1, attn_w_in, w_idx=j, n0=2 * dm, n=dm, tm=tm_seq, tn=tn, out_dtypes=(BF16, F32),
                              epi=_epi_two, name="attn_in_v")
            outs["k"] = k_plain[:n_ctx].reshape(batch, 1, seq, a_heads, 2, a_hd)
            outs["v"] = v_f32[:n_ctx].reshape(batch, 1, seq, a_heads, 2 * a_hd)
            lv = attn_lambda[j]
            lam = jnp.exp(jnp.sum(lv[0] * lv[1])) - jnp.exp(jnp.sum(lv[2] * lv[3])) + lam_init
            cache = (cache_attn_k[:, j].reshape(dec_batch * past, dm), cache_attn_v[:, j].reshape(dec_batch * past, dm))
            o_parts = [diff_attention(q, k, v, cache if row0 else None, lam, attn_sub_norm[j], row0=row0, nseq=nseq,
                                      length=length, heads=a_heads, post_scale=1.0 - lam_init)
                       for row0, nseq, length in parts]
            mix_in, w_out = jnp.concatenate(o_parts, axis=0), attn_w_out

        elif kind == 1:
            qk_w = 2 * m_heads * m_dk
            v_w = m_heads * m_dv
            k_scale = jnp.concatenate([ones(qk_w // 2), jnp.full((qk_w // 2,), m_dk ** -0.5, F32)])
            qk = matmul(h1, mlstm_w_in, w_idx=j, n=qk_w, tm=tm_seq, out_dtypes=(BF16,),
                        extras=conv_extras(mlstm_conv_w[j], mlstm_conv_b[j], k_scale, 0),
                        epi=_make_epi_conv(tm_seq, seq_of_tile, True), name="mlstm_in_qk")
            v = matmul(h1, mlstm_w_in, w_idx=j, n0=qk_w, n=v_w, out_dtypes=(BF16,), name="mlstm_in_v")
            o_gate = matmul(h1, mlstm_w_in, w_idx=j, n0=qk_w + v_w, n=v_w, epi=_epi_sigmoid, name="mlstm_in_o")
            g = matmul(h1, mlstm_w_gate, w_idx=j, x3=True, tm=512, extras=[_row_vec(mlstm_b_gate[j], 4 * m_heads)],
                       epi=_epi_bias, name="mlstm_gate")
            g = g.reshape(rows, 2, 2, m_heads)
            gates = jnp.stack([g[:, :, 0], jax.nn.log_sigmoid(g[:, :, 1])], axis=-1)
            gcol = gates.transpose(1, 2, 0, 3)
            init = (state_mlstm_C[:, j], state_mlstm_n[:, j][:, :, :, None, :], state_mlstm_m[:, j][:, :, :, None, None])
            y_parts = []
            for row0, nseq, length in parts:
                t = _tile(length, MLSTM_CHUNK, 8)
                n = nseq * length
                gc = gcol[:, :, row0:row0 + n]
                grow = gc.reshape(2, m_heads, n // t, t, 2).transpose(0, 1, 2, 4, 3)
                res = mlstm_scan(qk, v, gc, grow, init if row0 else None, row0=row0, nseq=nseq, length=length,
                                 heads=m_heads, emit_state=not row0)
                if not row0:
                    outs["C"] = res[1][:, None]
                    outs["n"] = res[2][:, None, :, :, 0]
                    outs["m"] = res[3][:, None, :, :, 0, 0]
                y_parts.append(mlstm_post(res[0], o_gate, mlstm_head_norm[j], row0=row0, heads=m_heads))
            mix_in, w_out = jnp.concatenate(y_parts, axis=0), mlstm_w_out

        elif kind == 2:
            gn = s_groups * s_n
            conv = _make_epi_conv(tm_seq, seq_of_tile, True)
            z = matmul(h1, ssd_w_in, w_idx=j, n=s_inner, name="ssd_in_z")
            xs = matmul(h1, ssd_w_in, w_idx=j, n0=s_inner, n=s_inner, tm=tm_seq,
                        extras=conv_extras(ssd_conv_w[j], ssd_conv_b[j], ones(s_inner), 0),
                        epi=conv, name="ssd_in_x")
            bc = matmul(h1, ssd_w_in, w_idx=j, n0=2 * s_inner, n=2 * gn, tm=tm_seq, out_dtypes=(BF16,),
                        extras=conv_extras(ssd_conv_w[j], ssd_conv_b[j], ones(2 * gn), s_inner),
                        epi=conv, name="ssd_in_bc")
            dt_raw = matmul(h1, ssd_w_in, w_idx=j, n0=2 * s_inner + 2 * gn, n=2 * s_heads, x3=True, tm=512, name="ssd_dt")
            dt = jax.nn.softplus(dt_raw.reshape(rows, 2, s_heads) + ssd_dt_bias[j])
            a = dt * (-jnp.exp(ssd_a_log[j]))
            to_col = lambda y: y.reshape(rows, 2, s_groups, s_hpg).transpose(1, 2, 0, 3)
            acol, dtcol = to_col(a), to_col(dt)
            init = state_ssd[:, j].reshape(dec_batch, 2, s_groups, s_hpg * s_p, s_n)
            d_skip_cols = jnp.repeat(ssd_d_skip[j], s_p)
            y_parts = []
            for row0, nseq, length in parts:
                t = _tile(length, SSD_CHUNK, 8)
                n = nseq * length
                ac = acol[:, :, row0:row0 + n]
                arow = ac.reshape(2, s_groups, n // t, t, s_hpg).transpose(0, 1, 2, 4, 3)
                res = ssd_scan(xs, bc, ac, arow, dtcol[:, :, row0:row0 + n], init if row0 else None, row0=row0,
                               nseq=nseq, length=length, groups=s_groups, hpg=s_hpg, emit_state=not row0)
                if not row0:
                    outs["ssd"] = res[1].reshape(batch, 1, 2, s_heads, s_p, s_n)
                y_parts.append(ssd_post(res[0], xs, z, d_skip_cols, ssd_norm[j], row0=row0))
            mix_in, w_out = jnp.concatenate(y_parts, axis=0), ssd_w_out

        else:
            proj = matmul(h1, hyena_w_in, w_idx=j, tm=tm_seq,
                          extras=conv_extras(hyena_conv_w[j], hyena_conv_b[j], ones(3 * dm), 0),
                          epi=_make_epi_conv(tm_seq, seq_of_tile, False), name="hyena_in")
            y_parts = []
            for row0, nseq, length in parts:
                fmat, gmat, tps, tqs = _hyena_filter_spectra(length, hyena_f_w1[j], hyena_f_b1[j], hyena_f_w2[j],
                                                             hyena_f_b2[j], hyena_f_w3[j], hyena_f_freq[j], dm)
                y = hyena_conv(proj, (row0, 0), fmat, gmat, tps[0], tqs[0], hyena_skip_bias[j][0], proj, (row0, dm),
                               nseq=nseq, length=length, d=dm, out_dtype=F32)
                y = hyena_conv(y, (0, 0), fmat, gmat, tps[1], tqs[1], hyena_skip_bias[j][1], proj, (row0, 2 * dm),
                               nseq=nseq, length=length, d=dm, out_dtype=BF16)
                y_parts.append(y)
            mix_in, w_out = jnp.concatenate(y_parts, axis=0), hyena_w_out

        tn = tn_of(dm)
        x = matmul(mix_in, w_out, w_idx=j, tm=tm_res, tn=tn,
                   extras=[(x, (tm_res, tn), lambda i_, j_: (i_, j_)),
                           (g1, (None, 1, tn), lambda i_, j_: (row_map(i_ * tm_res), 0, j_))],
                   epi=_epi_resid, name="mixer_out")
        w_rt, b_rt = moe_router_weights(moe_w_group, moe_b_group, moe_w_expert, moe_b_expert, i)
        h2, logits_t = norm_mod_router(x, norm_ffn[i], sc2, sh2, w_rt, row_map, tm_norm)
        x = hier_moe_residual(x, h2, logits_t, b_rt, g2, row_map, moe_w_gate, moe_w_up, moe_w_down, i,
                              n_grp=moe_w_group.shape[-1], tm=tm_norm)

    y_prompt = x[:n_ctx].reshape(batch, seq, dm)
    y_sample = x[n_ctx:].reshape(dec_batch, dec_seq, dm)
    return (y_prompt, y_sample, outs["k"], outs["v"], outs["C"], outs["n"], outs["m"], outs["ssd"])
```

```python
import functools
import math

import jax
import jax.numpy as jnp
from jax import lax
from jax.experimental import pallas as pl
from jax.experimental.pallas import tpu as pltpu

F32 = jnp.float32
BF16 = jnp.bfloat16
U32 = jnp.uint32
LANES = 128
NORM_EPS = 1e-6
ROPE_GRID_W = 64
ROPE_BASE = 10000.0
HYENA_EMB = 33
HYENA_DECAY_SHORT = 0.3
HYENA_DECAY_LONG = 1.5
HYENA_TARGET = 1e-2
MOE_TOP_K = 2
MOE_ROW_BLOCK = 512
MLSTM_CHUNK = 256
SSD_CHUNK = 256
VMEM_LIMIT = 56 * 1024 * 1024

_NT = (((1,), (1,)), ((), ()))
_TN = (((0,), (0,)), ((), ()))


def _tile(n, pref, align):
    if n <= pref:
        return n
    t = (pref // align) * align
    while t >= align:
        if n % t == 0:
            return t
        t -= align
    return n


def _params(sem):
    return pltpu.CompilerParams(dimension_semantics=sem, vmem_limit_bytes=VMEM_LIMIT)


def _dot(a, b):
    return jnp.dot(a, b, preferred_element_type=F32)


def _split_bf16(a):
    hi = a.astype(BF16)
    lo = (a - hi.astype(F32)).astype(BF16)
    return hi, lo


def _silu(x):
    return x / (1.0 + jnp.exp(-x))


def _mm_kernel(*refs, nk, x3, n_extra, n_out, epi):
    x_ref, w_ref = refs[:2]
    extra = refs[2:2 + n_extra]
    o_refs = refs[2 + n_extra:2 + n_extra + n_out]

    if x3:
        xh, xl = _split_bf16(x_ref[...].astype(F32))
        wh, wl = _split_bf16(w_ref[...].astype(F32))
        part = _dot(xh, wh) + _dot(xh, wl) + _dot(xl, wh)
    else:
        part = _dot(x_ref[...].astype(BF16), w_ref[...].astype(BF16))

    def finish(r):
        for o_ref, val in zip(o_refs, epi(r, extra, pl.program_id(0))):
            o_ref[...] = val.astype(o_ref.dtype)

    if nk == 1:
        finish(part)
        return
    acc_ref = refs[-1]
    k = pl.program_id(2)

    @pl.when(k == 0)
    def _():
        acc_ref[...] = part

    @pl.when(k > 0)
    def _():
        acc_ref[...] += part

    @pl.when(k == nk - 1)
    def _():
        finish(acc_ref[...])


def _epi_plain(r, extra, i):
    return (r,)


def _epi_bias(r, extra, i):
    return (r + extra[0][...],)


def _epi_resid(r, extra, i):
    return (extra[0][...] + extra[1][...] * r,)


def _epi_sigmoid(r, extra, i):
    return (1.0 / (1.0 + jnp.exp(-r)),)


def _epi_two(r, extra, i):
    return (r, r)


def matmul(x, w, *, w_idx=None, n0=0, n=None, tm=2048, tn=512, tk=2048, out_dtypes=(F32,), x3=False,
           extras=(), epi=_epi_plain, name="matmul"):
    m, kdim = x.shape
    n = w.shape[-1] if n is None else n
    tm = _tile(m, tm, 8)
    tk = _tile(kdim, tk, LANES)
    tn = _tile(n, tn, LANES)
    assert n0 % tn == 0 and m % tm == 0 and kdim % tk == 0 and n % tn == 0
    j0 = n0 // tn
    nk = kdim // tk
    if w.ndim == 3:
        w_spec = pl.BlockSpec((None, tk, tn), lambda i, j, k: (w_idx, k, j + j0))
    else:
        w_spec = pl.BlockSpec((tk, tn), lambda i, j, k: (k, j + j0))
    in_specs = [pl.BlockSpec((tm, tk), lambda i, j, k: (i, k)), w_spec]
    args = [x, w]
    for arr, shape, imap in extras:
        in_specs.append(pl.BlockSpec(shape, lambda i, j, k, imap=imap: imap(i, j)))
        args.append(arr)
    outs = pl.pallas_call(
        functools.partial(_mm_kernel, nk=nk, x3=x3, n_extra=len(extras), n_out=len(out_dtypes), epi=epi),
        grid=(m // tm, n // tn, nk),
        in_specs=in_specs,
        out_specs=[pl.BlockSpec((tm, tn), lambda i, j, k: (i, j)) for _ in out_dtypes],
        out_shape=[jax.ShapeDtypeStruct((m, n), dt) for dt in out_dtypes],
        scratch_shapes=[pltpu.VMEM((tm, tn), F32)] if nk > 1 else [],
        compiler_params=_params(("parallel", "parallel", "arbitrary")),
        name=name,
    )(*args)
    return outs[0] if len(out_dtypes) == 1 else outs


def _row_vec(v, tn, col0=0):
    return (v.reshape(1, -1), (1, tn), lambda i, j, c=col0 // tn: (0, j + c))


def _make_epi_conv(tm, seq_of_tile, act):
    def epi(r, extra, i):
        taps = extra[0][...]
        length = seq_of_tile(i)
        pos = lax.broadcasted_iota(jnp.int32, (tm, 1), 0) & (length - 1)
        prev = jnp.where(pos == 0, 0.0, pltpu.roll(r, 1, 0))
        nxt = jnp.where(pos == length - 1, 0.0, pltpu.roll(r, tm - 1, 0))
        y = prev * taps[0:1] + r * taps[1:2] + nxt * taps[2:3] + extra[1][...]
        if act:
            y = _silu(y)
        return (y * extra[2][...],)
    return epi


def _make_epi_qk(tm, tn, hd, n_ctx, n_out):
    def epi(r, extra, i):
        ga = lax.broadcasted_iota(jnp.int32, (LANES, LANES), 0) // hd
        gb = lax.broadcasted_iota(jnp.int32, (LANES, LANES), 1) // hd
        avg = jnp.where(ga == gb, 1.0 / hd, 0.0).astype(BF16)
        lane = lax.broadcasted_iota(jnp.int32, (1, LANES), 1)
        first_half = (lane & (hd // 2 - 1)) < hd // 4
        rope_on = i * tm >= n_ctx
        cos = extra[1][...]
        sin = extra[2][...]
        rot, plain = [], []
        for c0 in range(0, tn, LANES):
            xc = r[:, c0:c0 + LANES]
            hi, lo = _split_bf16(xc * xc)
            y = xc * lax.rsqrt(_dot(hi, avg) + _dot(lo, avg) + NORM_EPS) * extra[0][:, c0:c0 + LANES]
            partner = jnp.where(first_half, pltpu.roll(y, LANES - hd // 4, 1), pltpu.roll(y, hd // 4, 1))
            rot.append(jnp.where(rope_on, y * cos + partner * sin, y))
            plain.append(y)
        outs = (jnp.concatenate(rot, axis=1),)
        if n_out == 2:
            outs += (jnp.concatenate(plain, axis=1),)
        return outs
    return epi


def _norm_mod_kernel(x_ref, w_ref, sc_ref, sh_ref, o_ref):
    x = x_ref[...]
    y = x * lax.rsqrt(jnp.mean(x * x, axis=-1, keepdims=True) + NORM_EPS) * w_ref[...]
    o_ref[...] = (y * (1.0 + sc_ref[...]) + sh_ref[...]).astype(o_ref.dtype)


def norm_mod(x, w, scale, shift, row_map, tm):
    rows, d = x.shape
    row = pl.BlockSpec((tm, d), lambda i: (i, 0))
    mod = pl.BlockSpec((None, 1, d), lambda i: (row_map(i * tm), 0, 0))
    return pl.pallas_call(
        _norm_mod_kernel,
        grid=(rows // tm,),
        in_specs=[row, pl.BlockSpec((1, d), lambda i: (0, 0)), mod, mod],
        out_specs=row,
        out_shape=jax.ShapeDtypeStruct((rows, d), BF16),
        compiler_params=_params(("parallel",)),
        name="norm_mod",
    )(x, w.reshape(1, d), scale, shift)


def _store_slab(ref, val, slab):
    n = val.shape[0]
    for s in range(slab):
        ref[pl.ds(s, n, stride=slab), :] = val[:, s * LANES:(s + 1) * LANES]


def _load_slab(ref, n, slab):
    return jnp.concatenate([ref[pl.ds(s, n, stride=slab), :] for s in range(slab)], axis=1)


def _pack_bf16_pairs(x):
    c = x.shape[1] // 2
    hi = lax.bitcast_convert_type(x[:, :c].astype(BF16).astype(F32), U32)
    lo = lax.bitcast_convert_type(x[:, c:].astype(BF16).astype(F32), U32)
    return hi | (lo >> 16)


def _unpack_hi(p):
    return lax.bitcast_convert_type(p & jnp.uint32(0xFFFF0000), F32)


def _unpack_lo(p):
    return lax.bitcast_convert_type(p << 16, F32)


def _slab_at(ref, row, n, slab):
    return ref.at[pl.ds(pl.multiple_of(row * slab, slab), n * slab)]


def _norm_router_kernel(x_ref, w_ref, sc_ref, sh_ref, wr_ref, h_ref, lt_ref, *, slab):
    x = x_ref[...]
    y = x * lax.rsqrt(jnp.mean(x * x, axis=-1, keepdims=True) + NORM_EPS) * w_ref[...]
    h = y * (1.0 + sc_ref[...]) + sh_ref[...]
    _store_slab(h_ref, _pack_bf16_pairs(h), slab)
    hh, hl = _split_bf16(h)
    wh, wl = _split_bf16(wr_ref[...])
    nt = lambda a, b: lax.dot_general(a, b, _NT, preferred_element_type=F32)
    lt_ref[...] = nt(wh, hh) + nt(wh, hl) + nt(wl, hh)


def norm_mod_router(x, w, scale, shift, w_router_t, row_map, tm):
    rows, d = x.shape
    nr = w_router_t.shape[0]
    slab = d // 2 // LANES
    row = pl.BlockSpec((tm, d), lambda i: (i, 0))
    mod = pl.BlockSpec((None, 1, d), lambda i: (row_map(i * tm), 0, 0))
    return pl.pallas_call(
        functools.partial(_norm_router_kernel, slab=slab),
        grid=(rows // tm,),
        in_specs=[row, pl.BlockSpec((1, d), lambda i: (0, 0)), mod, mod, pl.BlockSpec((nr, d), lambda i: (0, 0))],
        out_specs=[pl.BlockSpec((tm * slab, LANES), lambda i: (i, 0)), pl.BlockSpec((nr, tm), lambda i: (0, i))],
        out_shape=[jax.ShapeDtypeStruct((rows * slab, LANES), U32), jax.ShapeDtypeStruct((nr, rows), F32)],
        compiler_params=_params(("parallel",)),
        name="norm_mod_router",
    )(x, w.reshape(1, d), scale, shift, w_router_t)


def _attn_kernel(lam_ref, q_ref, k_ref, v_ref, *rest, hd2, scale, post_scale, has_cache):
    if has_cache:
        kc_ref, vc_ref, sn_ref, o_ref = rest
    else:
        sn_ref, o_ref = rest
    nt = lambda a, b: lax.dot_general(a, b, _NT, preferred_element_type=F32)
    lane = lax.broadcasted_iota(jnp.int32, (q_ref.shape[0], hd2), 1)
    for c0 in range(0, q_ref.shape[1], hd2):
        cols = slice(c0, c0 + hd2)
        q = q_ref[:, cols].astype(F32) * scale
        k = k_ref[:, cols]
        if has_cache:
            kc = kc_ref[:, cols].astype(BF16)

        def exps(qm):
            s = nt(qm, k)
            m = jnp.max(s, axis=-1, keepdims=True)
            if not has_cache:
                e = jnp.exp(s - m)
                return e, None, 1.0 / jnp.sum(e, axis=-1, keepdims=True)
            sc = nt(qm, kc)
            m = jnp.maximum(m, jnp.max(sc, axis=-1, keepdims=True))
            e = jnp.exp(s - m)
            ec = jnp.exp(sc - m)
            return e, ec, 1.0 / (jnp.sum(e, axis=-1, keepdims=True) + jnp.sum(ec, axis=-1, keepdims=True))

        e1, e1c, r1 = exps(jnp.where(lane < hd2 // 2, q, 0.0).astype(BF16))
        e2, e2c, r2 = exps(jnp.where(lane >= hd2 // 2, q, 0.0).astype(BF16))
        r2 = lam_ref[0] * r2
        o = _dot((e1 * r1 - e2 * r2).astype(BF16), v_ref[:, cols])
        if has_cache:
            o = o + _dot((e1c * r1 - e2c * r2).astype(BF16), vc_ref[:, cols].astype(BF16))
        o = o * lax.rsqrt(jnp.mean(o * o, axis=-1, keepdims=True) + NORM_EPS) * sn_ref[...] * post_scale
        o_ref[:, cols] = o.astype(o_ref.dtype)


def diff_attention(q, k, v, cache, lam, sub_norm, *, row0, nseq, length, heads, post_scale):
    hd2 = q.shape[1] // heads
    hp = 2 if heads % 2 == 0 else 1
    wid = hp * hd2
    tq = _tile(length, 256, 8)
    nq = length // tq
    in_specs = [
        pl.BlockSpec(memory_space=pltpu.SMEM),
        pl.BlockSpec((tq, wid), lambda s, h, i: (row0 // tq + s * nq + i, h)),
        pl.BlockSpec((length, wid), lambda s, h, i: (row0 // length + s, h)),
        pl.BlockSpec((length, wid), lambda s, h, i: (row0 // length + s, h)),
    ]
    args = [lam.reshape(1), q, k, v]
    if cache is not None:
        past = cache[0].shape[0] // nseq
        in_specs += [pl.BlockSpec((past, wid), lambda s, h, i: (s, h))] * 2
        args += list(cache)
    in_specs.append(pl.BlockSpec((1, hd2), lambda s, h, i: (0, 0)))
    args.append(sub_norm.reshape(1, hd2))
    return pl.pallas_call(
        functools.partial(_attn_kernel, hd2=hd2, scale=(hd2 // 2) ** -0.5, post_scale=post_scale,
                          has_cache=cache is not None),
        grid=(nseq, heads // hp, nq),
        in_specs=in_specs,
        out_specs=pl.BlockSpec((tq, wid), lambda s, h, i: (s * nq + i, h)),
        out_shape=jax.ShapeDtypeStruct((nseq * length, q.shape[1]), BF16),
        compiler_params=_params(("parallel", "parallel", "parallel")),
        name="diff_attention",
    )(*args)


def _causal_masks(t, d):
    sign = jnp.where(d == 0, 1, -1)
    r = lax.broadcasted_iota(jnp.int32, (t, t), 0)
    s = lax.broadcasted_iota(jnp.int32, (t, t), 1)
    diff = (s - r) * sign
    return diff <= 0, diff >= 0


def _scan_masks(t):
    r = jnp.arange(t)[:, None]
    s = jnp.arange(t)[None, :]
    vis = jnp.stack([s <= r, s >= r])
    return vis.astype(BF16), vis.astype(F32), jnp.where(vis, 0.0, -jnp.inf).astype(F32)


def _mask_specs(t, dir_of):
    return [pl.BlockSpec((None, t, t), lambda *g: (dir_of(*g), 0, 0)),
            pl.BlockSpec((None, t, t), lambda *g: (1 - dir_of(*g), 0, 0)),
            pl.BlockSpec((None, t, t), lambda *g: (dir_of(*g), 0, 0))]


def _split3(a):
    h = a.astype(BF16)
    r1 = a - h.astype(F32)
    m = r1.astype(BF16)
    return h, m, (r1 - m.astype(F32)).astype(BF16)


def _masked_cumsum(vis, a):
    h, m, l = _split3(a)
    return _dot(vis, h) + _dot(vis, m) + _dot(vis, l)


def _chunk_row(s, d, c, nc):
    return s * nc + jnp.where(d == 0, c, nc - 1 - c)


def _mlstm_kernel(*refs, t, nc, has_init, emit_state):
    refs = list(refs)
    q_ref, k_ref, v_ref, gcol_ref, grow_ref = refs[:5]
    pos = 5
    if has_init:
        c0_ref, n0_ref, m0_ref = refs[pos:pos + 3]
        pos += 3
    h_ref = refs[pos]
    pos += 1
    if emit_state:
        co_ref, no_ref, mo_ref = refs[pos:pos + 3]
        pos += 3
    c_s, n_s, m_s = refs[pos:pos + 3]
    d = pl.program_id(1)
    c = pl.program_id(3)

    @pl.when(c == 0)
    def _():
        if has_init:
            c_s[...] = c0_ref[...]
            n_s[...] = n0_ref[...]
            m_s[...] = m0_ref[...]
        else:
            c_s[...] = jnp.zeros_like(c_s)
            n_s[...] = jnp.zeros_like(n_s)
            m_s[...] = jnp.zeros_like(m_s)

    q = q_ref[...]
    k = k_ref[...]
    v = v_ref[...]
    i_col = gcol_ref[:, 0:1]
    f_col = gcol_ref[:, 1:2]
    i_row = grow_ref[0:1, :]
    f_row = grow_ref[1:2, :]
    causal, causal_t = _causal_masks(t, d)
    b_col = jnp.sum(jnp.where(causal, f_row, 0.0), axis=1, keepdims=True)
    b_row = jnp.sum(jnp.where(causal_t, f_col, 0.0), axis=0, keepdims=True)
    dlog = jnp.where(causal, b_col - b_row + i_row, -jnp.inf)
    m_prev = m_s[...]
    inter = b_col + m_prev
    mt = jnp.maximum(inter, jnp.max(dlog, axis=1, keepdims=True))
    w_intra = jnp.exp(dlog - mt)
    w_inter = jnp.exp(inter - mt)
    cm = c_s[...]
    nv = n_s[...]
    sm = lax.dot_general(q, k, _NT, preferred_element_type=F32) * w_intra
    num = _dot(sm.astype(BF16), v) + w_inter * _dot(q, cm.astype(BF16))
    den = jnp.sum(sm, axis=1, keepdims=True) + w_inter * jnp.sum(q.astype(F32) * nv, axis=1, keepdims=True)
    h_ref[...] = num / jnp.maximum(jnp.abs(den), jnp.exp(-mt))

    rowid = lax.broadcasted_iota(jnp.int32, (t, 1), 0)
    end_row = jnp.where(d == 0, t - 1, 0)
    m_new = jnp.sum(jnp.where(rowid == end_row, mt, 0.0), axis=0, keepdims=True)
    b_last = jnp.sum(f_row, axis=1, keepdims=True)
    kw = k.astype(F32) * jnp.exp(b_last - b_col + i_col - m_new)
    decay = jnp.exp(b_last + m_prev - m_new)
    c_new = decay * cm + lax.dot_general(kw.astype(BF16), v, _TN, preferred_element_type=F32)
    n_new = decay * nv + jnp.sum(kw, axis=0, keepdims=True)
    c_s[...] = c_new
    n_s[...] = n_new
    m_s[...] = m_new
    if emit_state:
        @pl.when(c == nc - 1)
        def _():
            co_ref[...] = c_new
            no_ref[...] = n_new
            mo_ref[...] = m_new


def mlstm_scan(qk, v, gcol, grow, init, *, row0, nseq, length, heads, emit_state):
    n = nseq * length
    dk = qk.shape[1] // (2 * heads)
    dv = v.shape[1] // heads
    t = _tile(length, MLSTM_CHUNK, 8)
    nc = length // t
    rc = lambda s, d, h, c: _chunk_row(s, d, c, nc)
    r0 = row0 // t
    in_specs = [
        pl.BlockSpec((t, dk), lambda s, d, h, c: (r0 + rc(s, d, h, c), h)),
        pl.BlockSpec((t, dk), lambda s, d, h, c: (r0 + rc(s, d, h, c), heads + h)),
        pl.BlockSpec((t, dv), lambda s, d, h, c: (r0 + rc(s, d, h, c), h)),
        pl.BlockSpec((None, None, t, 2), lambda s, d, h, c: (d, h, rc(s, d, h, c), 0)),
        pl.BlockSpec((None, None, None, 2, t), lambda s, d, h, c: (d, h, rc(s, d, h, c), 0, 0)),
    ]
    args = [qk, qk, v, gcol, grow]
    st_specs = [
        pl.BlockSpec((None, None, None, dk, dv), lambda s, d, h, c: (s, d, h, 0, 0)),
        pl.BlockSpec((None, None, None, 1, dk), lambda s, d, h, c: (s, d, h, 0, 0)),
        pl.BlockSpec((None, None, None, 1, 1), lambda s, d, h, c: (s, d, h, 0, 0)),
    ]
    if init is not None:
        in_specs += st_specs
        args += list(init)
    out_specs = [pl.BlockSpec((None, t, dv), lambda s, d, h, c: (d, rc(s, d, h, c), h))]
    out_shape = [jax.ShapeDtypeStruct((2, n, heads * dv), F32)]
    if emit_state:
        out_specs += st_specs
        out_shape += [jax.ShapeDtypeStruct((nseq, 2, heads, dk, dv), F32),
                      jax.ShapeDtypeStruct((nseq, 2, heads, 1, dk), F32),
                      jax.ShapeDtypeStruct((nseq, 2, heads, 1, 1), F32)]
    return pl.pallas_call(
        functools.partial(_mlstm_kernel, t=t, nc=nc, has_init=init is not None, emit_state=emit_state),
        grid=(nseq, 2, heads, nc),
        in_specs=in_specs,
        out_specs=out_specs,
        out_shape=out_shape,
        scratch_shapes=[pltpu.VMEM((dk, dv), F32), pltpu.VMEM((1, dk), F32), pltpu.VMEM((1, 1), F32)],
        compiler_params=_params(("parallel", "parallel", "parallel", "arbitrary")),
        name="mlstm_scan",
    )(*args)


def _mlstm_post_kernel(h_ref, g_ref, w_ref, o_ref):
    hs = h_ref[0] + h_ref[1]
    y = hs * lax.rsqrt(jnp.mean(hs * hs, axis=-1, keepdims=True) + NORM_EPS) * w_ref[...]
    o_ref[...] = (y * g_ref[...]).astype(o_ref.dtype)


def mlstm_post(h, o_gate, head_norm, *, row0, heads):
    _, n, width = h.shape
    dv = width // heads
    tm = _tile(n, 512, 8)
    return pl.pallas_call(
        _mlstm_post_kernel,
        grid=(n // tm, heads),
        in_specs=[pl.BlockSpec((2, tm, dv), lambda i, hh: (0, i, hh)),
                  pl.BlockSpec((tm, dv), lambda i, hh: (row0 // tm + i, hh)),
                  pl.BlockSpec((1, dv), lambda i, hh: (0, 0))],
        out_specs=pl.BlockSpec((tm, dv), lambda i, hh: (i, hh)),
        out_shape=jax.ShapeDtypeStruct((n, width), BF16),
        compiler_params=_params(("parallel", "parallel")),
        name="mlstm_post",
    )(h, o_gate, head_norm.reshape(1, dv))


def _ssd_kernel(*refs, t, nc, hpg, p, has_init, emit_state):
    refs = list(refs)
    x_ref, b_ref, c_ref, acol_ref, dtcol_ref, aneg_ref, vis_ref, vist_ref, neg_ref = refs[:9]
    pos = 9
    if has_init:
        s0_ref = refs[pos]
        pos += 1
    y_ref = refs[pos]
    pos += 1
    if emit_state:
        so_ref = refs[pos]
        pos += 1
    st = refs[pos]
    c = pl.program_id(3)
    width = hpg * p

    @pl.when(c == 0)
    def _():
        if has_init:
            st[...] = s0_ref[...].T
        else:
            st[...] = jnp.zeros_like(st)

    lane_head = lax.broadcasted_iota(jnp.int32, (1, width), 1) // p
    lane_pair = lax.broadcasted_iota(jnp.int32, (1, 2 * p), 1)
    bm = b_ref[...]
    cmat = c_ref[...]
    dt_e = jnp.zeros((t, width), F32)
    for e in range(hpg):
        dt_e = jnp.where(lane_head == e, dtcol_ref[:, e:e + 1], dt_e)
    xd = x_ref[...] * dt_e
    xdb = xd.astype(BF16)
    a_e = dt_e * aneg_ref[...]
    cum_e = _masked_cumsum(vis_ref[...], a_e)
    tot_e = jnp.sum(a_e, axis=0, keepdims=True)
    cb = lax.dot_general(cmat, bm, _NT, preferred_element_type=F32)
    vis_t = vist_ref[...]
    neg = neg_ref[...]
    y_pairs = []
    for e in range(hpg):
        cum_col = cum_e[:, e * p:e * p + 1]
        cum_row = jnp.sum(vis_t * acol_ref[:, e:e + 1], axis=0, keepdims=True)
        wmat = (cb * jnp.exp(cum_col - cum_row + neg)).astype(BF16)
        j = e // 2
        yp = _dot(wmat, xdb[:, j * 2 * p:(j + 1) * 2 * p])
        if e % 2 == 0:
            y_even = yp
        else:
            y_pairs.append(jnp.where(lane_pair < p, y_even, yp))
    s_prev = st[...]
    y_off = _dot(cmat, s_prev.astype(BF16)) * jnp.exp(cum_e)
    y_ref[...] = jnp.concatenate(y_pairs, axis=1) + y_off
    xw = (xd * jnp.exp(tot_e - cum_e)).astype(BF16)
    s_new = jnp.exp(tot_e) * s_prev + lax.dot_general(bm, xw, _TN, preferred_element_type=F32)
    st[...] = s_new
    if emit_state:
        @pl.when(c == nc - 1)
        def _():
            so_ref[...] = s_new.T


def ssd_scan(x, bc, acol, dtcol, aneg, init, *, row0, nseq, length, groups, hpg, emit_state):
    n = nseq * length
    inner = x.shape[1]
    width = inner // groups
    p = width // hpg
    ns = bc.shape[1] // (2 * groups)
    t = _tile(length, SSD_CHUNK, 8)
    nc = length // t
    rc = lambda s, d, g, c: _chunk_row(s, d, c, nc)
    r0 = row0 // t
    in_specs = [
        pl.BlockSpec((t, width), lambda s, d, g, c: (r0 + rc(s, d, g, c), g)),
        pl.BlockSpec((t, ns), lambda s, d, g, c: (r0 + rc(s, d, g, c), g)),
        pl.BlockSpec((t, ns), lambda s, d, g, c: (r0 + rc(s, d, g, c), groups + g)),
        pl.BlockSpec((None, None, t, hpg), lambda s, d, g, c: (d, g, rc(s, d, g, c), 0)),
        pl.BlockSpec((None, None, t, hpg), lambda s, d, g, c: (d, g, rc(s, d, g, c), 0)),
        pl.BlockSpec((None, 1, width), lambda s, d, g, c: (d, 0, g)),
    ] + _mask_specs(t, lambda s, d, g, c: d)
    args = [x, bc, bc, acol, dtcol, aneg, *_scan_masks(t)]
    st_spec = pl.BlockSpec((None, None, None, width, ns), lambda s, d, g, c: (s, d, g, 0, 0))
    if init is not None:
        in_specs.append(st_spec)
        args.append(init)
    out_specs = [pl.BlockSpec((None, t, width), lambda s, d, g, c: (d, rc(s, d, g, c), g))]
    out_shape = [jax.ShapeDtypeStruct((2, n, inner), F32)]
    if emit_state:
        out_specs.append(st_spec)
        out_shape.append(jax.ShapeDtypeStruct((nseq, 2, groups, width, ns), F32))
    return pl.pallas_call(
        functools.partial(_ssd_kernel, t=t, nc=nc, hpg=hpg, p=p, has_init=init is not None, emit_state=emit_state),
        grid=(nseq, 2, groups, nc),
        in_specs=in_specs,
        out_specs=out_specs,
        out_shape=out_shape,
        scratch_shapes=[pltpu.VMEM((ns, width), F32)],
        compiler_params=_params(("parallel", "parallel", "parallel", "arbitrary")),
        name="ssd_scan",
    )(*args)


def _ssd_post_kernel(y_ref, x_ref, z_ref, dsk_ref, w_ref, o_ref):
    y = (y_ref[0] + y_ref[1] + x_ref[...] * dsk_ref[...]) * _silu(z_ref[...])
    o_ref[...] = (y * lax.rsqrt(jnp.mean(y * y, axis=-1, keepdims=True) + NORM_EPS) * w_ref[...]).astype(o_ref.dtype)


def ssd_post(y, x, z, d_skip_cols, norm_w, *, row0):
    _, n, inner = y.shape
    tm = _tile(n, 128, 8)
    row = pl.BlockSpec((tm, inner), lambda i: (row0 // tm + i, 0))
    vec = pl.BlockSpec((1, inner), lambda i: (0, 0))
    return pl.pallas_call(
        _ssd_post_kernel,
        grid=(n // tm,),
        in_specs=[pl.BlockSpec((2, tm, inner), lambda i: (0, i, 0)), row, row, vec, vec],
        out_specs=pl.BlockSpec((tm, inner), lambda i: (i, 0)),
        out_shape=jax.ShapeDtypeStruct((n, inner), BF16),
        compiler_params=_params(("parallel",)),
        name="ssd_post",
    )(y, x, z, d_skip_cols.reshape(1, inner), norm_w.reshape(1, inner))


def _hyena_kernel(u_ref, f_ref, g_ref, tp_ref, tq_ref, bias_ref, xm_ref, o_ref, *, length):
    u = u_ref[...]
    a = _dot(f_ref[...], u.astype(BF16))
    a_sw = jnp.concatenate([a[length:], a[:length]], axis=0)
    y = a * tp_ref[...] + a_sw * tq_ref[...]
    conv = _dot(g_ref[...], y.astype(BF16))
    o_ref[...] = ((conv + u * bias_ref[...]) * xm_ref[...]).astype(o_ref.dtype)


def hyena_conv(u, u_at, fmat, gmat, tp, tq, bias, xm, xm_at, *, nseq, length, d, out_dtype):
    tc = _tile(d, 256, LANES)
    n2 = 2 * length
    (u_r0, u_c0), (x_r0, x_c0) = u_at, xm_at
    return pl.pallas_call(
        functools.partial(_hyena_kernel, length=length),
        grid=(d // tc, nseq),
        in_specs=[
            pl.BlockSpec((length, tc), lambda j, s: (u_r0 // length + s, u_c0 // tc + j)),
            pl.BlockSpec((n2, length), lambda j, s: (0, 0)),
            pl.BlockSpec((length, n2), lambda j, s: (0, 0)),
            pl.BlockSpec((n2, tc), lambda j, s: (0, j)),
            pl.BlockSpec((n2, tc), lambda j, s: (0, j)),
            pl.BlockSpec((1, tc), lambda j, s: (0, j)),
            pl.BlockSpec((length, tc), lambda j, s: (x_r0 // length + s, x_c0 // tc + j)),
        ],
        out_specs=pl.BlockSpec((length, tc), lambda j, s: (s, j)),
        out_shape=jax.ShapeDtypeStruct((nseq * length, d), out_dtype),
        compiler_params=_params(("parallel", "parallel")),
        name="hyena_conv",
    )(u, fmat, gmat, tp, tq, bias.reshape(1, d), xm)


def _dft_matrices(length):
    n2 = 2 * length
    r = jnp.arange(n2, dtype=jnp.int32)
    kfreq = jnp.where(r <= length, r, r - length)
    ang = ((kfreq[:, None] * r[None, :]) % n2).astype(F32) * (2.0 * math.pi / n2)
    is_sin = (r > length)[:, None]
    fwd = jnp.where(is_sin, -jnp.sin(ang), jnp.cos(ang))[:, :length]
    bwd = jnp.where(is_sin, -fwd, fwd) * (jnp.arange(length) > 0)
    wk = jnp.where((r == 0) | (r == length), 1.0, 2.0) / n2
    inv = (fwd * wk[:, None]).T
    return jnp.concatenate([fwd, bwd], axis=1), fwd.astype(BF16), inv.astype(BF16)


def _hyena_filter_spectra(length, f_w1, f_b1, f_w2, f_b2, f_w3, f_freq, d):
    hp = lax.Precision.HIGHEST
    t = jnp.linspace(0.0, 1.0, length, dtype=F32)[:, None]
    bands = (HYENA_EMB - 1) // 2
    f = jnp.linspace(1e-4, bands - 1, bands, dtype=F32)
    w = 2.0 * math.pi * jnp.arange(length, dtype=F32)[:, None] / length
    z = jnp.concatenate([t, jnp.cos(f * w), -jnp.sin(f * w)], axis=-1)
    hid = jnp.sin(f_freq * (jnp.dot(z, f_w1, precision=hp) + f_b1))
    hid = jnp.sin(f_freq * (jnp.dot(hid, f_w2, precision=hp) + f_b2))
    filt = matmul(hid, f_w3, x3=True, tk=hid.shape[1], name="hyena_filter").reshape(length, 2, 2, d)
    min_decay = math.log(HYENA_TARGET) / HYENA_DECAY_LONG
    max_decay = math.log(HYENA_TARGET) / HYENA_DECAY_SHORT
    deltas = jnp.linspace(min_decay, max_decay, d, dtype=F32)
    filt = filt * jnp.exp(-t * jnp.abs(deltas))[:, None, None, :]
    both, fmat, gmat = _dft_matrices(length)
    tps, tqs = [], []
    for o in range(2):
        h_fwd, h_bwd = filt[:, o, 0], filt[:, o, 1]
        norm = jnp.sum(jnp.abs(h_fwd), axis=0) + jnp.sum(jnp.abs(h_bwd[1:]), axis=0)
        spec = matmul(both, jnp.concatenate([h_fwd, h_bwd], axis=0), x3=True, tm=512,
                      name="hyena_filter_dft") / norm
        top, bot = spec[:length], spec[length:]
        zero = jnp.zeros((1, d), F32)
        im = jnp.concatenate([zero, bot[1:]], axis=0)
        tps.append(jnp.concatenate([top, bot[0:1], top[1:]], axis=0))
        tqs.append(jnp.concatenate([-im, im], axis=0))
    return fmat, gmat, tps, tqs


def _router_kernel(l_ref, b_ref, gates_ref, dest_ref, cnt_ref, pre_s, *, n_grp, epg, tb, chunk):
    n_exp = n_grp * epg
    rows = l_ref.shape[1]
    lg = l_ref[...] + b_ref[...]
    g = lg[n_exp:n_exp + n_grp]
    ge = jnp.exp(g - jnp.max(g, axis=0, keepdims=True))
    gp = ge / jnp.sum(ge, axis=0, keepdims=True)
    gpm = jnp.max(gp, axis=0, keepdims=True)
    gi = lax.broadcasted_iota(jnp.int32, gp.shape, 0)
    gsel = jnp.min(jnp.where(gp == gpm, gi, n_grp), axis=0, keepdims=True)
    e_in = lg[0:epg]
    for q in range(1, n_grp):
        e_in = jnp.where(gsel == q, lg[q * epg:(q + 1) * epg], e_in)
    ee = jnp.exp(e_in - jnp.max(e_in, axis=0, keepdims=True))
    ep = ee / jnp.sum(ee, axis=0, keepdims=True)
    ei = lax.broadcasted_iota(jnp.int32, ep.shape, 0)
    p1 = jnp.max(ep, axis=0, keepdims=True)
    s1 = jnp.min(jnp.where(ep == p1, ei, epg), axis=0, keepdims=True)
    ep2 = jnp.where(ei == s1, -1.0, ep)
    p2 = jnp.max(ep2, axis=0, keepdims=True)
    s2 = jnp.min(jnp.where(ep2 == p2, ei, epg), axis=0, keepdims=True)
    den = p1 + p2
    gates_ref[0:1, :] = gpm * p1 / den
    gates_ref[1:2, :] = gpm * p2 / den

    eio = lax.broadcasted_iota(jnp.int32, (n_exp, rows), 0)
    oh0 = eio == gsel * epg + s1
    oh1 = eio == gsel * epg + s2
    cnt = oh0.astype(F32) + oh1.astype(F32)
    ca = lax.broadcasted_iota(jnp.int32, (chunk, chunk), 0)
    cb = lax.broadcasted_iota(jnp.int32, (chunk, chunk), 1)
    tri = (ca < cb).astype(BF16)
    carry = jnp.zeros((n_exp, 1), F32)
    for c0 in range(0, rows, chunk):
        c = cnt[:, c0:c0 + chunk]
        pre_s[:, c0:c0 + chunk] = _dot(c.astype(BF16), tri) + carry
        carry = carry + jnp.sum(c, axis=1, keepdims=True)
    nblk = jnp.floor((carry + (tb - 1)) * (1.0 / tb))
    sa = lax.broadcasted_iota(jnp.int32, (n_exp, n_exp), 0)
    sb = lax.broadcasted_iota(jnp.int32, (n_exp, n_exp), 1)
    nblk_row = jnp.sum(jnp.where(sa == sb, nblk, 0.0), axis=0, keepdims=True)
    blk_start = jnp.sum(jnp.where(sb < sa, nblk_row, 0.0), axis=1, keepdims=True)
    slot = blk_start * tb + pre_s[...]
    dest_ref[0:1, :] = jnp.sum(jnp.where(oh0, slot, 0.0), axis=0, keepdims=True).astype(jnp.int32)
    dest_ref[1:2, :] = jnp.sum(jnp.where(oh1, slot, 0.0), axis=0, keepdims=True).astype(jnp.int32)
    cnt_ref[...] = carry.astype(jnp.int32)


def moe_route(logits_t, bias, *, n_grp, epg, tb):
    nr, rows = logits_t.shape
    n_exp = n_grp * epg
    assert tb & (tb - 1) == 0
    return pl.pallas_call(
        functools.partial(_router_kernel, n_grp=n_grp, epg=epg, tb=tb, chunk=_tile(rows, 512, LANES)),
        out_shape=[jax.ShapeDtypeStruct((MOE_TOP_K, rows), F32),
                   jax.ShapeDtypeStruct((MOE_TOP_K, rows), jnp.int32),
                   jax.ShapeDtypeStruct((n_exp, 1), jnp.int32)],
        scratch_shapes=[pltpu.VMEM((n_exp, rows), F32)],
        compiler_params=pltpu.CompilerParams(vmem_limit_bytes=VMEM_LIMIT),
        name="moe_route",
    )(logits_t, bias)


def _dispatch_kernel(dest_ref, cnt_ref, start_ref, nu_ref, h_ref, xs_ref, zero_s, sem, zsem, *,
                     rows, tm, tb, n_exp, n_blocks, slab):
    i = pl.program_id(0)

    def zero_copy(row):
        return pltpu.make_async_copy(zero_s, _slab_at(xs_ref, row, tb, slab), zsem)

    @pl.when(i == 0)
    def _():
        zero_s[...] = jnp.zeros_like(zero_s)
        for wait in (False, True):
            def go(row):
                if wait:
                    zero_copy(row).wait()
                else:
                    zero_copy(row).start()

            def last_block(e, carry):
                @pl.when(cnt_ref[e] > 0)
                def _():
                    go(start_ref[e] + ((cnt_ref[e] - 1) & ~(tb - 1)))
                return carry

            def tail_block(b, carry):
                go(b * tb)
                return carry

            lax.fori_loop(0, n_exp, last_block, 0)
            lax.fori_loop(nu_ref[0], n_blocks, tail_block, 0)

    def row_copy(r, slot):
        return pltpu.make_async_copy(_slab_at(h_ref, r, 1, slab), _slab_at(xs_ref, slot, 1, slab), sem)

    def issue(r, carry):
        row_copy(r, dest_ref[i * tm + r]).start()
        row_copy(r, dest_ref[rows + i * tm + r]).start()
        return carry

    def drain(r, carry):
        row_copy(0, 0).wait()
        row_copy(0, 0).wait()
        return carry

    lax.fori_loop(0, tm, issue, 0, unroll=8)
    lax.fori_loop(0, tm, drain, 0, unroll=8)


def _combine_kernel(dest_ref, x_ref, g2_ref, gt_ref, ys_ref, o_ref, buf, sem, *, rows, tm, slab):
    base = pl.program_id(0) * tm

    def row_copy(k, r, slot):
        return pltpu.make_async_copy(_slab_at(ys_ref, slot, 1, slab), _slab_at(buf.at[k], r, 1, slab), sem)

    def issue(r, carry):
        row_copy(0, r, dest_ref[base + r]).start()
        row_copy(1, r, dest_ref[rows + base + r]).start()
        return carry

    def drain(r, carry):
        row_copy(0, 0, 0).wait()
        row_copy(0, 0, 0).wait()
        return carry

    lax.fori_loop(0, tm, issue, 0, unroll=8)
    lax.fori_loop(0, tm, drain, 0, unroll=8)
    g0 = gt_ref[:, 0:1]
    g1 = gt_ref[:, 1:2]
    half = slab * LANES
    for s in range(slab):
        p0 = buf[0, pl.ds(s, tm, stride=slab), :]
        p1 = buf[1, pl.ds(s, tm, stride=slab), :]
        for unpack, c0 in ((_unpack_hi, s * LANES), (_unpack_lo, half + s * LANES)):
            cols = slice(c0, c0 + LANES)
            o_ref[:, cols] = x_ref[:, cols] + g2_ref[:, cols] * (g0 * unpack(p0) + g1 * unpack(p1))


def _moe_up_kernel(be_ref, nu_ref, x_ref, wg_ref, wu_ref, o_ref, *, fchunk, slab):
    used = pl.program_id(0) < nu_ref[0]

    @pl.when(used)
    def _():
        p = _load_slab(x_ref, o_ref.shape[0], slab)
        x = jnp.concatenate([_unpack_hi(p), _unpack_lo(p)], axis=1).astype(BF16)
        for f0 in range(0, o_ref.shape[1], fchunk):
            g = _dot(x, wg_ref[:, f0:f0 + fchunk].astype(BF16))
            u = _dot(x, wu_ref[:, f0:f0 + fchunk].astype(BF16))
            o_ref[:, f0:f0 + fchunk] = (_silu(g) * u).astype(o_ref.dtype)

    @pl.when(jnp.logical_not(used))
    def _():
        o_ref[...] = jnp.zeros_like(o_ref)


def _moe_down_kernel(be_ref, nu_ref, h_ref, wd_ref, o_ref, *, slab):
    used = pl.program_id(0) < nu_ref[0]

    @pl.when(used)
    def _():
        _store_slab(o_ref, _pack_bf16_pairs(_dot(h_ref[...], wd_ref[...].astype(BF16))), slab)

    @pl.when(jnp.logical_not(used))
    def _():
        o_ref[...] = jnp.zeros_like(o_ref)


def moe_experts(x_slots, block_e, n_used, w_gate, w_up, w_down, layer):
    d, ff = w_gate.shape[-2:]
    slab = d // 2 // LANES
    n_slots = x_slots.shape[0] // slab
    tb = MOE_ROW_BLOCK
    n_blocks = n_slots // tb
    blk = lambda b, be, nu: (jnp.minimum(b, nu[0] - 1), 0)
    out_blk = lambda b, be, nu: (b, 0)
    wsel = lambda b, be, nu: (layer, be[b], 0, 0)
    hmid = pl.pallas_call(
        functools.partial(_moe_up_kernel, fchunk=_tile(ff, 256, LANES), slab=slab),
        grid_spec=pltpu.PrefetchScalarGridSpec(
            num_scalar_prefetch=2, grid=(n_blocks,),
            in_specs=[pl.BlockSpec((tb * slab, LANES), blk),
                      pl.BlockSpec((None, None, d, ff), wsel),
                      pl.BlockSpec((None, None, d, ff), wsel)],
            out_specs=pl.BlockSpec((tb, ff), out_blk)),
        out_shape=jax.ShapeDtypeStruct((n_slots, ff), BF16),
        compiler_params=_params(("arbitrary",)),
        name="moe_gate_up",
    )(block_e, n_used, x_slots, w_gate, w_up)
    return pl.pallas_call(
        functools.partial(_moe_down_kernel, slab=slab),
        grid_spec=pltpu.PrefetchScalarGridSpec(
            num_scalar_prefetch=2, grid=(n_blocks,),
            in_specs=[pl.BlockSpec((tb, ff), blk),
                      pl.BlockSpec((None, None, ff, d), wsel)],
            out_specs=pl.BlockSpec((tb * slab, LANES), out_blk)),
        out_shape=jax.ShapeDtypeStruct((n_slots * slab, LANES), U32),
        compiler_params=_params(("arbitrary",)),
        name="moe_down",
    )(block_e, n_used, hmid, w_down)


def moe_router_weights(w_rg, b_rg, w_re, b_re, layer):
    w = jnp.concatenate([w_re[layer], w_rg[layer]], axis=-1).T
    b = jnp.concatenate([b_re[layer], b_rg[layer]])[:, None]
    pad = -w.shape[0] % 8
    return jnp.pad(w, ((0, pad), (0, 0))), jnp.pad(b, ((0, pad), (0, 0)))


def hier_moe_residual(x, h, logits_t, bias, g2, row_map, w_gate, w_up, w_down, layer, *, n_grp, tm):
    rows, d = x.shape
    slab = d // 2 // LANES
    nsteps = rows // tm
    n_exp = w_gate.shape[1]
    tb = MOE_ROW_BLOCK
    n_pair = rows * MOE_TOP_K
    n_blocks = -(-(n_pair + n_exp * (tb - 1)) // tb)
    gates_t, dest_t, counts = moe_route(logits_t, bias, n_grp=n_grp, epg=n_exp // n_grp, tb=tb)
    counts = counts[:, 0]
    blocks = (counts + tb - 1) // tb
    blk_end = jnp.cumsum(blocks)
    starts = ((blk_end - blocks) * tb).astype(jnp.int32)
    n_used = blk_end[-1].astype(jnp.int32)
    block_e = jnp.minimum(jnp.searchsorted(blk_end, jnp.arange(n_blocks), side="right"), n_exp - 1)
    block_e = jnp.where(jnp.arange(n_blocks) < n_used, block_e, block_e[n_used - 1]).astype(jnp.int32)
    dest = dest_t.reshape(-1)

    x_slots = pl.pallas_call(
        functools.partial(_dispatch_kernel, rows=rows, tm=tm, tb=tb, n_exp=n_exp, n_blocks=n_blocks, slab=slab),
        grid_spec=pltpu.PrefetchScalarGridSpec(
            num_scalar_prefetch=4, grid=(nsteps,),
            in_specs=[pl.BlockSpec((tm * slab, LANES), lambda i, *_: (i, 0))],
            out_specs=pl.BlockSpec(memory_space=pl.ANY),
            scratch_shapes=[pltpu.VMEM((tb * slab, LANES), U32), pltpu.SemaphoreType.DMA,
                            pltpu.SemaphoreType.DMA]),
        out_shape=jax.ShapeDtypeStruct((n_blocks * tb * slab, LANES), U32),
        compiler_params=_params(("arbitrary",)),
        name="moe_dispatch",
    )(dest, counts, starts, n_used.reshape(1), h)
    y_slots = moe_experts(x_slots, block_e, n_used.reshape(1), w_gate, w_up, w_down, layer)
    return pl.pallas_call(
        functools.partial(_combine_kernel, rows=rows, tm=tm, slab=slab),
        grid_spec=pltpu.PrefetchScalarGridSpec(
            num_scalar_prefetch=1, grid=(nsteps,),
            in_specs=[pl.BlockSpec((tm, d), lambda i, *_: (i, 0)),
                      pl.BlockSpec((None, 1, d), lambda i, *_: (row_map(i * tm), 0, 0)),
                      pl.BlockSpec((tm, MOE_TOP_K), lambda i, *_: (i, 0)),
                      pl.BlockSpec(memory_space=pl.ANY)],
            out_specs=pl.BlockSpec((tm, d), lambda i, *_: (i, 0)),
            scratch_shapes=[pltpu.VMEM((MOE_TOP_K, tm * slab, LANES), U32), pltpu.SemaphoreType.DMA]),
        out_shape=jax.ShapeDtypeStruct((rows, d), F32),
        compiler_params=_params(("arbitrary",)),
        name="moe_combine",
    )(dest, x, g2, gates_t.T, y_slots)


def _rope_lane_tables(length, hd, n_rows):
    grid_rows = length // ROPE_GRID_W
    row = jnp.broadcast_to(jnp.arange(grid_rows, dtype=F32)[:, None], (grid_rows, ROPE_GRID_W)).reshape(length)
    col = jnp.broadcast_to(jnp.arange(ROPE_GRID_W, dtype=F32)[None, :], (grid_rows, ROPE_GRID_W)).reshape(length)
    axis_dim = hd // 2
    inv_freq = ROPE_BASE ** (-jnp.arange(0, axis_dim, 2, dtype=F32) / axis_dim)
    ang_r = row[:, None] * inv_freq
    ang_c = col[:, None] * inv_freq
    cos = jnp.concatenate([jnp.cos(ang_r)] * 2 + [jnp.cos(ang_c)] * 2, axis=1)
    sin = jnp.concatenate([-jnp.sin(ang_r), jnp.sin(ang_r), -jnp.sin(ang_c), jnp.sin(ang_c)], axis=1)
    reps = (n_rows // length, 2 * LANES // (2 * hd))
    return jnp.tile(cos, reps), jnp.tile(sin, reps)


def kernel(x_prompt, x_sample, cache_attn_k, cache_attn_v, state_mlstm_C, state_mlstm_n, state_mlstm_m, state_ssd, c, c_ctx, norm_mix, norm_ffn, w_mod, b_mod, attn_w_in, attn_q_norm, attn_k_norm, attn_lambda, attn_sub_norm, attn_w_out, mlstm_w_in, mlstm_conv_w, mlstm_conv_b, mlstm_w_gate, mlstm_b_gate, mlstm_head_norm, mlstm_w_out, ssd_w_in, ssd_conv_w, ssd_conv_b, ssd_dt_bias, ssd_a_log, ssd_d_skip, ssd_norm, ssd_w_out, hyena_w_in, hyena_conv_w, hyena_conv_b, hyena_f_w1, hyena_f_b1, hyena_f_w2, hyena_f_b2, hyena_f_w3, hyena_f_freq, hyena_skip_bias, hyena_w_out, moe_w_group, moe_b_group, moe_w_expert, moe_b_expert, moe_w_gate, moe_w_up, moe_w_down):
    batch, seq, dm = x_prompt.shape
    dec_batch, dec_seq, _ = x_sample.shape
    depth = norm_mix.shape[0]
    n_ctx = batch * seq
    n_lat = dec_batch * dec_seq
    rows = n_ctx + n_lat
    past = cache_attn_k.shape[2]
    a_heads, a_hd = cache_attn_k.shape[3], cache_attn_k.shape[5]
    m_heads, m_dk, m_dv = state_mlstm_C.shape[3:]
    s_heads, s_p, s_n = state_ssd.shape[3:]
    s_inner = s_heads * s_p
    s_groups = (ssd_conv_w.shape[-1] - s_inner) // (2 * s_n)
    s_hpg = s_heads // s_groups

    def row_map(r):
        return jnp.where(r < n_ctx, 0, 1 + (r - n_ctx) // dec_seq)

    row_unit = math.gcd(n_ctx, dec_seq)
    tm_norm = _tile(row_unit, 256, 8)
    tm_res = _tile(row_unit, 1024, 8)
    seq_lcm = math.lcm(seq, dec_seq)
    pass_unit = math.gcd(n_ctx, n_lat)
    assert pass_unit % seq_lcm == 0 and seq & (seq - 1) == 0 and dec_seq & (dec_seq - 1) == 0
    tm_seq = seq_lcm * max(k for k in range(1, max(2, 2048 // seq_lcm + 1)) if pass_unit % (seq_lcm * k) == 0)
    seq_of_tile = lambda i: jnp.where(i * tm_seq < n_ctx, seq, dec_seq)
    parts = ((0, batch, seq), (n_ctx, dec_batch, dec_seq))
    ones = lambda n: jnp.ones((n,), F32)

    tn_of = lambda n: _tile(n, 512, LANES)

    def conv_extras(taps, bias, scale, col0):
        tn = tn_of(scale.shape[0])
        assert col0 % tn == 0
        return [(taps, (3, tn), lambda i, j, c=col0 // tn: (0, j + c)), _row_vec(bias, tn, col0), _row_vec(scale, tn)]

    n_cond = 1 + dec_batch
    cond = jnp.concatenate([c_ctx[None, :], c], axis=0)
    cond = jnp.pad(jax.nn.silu(cond), ((0, -n_cond % 8), (0, 0))).astype(BF16)

    x = jnp.concatenate([x_prompt.reshape(n_ctx, dm), x_sample.reshape(n_lat, dm)], axis=0)
    outs = {}
    for i in range(depth):
        kind, j = i % 4, i // 4
        mod = matmul(cond, w_mod, w_idx=i, tm=8, tn=1024, tk=2048, extras=[_row_vec(b_mod[i], 1024)],
                     epi=_epi_bias, name="adaln_mod")
        sh1, sc1, g1, sh2, sc2, g2 = [mod[:, None, q * dm:(q + 1) * dm] for q in range(6)]
        h1 = norm_mod(x, norm_mix[i], sc1, sh1, row_map, tm_norm)

        if kind == 0:
            lam_init = 0.8 - 0.6 * math.exp(-0.3 * i)
            tn = tn_of(dm)
            cos, sin = _rope_lane_tables(dec_seq, a_hd, tm_seq)
            tabs = [(cos, (tm_seq, LANES), lambda i_, j_: (0, 0)), (sin, (tm_seq, LANES), lambda i_, j_: (0, 0))]
            rep = dm // a_hd
            q = matmul(h1, attn_w_in, w_idx=j, n=dm, tm=tm_seq, tn=tn, out_dtypes=(BF16,),
                       extras=[_row_vec(jnp.tile(attn_q_norm[j], rep), tn)] + tabs,
                       epi=_make_epi_qk(tm_seq, tn, a_hd, n_ctx, 1), name="attn_in_q")
            k, k_plain = matmul(h1, attn_w_in, w_idx=j, n0=dm, n=dm, tm=tm_seq, tn=tn, out_dtypes=(BF16, F32),
                                extras=[_row_vec(jnp.tile(attn_k_norm[j], rep), tn)] + tabs,
                                epi=_make_epi_qk(tm_seq, tn, a_hd, n_ctx, 2), name="attn_in_k")
            v, v_f32 = matmul(h1, attn_w_in, w_idx=j, n0=2 * dm, n=dm, tm=tm_seq, tn=tn, out_dtypes=(BF16, F32),
                              epi=_epi_two, name="attn_in_v")
            outs["k"] = k_plain[:n_ctx].reshape(batch, 1, seq, a_heads, 2, a_hd)
            outs["v"] = v_f32[:n_ctx].reshape(batch, 1, seq, a_heads, 2 * a_hd)
            lv = attn_lambda[j]
            lam = jnp.exp(jnp.sum(lv[0] * lv[1])) - jnp.exp(jnp.sum(lv[2] * lv[3])) + lam_init
            cache = (cache_attn_k[:, j].reshape(dec_batch * past, dm), cache_attn_v[:, j].reshape(dec_batch * past, dm))
            o_parts = [diff_attention(q, k, v, cache if row0 else None, lam, attn_sub_norm[j], row0=row0, nseq=nseq,
                                      length=length, heads=a_heads, post_scale=1.0 - lam_init)
                       for row0, nseq, length in parts]
            mix_in, w_out = jnp.concatenate(o_parts, axis=0), attn_w_out

        elif kind == 1:
            qk_w = 2 * m_heads * m_dk
            v_w = m_heads * m_dv
            k_scale = jnp.concatenate([ones(qk_w // 2), jnp.full((qk_w // 2,), m_dk ** -0.5, F32)])
            qk = matmul(h1, mlstm_w_in, w_idx=j, n=qk_w, tm=tm_seq, out_dtypes=(BF16,),
                        extras=conv_extras(mlstm_conv_w[j], mlstm_conv_b[j], k_scale, 0),
                        epi=_make_epi_conv(tm_seq, seq_of_tile, True), name="mlstm_in_qk")
            v = matmul(h1, mlstm_w_in, w_idx=j, n0=qk_w, n=v_w, out_dtypes=(BF16,), name="mlstm_in_v")
            o_gate = matmul(h1, mlstm_w_in, w_idx=j, n0=qk_w + v_w, n=v_w, epi=_epi_sigmoid, name="mlstm_in_o")
            g = matmul(h1, mlstm_w_gate, w_idx=j, x3=True, tm=512, extras=[_row_vec(mlstm_b_gate[j], 4 * m_heads)],
                       epi=_epi_bias, name="mlstm_gate")
            g = g.reshape(rows, 2, 2, m_heads)
            gates = jnp.stack([g[:, :, 0], jax.nn.log_sigmoid(g[:, :, 1])], axis=-1)
            gcol = gates.transpose(1, 2, 0, 3)
            init = (state_mlstm_C[:, j], state_mlstm_n[:, j][:, :, :, None, :], state_mlstm_m[:, j][:, :, :, None, None])
            y_parts = []
            for row0, nseq, length in parts:
                t = _tile(length, MLSTM_CHUNK, 8)
                n = nseq * length
                gc = gcol[:, :, row0:row0 + n]
                grow = gc.reshape(2, m_heads, n // t, t, 2).transpose(0, 1, 2, 4, 3)
                res = mlstm_scan(qk, v, gc, grow, init if row0 else None, row0=row0, nseq=nseq, length=length,
                                 heads=m_heads, emit_state=not row0)
                if not row0:
                    outs["C"] = res[1][:, None]
                    outs["n"] = res[2][:, None, :, :, 0]
                    outs["m"] = res[3][:, None, :, :, 0, 0]
                y_parts.append(mlstm_post(res[0], o_gate, mlstm_head_norm[j], row0=row0, heads=m_heads))
            mix_in, w_out = jnp.concatenate(y_parts, axis=0), mlstm_w_out

        elif kind == 2:
            gn = s_groups * s_n
            conv = _make_epi_conv(tm_seq, seq_of_tile, True)
            z = matmul(h1, ssd_w_in, w_idx=j, n=s_inner, name="ssd_in_z")
            xs = matmul(h1, ssd_w_in, w_idx=j, n0=s_inner, n=s_inner, tm=tm_seq,
                        extras=conv_extras(ssd_conv_w[j], ssd_conv_b[j], ones(s_inner), 0),
                        epi=conv, name="ssd_in_x")
            bc = matmul(h1, ssd_w_in, w_idx=j, n0=2 * s_inner, n=2 * gn, tm=tm_seq, out_dtypes=(BF16,),
                        extras=conv_extras(ssd_conv_w[j], ssd_conv_b[j], ones(2 * gn), s_inner),
                        epi=conv, name="ssd_in_bc")
            dt_raw = matmul(h1, ssd_w_in, w_idx=j, n0=2 * s_inner + 2 * gn, n=2 * s_heads, x3=True, tm=512, name="ssd_dt")
            dt = jax.nn.softplus(dt_raw.reshape(rows, 2, s_heads) + ssd_dt_bias[j])
            a_neg = -jnp.exp(ssd_a_log[j])
            a = dt * a_neg
            to_col = lambda y: y.reshape(rows, 2, s_groups, s_hpg).transpose(1, 2, 0, 3)
            acol, dtcol = to_col(a), to_col(dt)
            aneg_lanes = jnp.repeat(a_neg, s_p, axis=1)[:, None, :]
            init = state_ssd[:, j].reshape(dec_batch, 2, s_groups, s_hpg * s_p, s_n)
            d_skip_cols = jnp.repeat(ssd_d_skip[j], s_p)
            y_parts = []
            for row0, nseq, length in parts:
                n = nseq * length
                res = ssd_scan(xs, bc, acol[:, :, row0:row0 + n], dtcol[:, :, row0:row0 + n], aneg_lanes,
                               init if row0 else None, row0=row0,
                               nseq=nseq, length=length, groups=s_groups, hpg=s_hpg, emit_state=not row0)
                if not row0:
                    outs["ssd"] = res[1].reshape(batch, 1, 2, s_heads, s_p, s_n)
                y_parts.append(ssd_post(res[0], xs, z, d_skip_cols, ssd_norm[j], row0=row0))
            mix_in, w_out = jnp.concatenate(y_parts, axis=0), ssd_w_out

        else:
            proj = matmul(h1, hyena_w_in, w_idx=j, tm=tm_seq,
                          extras=conv_extras(hyena_conv_w[j], hyena_conv_b[j], ones(3 * dm), 0),
                          epi=_make_epi_conv(tm_seq, seq_of_tile, False), name="hyena_in")
            y_parts = []
            for row0, nseq, length in parts:
                fmat, gmat, tps, tqs = _hyena_filter_spectra(length, hyena_f_w1[j], hyena_f_b1[j], hyena_f_w2[j],
                                                             hyena_f_b2[j], hyena_f_w3[j], hyena_f_freq[j], dm)
                y = hyena_conv(proj, (row0, 0), fmat, gmat, tps[0], tqs[0], hyena_skip_bias[j][0], proj, (row0, dm),
                               nseq=nseq, length=length, d=dm, out_dtype=F32)
                y = hyena_conv(y, (0, 0), fmat, gmat, tps[1], tqs[1], hyena_skip_bias[j][1], proj, (row0, 2 * dm),
                               nseq=nseq, length=length, d=dm, out_dtype=BF16)
                y_parts.append(y)
            mix_in, w_out = jnp.concatenate(y_parts, axis=0), hyena_w_out

        tn = _tile(dm, 1024, LANES)
        x = matmul(mix_in, w_out[j].astype(BF16), tm=tm_res, tn=tn,
                   extras=[(x, (tm_res, tn), lambda i_, j_: (i_, j_)),
                           (g1, (None, 1, tn), lambda i_, j_: (row_map(i_ * tm_res), 0, j_))],
                   epi=_epi_resid, name="mixer_out")
        w_rt, b_rt = moe_router_weights(moe_w_group, moe_b_group, moe_w_expert, moe_b_expert, i)
        h2, logits_t = norm_mod_router(x, norm_ffn[i], sc2, sh2, w_rt, row_map, tm_norm)
        x = hier_moe_residual(x, h2, logits_t, b_rt, g2, row_map, moe_w_gate, moe_w_up, moe_w_down, i,
                              n_grp=moe_w_group.shape[-1], tm=tm_norm)

    y_prompt = x[:n_ctx].reshape(batch, seq, dm)
    y_sample = x[n_ctx:].reshape(dec_batch, dec_seq, dm)
    return (y_prompt, y_sample, outs["k"], outs["v"], outs["C"], outs["n"], outs["m"], outs["ssd"])
```

```python
import functools
import math

import jax
import jax.numpy as jnp
from jax import lax
from jax.experimental import pallas as pl
from jax.experimental.pallas import tpu as pltpu

F32 = jnp.float32
BF16 = jnp.bfloat16
U32 = jnp.uint32
LANES = 128
NORM_EPS = 1e-6
ROPE_GRID_W = 64
ROPE_BASE = 10000.0
HYENA_EMB = 33
HYENA_DECAY_SHORT = 0.3
HYENA_DECAY_LONG = 1.5
HYENA_TARGET = 1e-2
MOE_TOP_K = 2
MOE_ROW_BLOCK = 512
MLSTM_CHUNK = 256
SSD_CHUNK = 256
VMEM_LIMIT = 56 * 1024 * 1024

_NT = (((1,), (1,)), ((), ()))
_TN = (((0,), (0,)), ((), ()))


def _tile(n, pref, align):
    if n <= pref:
        return n
    t = (pref // align) * align
    while t >= align:
        if n % t == 0:
            return t
        t -= align
    return n


def _params(sem):
    return pltpu.CompilerParams(dimension_semantics=sem, vmem_limit_bytes=VMEM_LIMIT)


def _dot(a, b):
    return jnp.dot(a, b, preferred_element_type=F32)


def _split_bf16(a):
    hi = a.astype(BF16)
    lo = (a - hi.astype(F32)).astype(BF16)
    return hi, lo


def _silu(x):
    return x / (1.0 + jnp.exp(-x))


def _mm_kernel(*refs, nk, x3, n_extra, n_out, epi):
    x_ref, w_ref = refs[:2]
    extra = refs[2:2 + n_extra]
    o_refs = refs[2 + n_extra:2 + n_extra + n_out]

    if x3:
        xh, xl = _split_bf16(x_ref[...].astype(F32))
        wh, wl = _split_bf16(w_ref[...].astype(F32))
        part = _dot(xh, wh) + _dot(xh, wl) + _dot(xl, wh)
    else:
        part = _dot(x_ref[...].astype(BF16), w_ref[...].astype(BF16))

    def finish(r):
        for o_ref, val in zip(o_refs, epi(r, extra, pl.program_id(0))):
            o_ref[...] = val.astype(o_ref.dtype)

    if nk == 1:
        finish(part)
        return
    acc_ref = refs[-1]
    k = pl.program_id(2)

    @pl.when(k == 0)
    def _():
        acc_ref[...] = part

    @pl.when(k > 0)
    def _():
        acc_ref[...] += part

    @pl.when(k == nk - 1)
    def _():
        finish(acc_ref[...])


def _epi_plain(r, extra, i):
    return (r,)


def _epi_bias(r, extra, i):
    return (r + extra[0][...],)


def _epi_resid(r, extra, i):
    return (extra[0][...] + extra[1][...] * r,)


def _epi_sigmoid(r, extra, i):
    return (1.0 / (1.0 + jnp.exp(-r)),)


def _epi_two(r, extra, i):
    return (r, r)


def matmul(x, w, *, w_idx=None, n0=0, n=None, tm=2048, tn=512, tk=2048, out_dtypes=(F32,), x3=False,
           extras=(), epi=_epi_plain, name="matmul"):
    m, kdim = x.shape
    n = w.shape[-1] if n is None else n
    tm = _tile(m, tm, 8)
    tk = _tile(kdim, tk, LANES)
    tn = _tile(n, tn, LANES)
    assert n0 % tn == 0 and m % tm == 0 and kdim % tk == 0 and n % tn == 0
    j0 = n0 // tn
    nk = kdim // tk
    if w.ndim == 3:
        w_spec = pl.BlockSpec((None, tk, tn), lambda i, j, k: (w_idx, k, j + j0))
    else:
        w_spec = pl.BlockSpec((tk, tn), lambda i, j, k: (k, j + j0))
    in_specs = [pl.BlockSpec((tm, tk), lambda i, j, k: (i, k)), w_spec]
    args = [x, w]
    for arr, shape, imap in extras:
        in_specs.append(pl.BlockSpec(shape, lambda i, j, k, imap=imap: imap(i, j)))
        args.append(arr)
    outs = pl.pallas_call(
        functools.partial(_mm_kernel, nk=nk, x3=x3, n_extra=len(extras), n_out=len(out_dtypes), epi=epi),
        grid=(m // tm, n // tn, nk),
        in_specs=in_specs,
        out_specs=[pl.BlockSpec((tm, tn), lambda i, j, k: (i, j)) for _ in out_dtypes],
        out_shape=[jax.ShapeDtypeStruct((m, n), dt) for dt in out_dtypes],
        scratch_shapes=[pltpu.VMEM((tm, tn), F32)] if nk > 1 else [],
        compiler_params=_params(("parallel", "parallel", "arbitrary")),
        name=name,
    )(*args)
    return outs[0] if len(out_dtypes) == 1 else outs


def _row_vec(v, tn, col0=0):
    return (v.reshape(1, -1), (1, tn), lambda i, j, c=col0 // tn: (0, j + c))


def _make_epi_conv(tm, seq_of_tile, act):
    def epi(r, extra, i):
        taps = extra[0][...]
        length = seq_of_tile(i)
        pos = lax.broadcasted_iota(jnp.int32, (tm, 1), 0) & (length - 1)
        prev = jnp.where(pos == 0, 0.0, pltpu.roll(r, 1, 0))
        nxt = jnp.where(pos == length - 1, 0.0, pltpu.roll(r, tm - 1, 0))
        y = prev * taps[0:1] + r * taps[1:2] + nxt * taps[2:3] + extra[1][...]
        if act:
            y = _silu(y)
        return (y * extra[2][...],)
    return epi


def _make_epi_qk(tm, tn, hd, n_ctx, n_out):
    def epi(r, extra, i):
        ga = lax.broadcasted_iota(jnp.int32, (LANES, LANES), 0) // hd
        gb = lax.broadcasted_iota(jnp.int32, (LANES, LANES), 1) // hd
        avg = jnp.where(ga == gb, 1.0 / hd, 0.0).astype(BF16)
        lane = lax.broadcasted_iota(jnp.int32, (1, LANES), 1)
        first_half = (lane & (hd // 2 - 1)) < hd // 4
        rope_on = i * tm >= n_ctx
        cos = extra[1][...]
        sin = extra[2][...]
        rot, plain = [], []
        for c0 in range(0, tn, LANES):
            xc = r[:, c0:c0 + LANES]
            hi, lo = _split_bf16(xc * xc)
            y = xc * lax.rsqrt(_dot(hi, avg) + _dot(lo, avg) + NORM_EPS) * extra[0][:, c0:c0 + LANES]
            partner = jnp.where(first_half, pltpu.roll(y, LANES - hd // 4, 1), pltpu.roll(y, hd // 4, 1))
            rot.append(jnp.where(rope_on, y * cos + partner * sin, y))
            plain.append(y)
        outs = (jnp.concatenate(rot, axis=1),)
        if n_out == 2:
            outs += (jnp.concatenate(plain, axis=1),)
        return outs
    return epi


def _norm_mod_kernel(x_ref, w_ref, sc_ref, sh_ref, o_ref):
    x = x_ref[...]
    y = x * lax.rsqrt(jnp.mean(x * x, axis=-1, keepdims=True) + NORM_EPS) * w_ref[...]
    o_ref[...] = (y * (1.0 + sc_ref[...]) + sh_ref[...]).astype(o_ref.dtype)


def norm_mod(x, w, scale, shift, row_map, tm):
    rows, d = x.shape
    row = pl.BlockSpec((tm, d), lambda i: (i, 0))
    mod = pl.BlockSpec((None, 1, d), lambda i: (row_map(i * tm), 0, 0))
    return pl.pallas_call(
        _norm_mod_kernel,
        grid=(rows // tm,),
        in_specs=[row, pl.BlockSpec((1, d), lambda i: (0, 0)), mod, mod],
        out_specs=row,
        out_shape=jax.ShapeDtypeStruct((rows, d), BF16),
        compiler_params=_params(("parallel",)),
        name="norm_mod",
    )(x, w.reshape(1, d), scale, shift)


def _store_slab(ref, val, slab):
    n = val.shape[0]
    for s in range(slab):
        ref[pl.ds(s, n, stride=slab), :] = val[:, s * LANES:(s + 1) * LANES]


def _load_slab(ref, n, slab):
    return jnp.concatenate([ref[pl.ds(s, n, stride=slab), :] for s in range(slab)], axis=1)


def _pack_bf16_pairs(x):
    c = x.shape[1] // 2
    hi = lax.bitcast_convert_type(x[:, :c].astype(BF16).astype(F32), U32)
    lo = lax.bitcast_convert_type(x[:, c:].astype(BF16).astype(F32), U32)
    return hi | (lo >> 16)


def _unpack_hi(p):
    return lax.bitcast_convert_type(p & jnp.uint32(0xFFFF0000), F32)


def _unpack_lo(p):
    return lax.bitcast_convert_type(p << 16, F32)


def _slab_at(ref, row, n, slab):
    return ref.at[pl.ds(pl.multiple_of(row * slab, slab), n * slab)]


def _norm_router_kernel(x_ref, w_ref, sc_ref, sh_ref, wr_ref, h_ref, lt_ref, *, slab):
    x = x_ref[...]
    y = x * lax.rsqrt(jnp.mean(x * x, axis=-1, keepdims=True) + NORM_EPS) * w_ref[...]
    h = y * (1.0 + sc_ref[...]) + sh_ref[...]
    _store_slab(h_ref, _pack_bf16_pairs(h), slab)
    hh, hl = _split_bf16(h)
    wh, wl = _split_bf16(wr_ref[...])
    nt = lambda a, b: lax.dot_general(a, b, _NT, preferred_element_type=F32)
    lt_ref[...] = nt(wh, hh) + nt(wh, hl) + nt(wl, hh)


def norm_mod_router(x, w, scale, shift, w_router_t, row_map, tm):
    rows, d = x.shape
    nr = w_router_t.shape[0]
    slab = d // 2 // LANES
    row = pl.BlockSpec((tm, d), lambda i: (i, 0))
    mod = pl.BlockSpec((None, 1, d), lambda i: (row_map(i * tm), 0, 0))
    return pl.pallas_call(
        functools.partial(_norm_router_kernel, slab=slab),
        grid=(rows // tm,),
        in_specs=[row, pl.BlockSpec((1, d), lambda i: (0, 0)), mod, mod, pl.BlockSpec((nr, d), lambda i: (0, 0))],
        out_specs=[pl.BlockSpec((tm * slab, LANES), lambda i: (i, 0)), pl.BlockSpec((nr, tm), lambda i: (0, i))],
        out_shape=[jax.ShapeDtypeStruct((rows * slab, LANES), U32), jax.ShapeDtypeStruct((nr, rows), F32)],
        compiler_params=_params(("parallel",)),
        name="norm_mod_router",
    )(x, w.reshape(1, d), scale, shift, w_router_t)


def _attn_kernel(lam_ref, q_ref, k_ref, v_ref, *rest, hd2, scale, post_scale, has_cache):
    if has_cache:
        kc_ref, vc_ref, sn_ref, o_ref = rest
    else:
        sn_ref, o_ref = rest
    nt = lambda a, b: lax.dot_general(a, b, _NT, preferred_element_type=F32)
    lane = lax.broadcasted_iota(jnp.int32, (q_ref.shape[0], hd2), 1)
    for c0 in range(0, q_ref.shape[1], hd2):
        cols = slice(c0, c0 + hd2)
        q = q_ref[:, cols].astype(F32) * scale
        k = k_ref[:, cols]
        if has_cache:
            kc = kc_ref[:, cols].astype(BF16)

        def exps(qm):
            s = nt(qm, k)
            m = jnp.max(s, axis=-1, keepdims=True)
            if not has_cache:
                e = jnp.exp(s - m)
                return e, None, 1.0 / jnp.sum(e, axis=-1, keepdims=True)
            sc = nt(qm, kc)
            m = jnp.maximum(m, jnp.max(sc, axis=-1, keepdims=True))
            e = jnp.exp(s - m)
            ec = jnp.exp(sc - m)
            return e, ec, 1.0 / (jnp.sum(e, axis=-1, keepdims=True) + jnp.sum(ec, axis=-1, keepdims=True))

        e1, e1c, r1 = exps(jnp.where(lane < hd2 // 2, q, 0.0).astype(BF16))
        e2, e2c, r2 = exps(jnp.where(lane >= hd2 // 2, q, 0.0).astype(BF16))
        r2 = lam_ref[0] * r2
        o = _dot((e1 * r1 - e2 * r2).astype(BF16), v_ref[:, cols])
        if has_cache:
            o = o + _dot((e1c * r1 - e2c * r2).astype(BF16), vc_ref[:, cols].astype(BF16))
        o = o * lax.rsqrt(jnp.mean(o * o, axis=-1, keepdims=True) + NORM_EPS) * sn_ref[...] * post_scale
        o_ref[:, cols] = o.astype(o_ref.dtype)


def diff_attention(q, k, v, cache, lam, sub_norm, *, row0, nseq, length, heads, post_scale):
    hd2 = q.shape[1] // heads
    hp = max(p for p in (1, 2, 4) if heads % p == 0)
    wid = hp * hd2
    tq = _tile(length, 256, 8)
    nq = length // tq
    in_specs = [
        pl.BlockSpec(memory_space=pltpu.SMEM),
        pl.BlockSpec((tq, wid), lambda s, h, i: (row0 // tq + s * nq + i, h)),
        pl.BlockSpec((length, wid), lambda s, h, i: (row0 // length + s, h)),
        pl.BlockSpec((length, wid), lambda s, h, i: (row0 // length + s, h)),
    ]
    args = [lam.reshape(1), q, k, v]
    if cache is not None:
        past = cache[0].shape[0] // nseq
        in_specs += [pl.BlockSpec((past, wid), lambda s, h, i: (s, h))] * 2
        args += list(cache)
    in_specs.append(pl.BlockSpec((1, hd2), lambda s, h, i: (0, 0)))
    args.append(sub_norm.reshape(1, hd2))
    return pl.pallas_call(
        functools.partial(_attn_kernel, hd2=hd2, scale=(hd2 // 2) ** -0.5, post_scale=post_scale,
                          has_cache=cache is not None),
        grid=(nseq, heads // hp, nq),
        in_specs=in_specs,
        out_specs=pl.BlockSpec((tq, wid), lambda s, h, i: (s * nq + i, h)),
        out_shape=jax.ShapeDtypeStruct((nseq * length, q.shape[1]), BF16),
        compiler_params=_params(("parallel", "parallel", "parallel")),
        name="diff_attention",
    )(*args)


def _causal_masks(t, d):
    sign = jnp.where(d == 0, 1, -1)
    r = lax.broadcasted_iota(jnp.int32, (t, t), 0)
    s = lax.broadcasted_iota(jnp.int32, (t, t), 1)
    diff = (s - r) * sign
    return diff <= 0, diff >= 0


def _scan_masks(t):
    r = jnp.arange(t)[:, None]
    s = jnp.arange(t)[None, :]
    vis = jnp.stack([s <= r, s >= r])
    return vis.astype(BF16), vis.astype(F32), jnp.where(vis, 0.0, -jnp.inf).astype(F32)


def _mask_specs(t, dir_of):
    return [pl.BlockSpec((None, t, t), lambda *g: (dir_of(*g), 0, 0)),
            pl.BlockSpec((None, t, t), lambda *g: (1 - dir_of(*g), 0, 0)),
            pl.BlockSpec((None, t, t), lambda *g: (dir_of(*g), 0, 0))]


def _split3(a):
    h = a.astype(BF16)
    r1 = a - h.astype(F32)
    m = r1.astype(BF16)
    return h, m, (r1 - m.astype(F32)).astype(BF16)


def _masked_cumsum(vis, a):
    h, m, l = _split3(a)
    return _dot(vis, h) + _dot(vis, m) + _dot(vis, l)


def _chunk_row(s, d, c, nc):
    return s * nc + jnp.where(d == 0, c, nc - 1 - c)


def _mlstm_kernel(*refs, t, nc, has_init, emit_state):
    refs = list(refs)
    q_ref, k_ref, v_ref, gcol_ref, grow_ref = refs[:5]
    pos = 5
    if has_init:
        c0_ref, n0_ref, m0_ref = refs[pos:pos + 3]
        pos += 3
    h_ref = refs[pos]
    pos += 1
    if emit_state:
        co_ref, no_ref, mo_ref = refs[pos:pos + 3]
        pos += 3
    c_s, n_s, m_s = refs[pos:pos + 3]
    d = pl.program_id(1)
    c = pl.program_id(3)

    @pl.when(c == 0)
    def _():
        if has_init:
            c_s[...] = c0_ref[...]
            n_s[...] = n0_ref[...]
            m_s[...] = m0_ref[...]
        else:
            c_s[...] = jnp.zeros_like(c_s)
            n_s[...] = jnp.zeros_like(n_s)
            m_s[...] = jnp.zeros_like(m_s)

    dk = c_s.shape[1]
    dv = c_s.shape[2]
    for hh in range(c_s.shape[0]):
        q = q_ref[:, hh * dk:(hh + 1) * dk]
        k = k_ref[:, hh * dk:(hh + 1) * dk]
        v = v_ref[:, hh * dv:(hh + 1) * dv]
        i_col = gcol_ref[hh, :, 0:1]
        f_col = gcol_ref[hh, :, 1:2]
        i_row = grow_ref[hh, 0:1, :]
        f_row = grow_ref[hh, 1:2, :]
        causal, causal_t = _causal_masks(t, d)
        b_col = jnp.sum(jnp.where(causal, f_row, 0.0), axis=1, keepdims=True)
        b_row = jnp.sum(jnp.where(causal_t, f_col, 0.0), axis=0, keepdims=True)
        dlog = jnp.where(causal, b_col - b_row + i_row, -jnp.inf)
        m_prev = m_s[hh]
        inter = b_col + m_prev
        mt = jnp.maximum(inter, jnp.max(dlog, axis=1, keepdims=True))
        w_intra = jnp.exp(dlog - mt)
        w_inter = jnp.exp(inter - mt)
        cm = c_s[hh]
        nv = n_s[hh]
        sm = lax.dot_general(q, k, _NT, preferred_element_type=F32) * w_intra
        num = _dot(sm.astype(BF16), v) + w_inter * _dot(q, cm.astype(BF16))
        den = jnp.sum(sm, axis=1, keepdims=True) + w_inter * jnp.sum(q.astype(F32) * nv, axis=1, keepdims=True)
        h_ref[:, hh * dv:(hh + 1) * dv] = num / jnp.maximum(jnp.abs(den), jnp.exp(-mt))

        rowid = lax.broadcasted_iota(jnp.int32, (t, 1), 0)
        end_row = jnp.where(d == 0, t - 1, 0)
        m_new = jnp.sum(jnp.where(rowid == end_row, mt, 0.0), axis=0, keepdims=True)
        b_last = jnp.sum(f_row, axis=1, keepdims=True)
        kw = k.astype(F32) * jnp.exp(b_last - b_col + i_col - m_new)
        decay = jnp.exp(b_last + m_prev - m_new)
        c_new = decay * cm + lax.dot_general(kw.astype(BF16), v, _TN, preferred_element_type=F32)
        n_new = decay * nv + jnp.sum(kw, axis=0, keepdims=True)
        c_s[hh] = c_new
        n_s[hh] = n_new
        m_s[hh] = m_new
        if emit_state:
            @pl.when(c == nc - 1)
            def _(hh=hh, c_new=c_new, n_new=n_new, m_new=m_new):
                co_ref[hh] = c_new
                no_ref[hh] = n_new
                mo_ref[hh] = m_new


def mlstm_scan(qk, v, gcol, grow, init, *, row0, nseq, length, heads, emit_state):
    n = nseq * length
    dk = qk.shape[1] // (2 * heads)
    dv = v.shape[1] // heads
    t = _tile(length, MLSTM_CHUNK, 8)
    nc = length // t
    hp = 2 if heads % 2 == 0 else 1
    rc = lambda s, d, h, c: _chunk_row(s, d, c, nc)
    r0 = row0 // t
    in_specs = [
        pl.BlockSpec((t, hp * dk), lambda s, d, h, c: (r0 + rc(s, d, h, c), h)),
        pl.BlockSpec((t, hp * dk), lambda s, d, h, c: (r0 + rc(s, d, h, c), heads // hp + h)),
        pl.BlockSpec((t, hp * dv), lambda s, d, h, c: (r0 + rc(s, d, h, c), h)),
        pl.BlockSpec((None, hp, t, 2), lambda s, d, h, c: (d, h, rc(s, d, h, c), 0)),
        pl.BlockSpec((None, hp, None, 2, t), lambda s, d, h, c: (d, h, rc(s, d, h, c), 0, 0)),
    ]
    args = [qk, qk, v, gcol, grow]
    st_specs = [
        pl.BlockSpec((None, None, hp, dk, dv), lambda s, d, h, c: (s, d, h, 0, 0)),
        pl.BlockSpec((None, None, hp, 1, dk), lambda s, d, h, c: (s, d, h, 0, 0)),
        pl.BlockSpec((None, None, hp, 1, 1), lambda s, d, h, c: (s, d, h, 0, 0)),
    ]
    if init is not None:
        in_specs += st_specs
        args += list(init)
    out_specs = [pl.BlockSpec((None, t, hp * dv), lambda s, d, h, c: (d, rc(s, d, h, c), h))]
    out_shape = [jax.ShapeDtypeStruct((2, n, heads * dv), F32)]
    if emit_state:
        out_specs += st_specs
        out_shape += [jax.ShapeDtypeStruct((nseq, 2, heads, dk, dv), F32),
                      jax.ShapeDtypeStruct((nseq, 2, heads, 1, dk), F32),
                      jax.ShapeDtypeStruct((nseq, 2, heads, 1, 1), F32)]
    return pl.pallas_call(
        functools.partial(_mlstm_kernel, t=t, nc=nc, has_init=init is not None, emit_state=emit_state),
        grid=(nseq, 2, heads // hp, nc),
        in_specs=in_specs,
        out_specs=out_specs,
        out_shape=out_shape,
        scratch_shapes=[pltpu.VMEM((hp, dk, dv), F32), pltpu.VMEM((hp, 1, dk), F32), pltpu.VMEM((hp, 1, 1), F32)],
        compiler_params=_params(("parallel", "parallel", "parallel", "arbitrary")),
        name="mlstm_scan",
    )(*args)


def _mlstm_post_kernel(h_ref, g_ref, w_ref, o_ref):
    hs = h_ref[0] + h_ref[1]
    y = hs * lax.rsqrt(jnp.mean(hs * hs, axis=-1, keepdims=True) + NORM_EPS) * w_ref[...]
    o_ref[...] = (y * g_ref[...]).astype(o_ref.dtype)


def mlstm_post(h, o_gate, head_norm, *, row0, heads):
    _, n, width = h.shape
    dv = width // heads
    tm = _tile(n, 512, 8)
    return pl.pallas_call(
        _mlstm_post_kernel,
        grid=(n // tm, heads),
        in_specs=[pl.BlockSpec((2, tm, dv), lambda i, hh: (0, i, hh)),
                  pl.BlockSpec((tm, dv), lambda i, hh: (row0 // tm + i, hh)),
                  pl.BlockSpec((1, dv), lambda i, hh: (0, 0))],
        out_specs=pl.BlockSpec((tm, dv), lambda i, hh: (i, hh)),
        out_shape=jax.ShapeDtypeStruct((n, width), BF16),
        compiler_params=_params(("parallel", "parallel")),
        name="mlstm_post",
    )(h, o_gate, head_norm.reshape(1, dv))


def _ssd_kernel(*refs, t, nc, hpg, p, has_init, emit_state):
    refs = list(refs)
    x_ref, b_ref, c_ref, acol_ref, dtcol_ref, aneg_ref, vis_ref, vist_ref, neg_ref = refs[:9]
    pos = 9
    if has_init:
        s0_ref = refs[pos]
        pos += 1
    y_ref = refs[pos]
    pos += 1
    if emit_state:
        so_ref = refs[pos]
        pos += 1
    st = refs[pos]
    c = pl.program_id(3)
    width = hpg * p
    ns = st.shape[1]

    @pl.when(c == 0)
    def _():
        for gg in range(st.shape[0]):
            if has_init:
                st[gg] = s0_ref[gg].T
            else:
                st[gg] = jnp.zeros(st.shape[1:], F32)

    lane_head = lax.broadcasted_iota(jnp.int32, (1, width), 1) // p
    lane_pair = lax.broadcasted_iota(jnp.int32, (1, 2 * p), 1)
    vis_t = vist_ref[...]
    neg = neg_ref[...]
    for gg in range(st.shape[0]):
        lanes = slice(gg * width, (gg + 1) * width)
        bm = b_ref[:, gg * ns:(gg + 1) * ns]
        cmat = c_ref[:, gg * ns:(gg + 1) * ns]
        dt_e = jnp.zeros((t, width), F32)
        for e in range(hpg):
            dt_e = jnp.where(lane_head == e, dtcol_ref[gg, :, e:e + 1], dt_e)
        xd = x_ref[:, lanes] * dt_e
        xdb = xd.astype(BF16)
        a_e = dt_e * aneg_ref[:, lanes]
        cum_e = _masked_cumsum(vis_ref[...], a_e)
        tot_e = jnp.sum(a_e, axis=0, keepdims=True)
        cb = lax.dot_general(cmat, bm, _NT, preferred_element_type=F32)
        y_pairs = []
        for e in range(hpg):
            cum_col = cum_e[:, e * p:e * p + 1]
            cum_row = jnp.sum(vis_t * acol_ref[gg, :, e:e + 1], axis=0, keepdims=True)
            wmat = (cb * jnp.exp(cum_col - cum_row + neg)).astype(BF16)
            j = e // 2
            yp = _dot(wmat, xdb[:, j * 2 * p:(j + 1) * 2 * p])
            if e % 2 == 0:
                y_even = yp
            else:
                y_pairs.append(jnp.where(lane_pair < p, y_even, yp))
        s_prev = st[gg]
        y_off = _dot(cmat, s_prev.astype(BF16)) * jnp.exp(cum_e)
        y_ref[:, lanes] = jnp.concatenate(y_pairs, axis=1) + y_off
        xw = (xd * jnp.exp(tot_e - cum_e)).astype(BF16)
        s_new = jnp.exp(tot_e) * s_prev + lax.dot_general(bm, xw, _TN, preferred_element_type=F32)
        st[gg] = s_new
        if emit_state:
            @pl.when(c == nc - 1)
            def _(gg=gg, s_new=s_new):
                so_ref[gg] = s_new.T


def ssd_scan(x, bc, acol, dtcol, aneg, init, *, row0, nseq, length, groups, hpg, emit_state):
    n = nseq * length
    inner = x.shape[1]
    width = inner // groups
    p = width // hpg
    ns = bc.shape[1] // (2 * groups)
    t = _tile(length, SSD_CHUNK, 8)
    nc = length // t
    gp = 2 if groups % 2 == 0 else 1
    rc = lambda s, d, g, c: _chunk_row(s, d, c, nc)
    r0 = row0 // t
    in_specs = [
        pl.BlockSpec((t, gp * width), lambda s, d, g, c: (r0 + rc(s, d, g, c), g)),
        pl.BlockSpec((t, gp * ns), lambda s, d, g, c: (r0 + rc(s, d, g, c), g)),
        pl.BlockSpec((t, gp * ns), lambda s, d, g, c: (r0 + rc(s, d, g, c), groups // gp + g)),
        pl.BlockSpec((None, gp, t, hpg), lambda s, d, g, c: (d, g, rc(s, d, g, c), 0)),
        pl.BlockSpec((None, gp, t, hpg), lambda s, d, g, c: (d, g, rc(s, d, g, c), 0)),
        pl.BlockSpec((None, 1, gp * width), lambda s, d, g, c: (d, 0, g)),
    ] + _mask_specs(t, lambda s, d, g, c: d)
    args = [x, bc, bc, acol, dtcol, aneg, *_scan_masks(t)]
    st_spec = pl.BlockSpec((None, None, gp, width, ns), lambda s, d, g, c: (s, d, g, 0, 0))
    if init is not None:
        in_specs.append(st_spec)
        args.append(init)
    out_specs = [pl.BlockSpec((None, t, gp * width), lambda s, d, g, c: (d, rc(s, d, g, c), g))]
    out_shape = [jax.ShapeDtypeStruct((2, n, inner), F32)]
    if emit_state:
        out_specs.append(st_spec)
        out_shape.append(jax.ShapeDtypeStruct((nseq, 2, groups, width, ns), F32))
    return pl.pallas_call(
        functools.partial(_ssd_kernel, t=t, nc=nc, hpg=hpg, p=p, has_init=init is not None, emit_state=emit_state),
        grid=(nseq, 2, groups // gp, nc),
        in_specs=in_specs,
        out_specs=out_specs,
        out_shape=out_shape,
        scratch_shapes=[pltpu.VMEM((gp, ns, width), F32)],
        compiler_params=_params(("parallel", "parallel", "parallel", "arbitrary")),
        name="ssd_scan",
    )(*args)


def _ssd_post_kernel(y_ref, x_ref, z_ref, dsk_ref, w_ref, o_ref):
    y = (y_ref[0] + y_ref[1] + x_ref[...] * dsk_ref[...]) * _silu(z_ref[...])
    o_ref[...] = (y * lax.rsqrt(jnp.mean(y * y, axis=-1, keepdims=True) + NORM_EPS) * w_ref[...]).astype(o_ref.dtype)


def ssd_post(y, x, z, d_skip_cols, norm_w, *, row0):
    _, n, inner = y.shape
    tm = _tile(n, 128, 8)
    row = pl.BlockSpec((tm, inner), lambda i: (row0 // tm + i, 0))
    vec = pl.BlockSpec((1, inner), lambda i: (0, 0))
    return pl.pallas_call(
        _ssd_post_kernel,
        grid=(n // tm,),
        in_specs=[pl.BlockSpec((2, tm, inner), lambda i: (0, i, 0)), row, row, vec, vec],
        out_specs=pl.BlockSpec((tm, inner), lambda i: (i, 0)),
        out_shape=jax.ShapeDtypeStruct((n, inner), BF16),
        compiler_params=_params(("parallel",)),
        name="ssd_post",
    )(y, x, z, d_skip_cols.reshape(1, inner), norm_w.reshape(1, inner))


def _hyena_kernel(u_ref, f_ref, g_ref, tp_ref, tq_ref, bias_ref, xm_ref, o_ref, *, length):
    u = u_ref[...]
    a = _dot(f_ref[...], u.astype(BF16))
    a_sw = jnp.concatenate([a[length:], a[:length]], axis=0)
    y = a * tp_ref[...] + a_sw * tq_ref[...]
    conv = _dot(g_ref[...], y.astype(BF16))
    o_ref[...] = ((conv + u * bias_ref[...]) * xm_ref[...]).astype(o_ref.dtype)


def hyena_conv(u, u_at, fmat, gmat, tp, tq, bias, xm, xm_at, *, nseq, length, d, out_dtype):
    tc = _tile(d, 256, LANES)
    n2 = 2 * length
    (u_r0, u_c0), (x_r0, x_c0) = u_at, xm_at
    return pl.pallas_call(
        functools.partial(_hyena_kernel, length=length),
        grid=(d // tc, nseq),
        in_specs=[
            pl.BlockSpec((length, tc), lambda j, s: (u_r0 // length + s, u_c0 // tc + j)),
            pl.BlockSpec((n2, length), lambda j, s: (0, 0)),
            pl.BlockSpec((length, n2), lambda j, s: (0, 0)),
            pl.BlockSpec((n2, tc), lambda j, s: (0, j)),
            pl.BlockSpec((n2, tc), lambda j, s: (0, j)),
            pl.BlockSpec((1, tc), lambda j, s: (0, j)),
            pl.BlockSpec((length, tc), lambda j, s: (x_r0 // length + s, x_c0 // tc + j)),
        ],
        out_specs=pl.BlockSpec((length, tc), lambda j, s: (s, j)),
        out_shape=jax.ShapeDtypeStruct((nseq * length, d), out_dtype),
        compiler_params=_params(("parallel", "parallel")),
        name="hyena_conv",
    )(u, fmat, gmat, tp, tq, bias.reshape(1, d), xm)


def _dft_matrices(length):
    n2 = 2 * length
    r = jnp.arange(n2, dtype=jnp.int32)
    kfreq = jnp.where(r <= length, r, r - length)
    ang = ((kfreq[:, None] * r[None, :]) % n2).astype(F32) * (2.0 * math.pi / n2)
    is_sin = (r > length)[:, None]
    fwd = jnp.where(is_sin, -jnp.sin(ang), jnp.cos(ang))[:, :length]
    bwd = jnp.where(is_sin, -fwd, fwd) * (jnp.arange(length) > 0)
    wk = jnp.where((r == 0) | (r == length), 1.0, 2.0) / n2
    inv = (fwd * wk[:, None]).T
    return jnp.concatenate([fwd, bwd], axis=1), fwd.astype(BF16), inv.astype(BF16)


def _hyena_filter_spectra(length, f_w1, f_b1, f_w2, f_b2, f_w3, f_freq, d):
    hp = lax.Precision.HIGHEST
    t = jnp.linspace(0.0, 1.0, length, dtype=F32)[:, None]
    bands = (HYENA_EMB - 1) // 2
    f = jnp.linspace(1e-4, bands - 1, bands, dtype=F32)
    w = 2.0 * math.pi * jnp.arange(length, dtype=F32)[:, None] / length
    z = jnp.concatenate([t, jnp.cos(f * w), -jnp.sin(f * w)], axis=-1)
    hid = jnp.sin(f_freq * (jnp.dot(z, f_w1, precision=hp) + f_b1))
    hid = jnp.sin(f_freq * (jnp.dot(hid, f_w2, precision=hp) + f_b2))
    filt = matmul(hid, f_w3, x3=True, tk=hid.shape[1], name="hyena_filter").reshape(length, 2, 2, d)
    min_decay = math.log(HYENA_TARGET) / HYENA_DECAY_LONG
    max_decay = math.log(HYENA_TARGET) / HYENA_DECAY_SHORT
    deltas = jnp.linspace(min_decay, max_decay, d, dtype=F32)
    filt = filt * jnp.exp(-t * jnp.abs(deltas))[:, None, None, :]
    both, fmat, gmat = _dft_matrices(length)
    tps, tqs = [], []
    for o in range(2):
        h_fwd, h_bwd = filt[:, o, 0], filt[:, o, 1]
        norm = jnp.sum(jnp.abs(h_fwd), axis=0) + jnp.sum(jnp.abs(h_bwd[1:]), axis=0)
        spec = matmul(both, jnp.concatenate([h_fwd, h_bwd], axis=0), x3=True, tm=512,
                      name="hyena_filter_dft") / norm
        top, bot = spec[:length], spec[length:]
        zero = jnp.zeros((1, d), F32)
        im = jnp.concatenate([zero, bot[1:]], axis=0)
        tps.append(jnp.concatenate([top, bot[0:1], top[1:]], axis=0))
        tqs.append(jnp.concatenate([-im, im], axis=0))
    return fmat, gmat, tps, tqs


def _router_kernel(l_ref, b_ref, gates_ref, dest_ref, cnt_ref, pre_s, *, n_grp, epg, tb, chunk):
    n_exp = n_grp * epg
    rows = l_ref.shape[1]
    lg = l_ref[...] + b_ref[...]
    g = lg[n_exp:n_exp + n_grp]
    ge = jnp.exp(g - jnp.max(g, axis=0, keepdims=True))
    gp = ge / jnp.sum(ge, axis=0, keepdims=True)
    gpm = jnp.max(gp, axis=0, keepdims=True)
    gi = lax.broadcasted_iota(jnp.int32, gp.shape, 0)
    gsel = jnp.min(jnp.where(gp == gpm, gi, n_grp), axis=0, keepdims=True)
    e_in = lg[0:epg]
    for q in range(1, n_grp):
        e_in = jnp.where(gsel == q, lg[q * epg:(q + 1) * epg], e_in)
    ee = jnp.exp(e_in - jnp.max(e_in, axis=0, keepdims=True))
    ep = ee / jnp.sum(ee, axis=0, keepdims=True)
    ei = lax.broadcasted_iota(jnp.int32, ep.shape, 0)
    p1 = jnp.max(ep, axis=0, keepdims=True)
    s1 = jnp.min(jnp.where(ep == p1, ei, epg), axis=0, keepdims=True)
    ep2 = jnp.where(ei == s1, -1.0, ep)
    p2 = jnp.max(ep2, axis=0, keepdims=True)
    s2 = jnp.min(jnp.where(ep2 == p2, ei, epg), axis=0, keepdims=True)
    den = p1 + p2
    gates_ref[0:1, :] = gpm * p1 / den
    gates_ref[1:2, :] = gpm * p2 / den

    eio = lax.broadcasted_iota(jnp.int32, (n_exp, rows), 0)
    oh0 = eio == gsel * epg + s1
    oh1 = eio == gsel * epg + s2
    cnt = oh0.astype(F32) + oh1.astype(F32)
    ca = lax.broadcasted_iota(jnp.int32, (chunk, chunk), 0)
    cb = lax.broadcasted_iota(jnp.int32, (chunk, chunk), 1)
    tri = (ca < cb).astype(BF16)
    carry = jnp.zeros((n_exp, 1), F32)
    for c0 in range(0, rows, chunk):
        c = cnt[:, c0:c0 + chunk]
        pre_s[:, c0:c0 + chunk] = _dot(c.astype(BF16), tri) + carry
        carry = carry + jnp.sum(c, axis=1, keepdims=True)
    nblk = jnp.floor((carry + (tb - 1)) * (1.0 / tb))
    sa = lax.broadcasted_iota(jnp.int32, (n_exp, n_exp), 0)
    sb = lax.broadcasted_iota(jnp.int32, (n_exp, n_exp), 1)
    nblk_row = jnp.sum(jnp.where(sa == sb, nblk, 0.0), axis=0, keepdims=True)
    blk_start = jnp.sum(jnp.where(sb < sa, nblk_row, 0.0), axis=1, keepdims=True)
    slot = blk_start * tb + pre_s[...]
    dest_ref[0:1, :] = jnp.sum(jnp.where(oh0, slot, 0.0), axis=0, keepdims=True).astype(jnp.int32)
    dest_ref[1:2, :] = jnp.sum(jnp.where(oh1, slot, 0.0), axis=0, keepdims=True).astype(jnp.int32)
    cnt_ref[...] = carry.astype(jnp.int32)


def moe_route(logits_t, bias, *, n_grp, epg, tb):
    nr, rows = logits_t.shape
    n_exp = n_grp * epg
    assert tb & (tb - 1) == 0
    return pl.pallas_call(
        functools.partial(_router_kernel, n_grp=n_grp, epg=epg, tb=tb, chunk=_tile(rows, 512, LANES)),
        out_shape=[jax.ShapeDtypeStruct((MOE_TOP_K, rows), F32),
                   jax.ShapeDtypeStruct((MOE_TOP_K, rows), jnp.int32),
                   jax.ShapeDtypeStruct((n_exp, 1), jnp.int32)],
        scratch_shapes=[pltpu.VMEM((n_exp, rows), F32)],
        compiler_params=pltpu.CompilerParams(vmem_limit_bytes=VMEM_LIMIT),
        name="moe_route",
    )(logits_t, bias)


def _dispatch_kernel(dest_ref, cnt_ref, start_ref, nu_ref, h_ref, xs_ref, zero_s, sem, zsem, *,
                     rows, tm, tb, n_exp, n_blocks, slab):
    i = pl.program_id(0)

    def zero_copy(row):
        return pltpu.make_async_copy(zero_s, _slab_at(xs_ref, row, tb, slab), zsem)

    @pl.when(i == 0)
    def _():
        zero_s[...] = jnp.zeros_like(zero_s)
        for wait in (False, True):
            def go(row):
                if wait:
                    zero_copy(row).wait()
                else:
                    zero_copy(row).start()

            def last_block(e, carry):
                @pl.when(cnt_ref[e] > 0)
                def _():
                    go(start_ref[e] + ((cnt_ref[e] - 1) & ~(tb - 1)))
                return carry

            def tail_block(b, carry):
                go(b * tb)
                return carry

            lax.fori_loop(0, n_exp, last_block, 0)
            lax.fori_loop(nu_ref[0], n_blocks, tail_block, 0)

    def row_copy(r, slot):
        return pltpu.make_async_copy(_slab_at(h_ref, r, 1, slab), _slab_at(xs_ref, slot, 1, slab), sem)

    def issue(r, carry):
        row_copy(r, dest_ref[i * tm + r]).start()
        row_copy(r, dest_ref[rows + i * tm + r]).start()
        return carry

    def drain(r, carry):
        row_copy(0, 0).wait()
        row_copy(0, 0).wait()
        return carry

    lax.fori_loop(0, tm, issue, 0, unroll=8)
    lax.fori_loop(0, tm, drain, 0, unroll=8)


def _combine_kernel(dest_ref, x_ref, g2_ref, gt_ref, ys_ref, o_ref, buf, sem, *, rows, tm, slab):
    base = pl.program_id(0) * tm

    def row_copy(k, r, slot):
        return pltpu.make_async_copy(_slab_at(ys_ref, slot, 1, slab), _slab_at(buf.at[k], r, 1, slab), sem)

    def issue(r, carry):
        row_copy(0, r, dest_ref[base + r]).start()
        row_copy(1, r, dest_ref[rows + base + r]).start()
        return carry

    def drain(r, carry):
        row_copy(0, 0, 0).wait()
        row_copy(0, 0, 0).wait()
        return carry

    lax.fori_loop(0, tm, issue, 0, unroll=8)
    lax.fori_loop(0, tm, drain, 0, unroll=8)
    g0 = gt_ref[:, 0:1]
    g1 = gt_ref[:, 1:2]
    half = slab * LANES
    for s in range(slab):
        p0 = buf[0, pl.ds(s, tm, stride=slab), :]
        p1 = buf[1, pl.ds(s, tm, stride=slab), :]
        for unpack, c0 in ((_unpack_hi, s * LANES), (_unpack_lo, half + s * LANES)):
            cols = slice(c0, c0 + LANES)
            o_ref[:, cols] = x_ref[:, cols] + g2_ref[:, cols] * (g0 * unpack(p0) + g1 * unpack(p1))


def _moe_up_kernel(be_ref, nu_ref, x_ref, wg_ref, wu_ref, o_ref, *, fchunk, slab):
    used = pl.program_id(0) < nu_ref[0]

    @pl.when(used)
    def _():
        p = _load_slab(x_ref, o_ref.shape[0], slab)
        x = jnp.concatenate([_unpack_hi(p), _unpack_lo(p)], axis=1).astype(BF16)
        for f0 in range(0, o_ref.shape[1], fchunk):
            g = _dot(x, wg_ref[:, f0:f0 + fchunk].astype(BF16))
            u = _dot(x, wu_ref[:, f0:f0 + fchunk].astype(BF16))
            o_ref[:, f0:f0 + fchunk] = (_silu(g) * u).astype(o_ref.dtype)

    @pl.when(jnp.logical_not(used))
    def _():
        o_ref[...] = jnp.zeros_like(o_ref)


def _moe_down_kernel(be_ref, nu_ref, h_ref, wd_ref, o_ref, *, slab):
    used = pl.program_id(0) < nu_ref[0]

    @pl.when(used)
    def _():
        _store_slab(o_ref, _pack_bf16_pairs(_dot(h_ref[...], wd_ref[...].astype(BF16))), slab)

    @pl.when(jnp.logical_not(used))
    def _():
        o_ref[...] = jnp.zeros_like(o_ref)


def moe_experts(x_slots, block_e, n_used, w_gate, w_up, w_down, layer):
    d, ff = w_gate.shape[-2:]
    slab = d // 2 // LANES
    n_slots = x_slots.shape[0] // slab
    tb = MOE_ROW_BLOCK
    n_blocks = n_slots // tb
    blk = lambda b, be, nu: (jnp.minimum(b, nu[0] - 1), 0)
    out_blk = lambda b, be, nu: (b, 0)
    wsel = lambda b, be, nu: (layer, be[b], 0, 0)
    hmid = pl.pallas_call(
        functools.partial(_moe_up_kernel, fchunk=_tile(ff, 256, LANES), slab=slab),
        grid_spec=pltpu.PrefetchScalarGridSpec(
            num_scalar_prefetch=2, grid=(n_blocks,),
            in_specs=[pl.BlockSpec((tb * slab, LANES), blk),
                      pl.BlockSpec((None, None, d, ff), wsel),
                      pl.BlockSpec((None, None, d, ff), wsel)],
            out_specs=pl.BlockSpec((tb, ff), out_blk)),
        out_shape=jax.ShapeDtypeStruct((n_slots, ff), BF16),
        compiler_params=_params(("arbitrary",)),
        name="moe_gate_up",
    )(block_e, n_used, x_slots, w_gate, w_up)
    return pl.pallas_call(
        functools.partial(_moe_down_kernel, slab=slab),
        grid_spec=pltpu.PrefetchScalarGridSpec(
            num_scalar_prefetch=2, grid=(n_blocks,),
            in_specs=[pl.BlockSpec((tb, ff), blk),
                      pl.BlockSpec((None, None, ff, d), wsel)],
            out_specs=pl.BlockSpec((tb * slab, LANES), out_blk)),
        out_shape=jax.ShapeDtypeStruct((n_slots * slab, LANES), U32),
        compiler_params=_params(("arbitrary",)),
        name="moe_down",
    )(block_e, n_used, hmid, w_down)


def moe_router_weights(w_rg, b_rg, w_re, b_re, layer):
    w = jnp.concatenate([w_re[layer], w_rg[layer]], axis=-1).T
    b = jnp.concatenate([b_re[layer], b_rg[layer]])[:, None]
    pad = -w.shape[0] % 8
    return jnp.pad(w, ((0, pad), (0, 0))), jnp.pad(b, ((0, pad), (0, 0)))


def hier_moe_residual(x, h, logits_t, bias, g2, row_map, w_gate, w_up, w_down, layer, *, n_grp, tm):
    rows, d = x.shape
    slab = d // 2 // LANES
    nsteps = rows // tm
    n_exp = w_gate.shape[1]
    tb = MOE_ROW_BLOCK
    n_pair = rows * MOE_TOP_K
    n_blocks = -(-(n_pair + n_exp * (tb - 1)) // tb)
    gates_t, dest_t, counts = moe_route(logits_t, bias, n_grp=n_grp, epg=n_exp // n_grp, tb=tb)
    counts = counts[:, 0]
    blocks = (counts + tb - 1) // tb
    blk_end = jnp.cumsum(blocks)
    starts = ((blk_end - blocks) * tb).astype(jnp.int32)
    n_used = blk_end[-1].astype(jnp.int32)
    block_e = jnp.minimum(jnp.searchsorted(blk_end, jnp.arange(n_blocks), side="right"), n_exp - 1)
    block_e = jnp.where(jnp.arange(n_blocks) < n_used, block_e, block_e[n_used - 1]).astype(jnp.int32)
    dest = dest_t.reshape(-1)

    x_slots = pl.pallas_call(
        functools.partial(_dispatch_kernel, rows=rows, tm=tm, tb=tb, n_exp=n_exp, n_blocks=n_blocks, slab=slab),
        grid_spec=pltpu.PrefetchScalarGridSpec(
            num_scalar_prefetch=4, grid=(nsteps,),
            in_specs=[pl.BlockSpec((tm * slab, LANES), lambda i, *_: (i, 0))],
            out_specs=pl.BlockSpec(memory_space=pl.ANY),
            scratch_shapes=[pltpu.VMEM((tb * slab, LANES), U32), pltpu.SemaphoreType.DMA,
                            pltpu.SemaphoreType.DMA]),
        out_shape=jax.ShapeDtypeStruct((n_blocks * tb * slab, LANES), U32),
        compiler_params=_params(("arbitrary",)),
        name="moe_dispatch",
    )(dest, counts, starts, n_used.reshape(1), h)
    y_slots = moe_experts(x_slots, block_e, n_used.reshape(1), w_gate, w_up, w_down, layer)
    return pl.pallas_call(
        functools.partial(_combine_kernel, rows=rows, tm=tm, slab=slab),
        grid_spec=pltpu.PrefetchScalarGridSpec(
            num_scalar_prefetch=1, grid=(nsteps,),
            in_specs=[pl.BlockSpec((tm, d), lambda i, *_: (i, 0)),
                      pl.BlockSpec((None, 1, d), lambda i, *_: (row_map(i * tm), 0, 0)),
                      pl.BlockSpec((tm, MOE_TOP_K), lambda i, *_: (i, 0)),
                      pl.BlockSpec(memory_space=pl.ANY)],
            out_specs=pl.BlockSpec((tm, d), lambda i, *_: (i, 0)),
            scratch_shapes=[pltpu.VMEM((MOE_TOP_K, tm * slab, LANES), U32), pltpu.SemaphoreType.DMA]),
        out_shape=jax.ShapeDtypeStruct((rows, d), F32),
        compiler_params=_params(("arbitrary",)),
        name="moe_combine",
    )(dest, x, g2, gates_t.T, y_slots)


def _rope_lane_tables(length, hd, n_rows):
    grid_rows = length // ROPE_GRID_W
    row = jnp.broadcast_to(jnp.arange(grid_rows, dtype=F32)[:, None], (grid_rows, ROPE_GRID_W)).reshape(length)
    col = jnp.broadcast_to(jnp.arange(ROPE_GRID_W, dtype=F32)[None, :], (grid_rows, ROPE_GRID_W)).reshape(length)
    axis_dim = hd // 2
    inv_freq = ROPE_BASE ** (-jnp.arange(0, axis_dim, 2, dtype=F32) / axis_dim)
    ang_r = row[:, None] * inv_freq
    ang_c = col[:, None] * inv_freq
    cos = jnp.concatenate([jnp.cos(ang_r)] * 2 + [jnp.cos(ang_c)] * 2, axis=1)
    sin = jnp.concatenate([-jnp.sin(ang_r), jnp.sin(ang_r), -jnp.sin(ang_c), jnp.sin(ang_c)], axis=1)
    reps = (n_rows // length, 2 * LANES // (2 * hd))
    return jnp.tile(cos, reps), jnp.tile(sin, reps)


def kernel(x_prompt, x_sample, cache_attn_k, cache_attn_v, state_mlstm_C, state_mlstm_n, state_mlstm_m, state_ssd, c, c_ctx, norm_mix, norm_ffn, w_mod, b_mod, attn_w_in, attn_q_norm, attn_k_norm, attn_lambda, attn_sub_norm, attn_w_out, mlstm_w_in, mlstm_conv_w, mlstm_conv_b, mlstm_w_gate, mlstm_b_gate, mlstm_head_norm, mlstm_w_out, ssd_w_in, ssd_conv_w, ssd_conv_b, ssd_dt_bias, ssd_a_log, ssd_d_skip, ssd_norm, ssd_w_out, hyena_w_in, hyena_conv_w, hyena_conv_b, hyena_f_w1, hyena_f_b1, hyena_f_w2, hyena_f_b2, hyena_f_w3, hyena_f_freq, hyena_skip_bias, hyena_w_out, moe_w_group, moe_b_group, moe_w_expert, moe_b_expert, moe_w_gate, moe_w_up, moe_w_down):
    batch, seq, dm = x_prompt.shape
    dec_batch, dec_seq, _ = x_sample.shape
    depth = norm_mix.shape[0]
    n_ctx = batch * seq
    n_lat = dec_batch * dec_seq
    rows = n_ctx + n_lat
    past = cache_attn_k.shape[2]
    a_heads, a_hd = cache_attn_k.shape[3], cache_attn_k.shape[5]
    m_heads, m_dk, m_dv = state_mlstm_C.shape[3:]
    s_heads, s_p, s_n = state_ssd.shape[3:]
    s_inner = s_heads * s_p
    s_groups = (ssd_conv_w.shape[-1] - s_inner) // (2 * s_n)
    s_hpg = s_heads // s_groups

    def row_map(r):
        return jnp.where(r < n_ctx, 0, 1 + (r - n_ctx) // dec_seq)

    row_unit = math.gcd(n_ctx, dec_seq)
    tm_norm = _tile(row_unit, 256, 8)
    tm_res = _tile(row_unit, 1024, 8)
    seq_lcm = math.lcm(seq, dec_seq)
    pass_unit = math.gcd(n_ctx, n_lat)
    assert pass_unit % seq_lcm == 0 and seq & (seq - 1) == 0 and dec_seq & (dec_seq - 1) == 0
    tm_seq = seq_lcm * max(k for k in range(1, max(2, 2048 // seq_lcm + 1)) if pass_unit % (seq_lcm * k) == 0)
    seq_of_tile = lambda i: jnp.where(i * tm_seq < n_ctx, seq, dec_seq)
    parts = ((0, batch, seq), (n_ctx, dec_batch, dec_seq))
    ones = lambda n: jnp.ones((n,), F32)

    tn_of = lambda n: _tile(n, 512, LANES)

    def conv_extras(taps, bias, scale, col0):
        tn = tn_of(scale.shape[0])
        assert col0 % tn == 0
        return [(taps, (3, tn), lambda i, j, c=col0 // tn: (0, j + c)), _row_vec(bias, tn, col0), _row_vec(scale, tn)]

    n_cond = 1 + dec_batch
    cond = jnp.concatenate([c_ctx[None, :], c], axis=0)
    cond = jnp.pad(jax.nn.silu(cond), ((0, -n_cond % 8), (0, 0))).astype(BF16)

    x = jnp.concatenate([x_prompt.reshape(n_ctx, dm), x_sample.reshape(n_lat, dm)], axis=0)
    outs = {}
    for i in range(depth):
        kind, j = i % 4, i // 4
        mod = matmul(cond, w_mod, w_idx=i, tm=8, tn=1024, tk=2048, extras=[_row_vec(b_mod[i], 1024)],
                     epi=_epi_bias, name="adaln_mod")
        sh1, sc1, g1, sh2, sc2, g2 = [mod[:, None, q * dm:(q + 1) * dm] for q in range(6)]
        h1 = norm_mod(x, norm_mix[i], sc1, sh1, row_map, tm_norm)

        if kind == 0:
            lam_init = 0.8 - 0.6 * math.exp(-0.3 * i)
            tn = tn_of(dm)
            cos, sin = _rope_lane_tables(dec_seq, a_hd, tm_seq)
            tabs = [(cos, (tm_seq, LANES), lambda i_, j_: (0, 0)), (sin, (tm_seq, LANES), lambda i_, j_: (0, 0))]
            rep = dm // a_hd
            q = matmul(h1, attn_w_in, w_idx=j, n=dm, tm=tm_seq, tn=tn, out_dtypes=(BF16,),
                       extras=[_row_vec(jnp.tile(attn_q_norm[j], rep), tn)] + tabs,
                       epi=_make_epi_qk(tm_seq, tn, a_hd, n_ctx, 1), name="attn_in_q")
            k, k_plain = matmul(h1, attn_w_in, w_idx=j, n0=dm, n=dm, tm=tm_seq, tn=tn, out_dtypes=(BF16, F32),
                                extras=[_row_vec(jnp.tile(attn_k_norm[j], rep), tn)] + tabs,
                                epi=_make_epi_qk(tm_seq, tn, a_hd, n_ctx, 2), name="attn_in_k")
            v, v_f32 = matmul(h1, attn_w_in, w_idx=j, n0=2 * dm, n=dm, tm=tm_seq, tn=tn, out_dtypes=(BF16, F32),
                              epi=_epi_two, name="attn_in_v")
            outs["k"] = k_plain[:n_ctx].reshape(batch, 1, seq, a_heads, 2, a_hd)
            outs["v"] = v_f32[:n_ctx].reshape(batch, 1, seq, a_heads, 2 * a_hd)
            lv = attn_lambda[j]
            lam = jnp.exp(jnp.sum(lv[0] * lv[1])) - jnp.exp(jnp.sum(lv[2] * lv[3])) + lam_init
            cache = (cache_attn_k[:, j].reshape(dec_batch * past, dm), cache_attn_v[:, j].reshape(dec_batch * past, dm))
            o_parts = [diff_attention(q, k, v, cache if row0 else None, lam, attn_sub_norm[j], row0=row0, nseq=nseq,
                                      length=length, heads=a_heads, post_scale=1.0 - lam_init)
                       for row0, nseq, length in parts]
            mix_in, w_out = jnp.concatenate(o_parts, axis=0), attn_w_out

        elif kind == 1:
            qk_w = 2 * m_heads * m_dk
            v_w = m_heads * m_dv
            k_scale = jnp.concatenate([ones(qk_w // 2), jnp.full((qk_w // 2,), m_dk ** -0.5, F32)])
            qk = matmul(h1, mlstm_w_in, w_idx=j, n=qk_w, tm=tm_seq, out_dtypes=(BF16,),
                        extras=conv_extras(mlstm_conv_w[j], mlstm_conv_b[j], k_scale, 0),
                        epi=_make_epi_conv(tm_seq, seq_of_tile, True), name="mlstm_in_qk")
            v = matmul(h1, mlstm_w_in, w_idx=j, n0=qk_w, n=v_w, out_dtypes=(BF16,), name="mlstm_in_v")
            o_gate = matmul(h1, mlstm_w_in, w_idx=j, n0=qk_w + v_w, n=v_w, epi=_epi_sigmoid, name="mlstm_in_o")
            g = matmul(h1, mlstm_w_gate, w_idx=j, x3=True, tm=512, extras=[_row_vec(mlstm_b_gate[j], 4 * m_heads)],
                       epi=_epi_bias, name="mlstm_gate")
            g = g.reshape(rows, 2, 2, m_heads)
            gates = jnp.stack([g[:, :, 0], jax.nn.log_sigmoid(g[:, :, 1])], axis=-1)
            gcol = gates.transpose(1, 2, 0, 3)
            init = (state_mlstm_C[:, j], state_mlstm_n[:, j][:, :, :, None, :], state_mlstm_m[:, j][:, :, :, None, None])
            y_parts = []
            for row0, nseq, length in parts:
                t = _tile(length, MLSTM_CHUNK, 8)
                n = nseq * length
                gc = gcol[:, :, row0:row0 + n]
                grow = gc.reshape(2, m_heads, n // t, t, 2).transpose(0, 1, 2, 4, 3)
                res = mlstm_scan(qk, v, gc, grow, init if row0 else None, row0=row0, nseq=nseq, length=length,
                                 heads=m_heads, emit_state=not row0)
                if not row0:
                    outs["C"] = res[1][:, None]
                    outs["n"] = res[2][:, None, :, :, 0]
                    outs["m"] = res[3][:, None, :, :, 0, 0]
                y_parts.append(mlstm_post(res[0], o_gate, mlstm_head_norm[j], row0=row0, heads=m_heads))
            mix_in, w_out = jnp.concatenate(y_parts, axis=0), mlstm_w_out

        elif kind == 2:
            gn = s_groups * s_n
            conv = _make_epi_conv(tm_seq, seq_of_tile, True)
            z = matmul(h1, ssd_w_in, w_idx=j, n=s_inner, name="ssd_in_z")
            xs = matmul(h1, ssd_w_in, w_idx=j, n0=s_inner, n=s_inner, tm=tm_seq,
                        extras=conv_extras(ssd_conv_w[j], ssd_conv_b[j], ones(s_inner), 0),
                        epi=conv, name="ssd_in_x")
            bc = matmul(h1, ssd_w_in, w_idx=j, n0=2 * s_inner, n=2 * gn, tm=tm_seq, out_dtypes=(BF16,),
                        extras=conv_extras(ssd_conv_w[j], ssd_conv_b[j], ones(2 * gn), s_inner),
                        epi=conv, name="ssd_in_bc")
            dt_raw = matmul(h1, ssd_w_in, w_idx=j, n0=2 * s_inner + 2 * gn, n=2 * s_heads, x3=True, tm=512, name="ssd_dt")
            dt = jax.nn.softplus(dt_raw.reshape(rows, 2, s_heads) + ssd_dt_bias[j])
            a_neg = -jnp.exp(ssd_a_log[j])
            a = dt * a_neg
            to_col = lambda y: y.reshape(rows, 2, s_groups, s_hpg).transpose(1, 2, 0, 3)
            acol, dtcol = to_col(a), to_col(dt)
            aneg_lanes = jnp.repeat(a_neg, s_p, axis=1)[:, None, :]
            init = state_ssd[:, j].reshape(dec_batch, 2, s_groups, s_hpg * s_p, s_n)
            d_skip_cols = jnp.repeat(ssd_d_skip[j], s_p)
            y_parts = []
            for row0, nseq, length in parts:
                n = nseq * length
                res = ssd_scan(xs, bc, acol[:, :, row0:row0 + n], dtcol[:, :, row0:row0 + n], aneg_lanes,
                               init if row0 else None, row0=row0,
                               nseq=nseq, length=length, groups=s_groups, hpg=s_hpg, emit_state=not row0)
                if not row0:
                    outs["ssd"] = res[1].reshape(batch, 1, 2, s_heads, s_p, s_n)
                y_parts.append(ssd_post(res[0], xs, z, d_skip_cols, ssd_norm[j], row0=row0))
            mix_in, w_out = jnp.concatenate(y_parts, axis=0), ssd_w_out

        else:
            proj = matmul(h1, hyena_w_in, w_idx=j, tm=tm_seq,
                          extras=conv_extras(hyena_conv_w[j], hyena_conv_b[j], ones(3 * dm), 0),
                          epi=_make_epi_conv(tm_seq, seq_of_tile, False), name="hyena_in")
            y_parts = []
            for row0, nseq, length in parts:
                fmat, gmat, tps, tqs = _hyena_filter_spectra(length, hyena_f_w1[j], hyena_f_b1[j], hyena_f_w2[j],
                                                             hyena_f_b2[j], hyena_f_w3[j], hyena_f_freq[j], dm)
                y = hyena_conv(proj, (row0, 0), fmat, gmat, tps[0], tqs[0], hyena_skip_bias[j][0], proj, (row0, dm),
                               nseq=nseq, length=length, d=dm, out_dtype=F32)
                y = hyena_conv(y, (0, 0), fmat, gmat, tps[1], tqs[1], hyena_skip_bias[j][1], proj, (row0, 2 * dm),
                               nseq=nseq, length=length, d=dm, out_dtype=BF16)
                y_parts.append(y)
            mix_in, w_out = jnp.concatenate(y_parts, axis=0), hyena_w_out

        tn = _tile(dm, 1024, LANES)
        x = matmul(mix_in, w_out[j].astype(BF16), tm=tm_res, tn=tn,
                   extras=[(x, (tm_res, tn), lambda i_, j_: (i_, j_)),
                           (g1, (None, 1, tn), lambda i_, j_: (row_map(i_ * tm_res), 0, j_))],
                   epi=_epi_resid, name="mixer_out")
        w_rt, b_rt = moe_router_weights(moe_w_group, moe_b_group, moe_w_expert, moe_b_expert, i)
        h2, logits_t = norm_mod_router(x, norm_ffn[i], sc2, sh2, w_rt, row_map, tm_norm)
        x = hier_moe_residual(x, h2, logits_t, b_rt, g2, row_map, moe_w_gate, moe_w_up, moe_w_down, i,
                              n_grp=moe_w_group.shape[-1], tm=tm_norm)

    y_prompt = x[:n_ctx].reshape(batch, seq, dm)
    y_sample = x[n_ctx:].reshape(dec_batch, dec_seq, dm)
    return (y_prompt, y_sample, outs["k"], outs["v"], outs["C"], outs["n"], outs["m"], outs["ssd"])
```

```python
import functools
import math

import jax
import jax.numpy as jnp
from jax import lax
from jax.experimental import pallas as pl
from jax.experimental.pallas import tpu as pltpu

F32 = jnp.float32
BF16 = jnp.bfloat16
U32 = jnp.uint32
LANES = 128
NORM_EPS = 1e-6
ROPE_GRID_W = 64
ROPE_BASE = 10000.0
HYENA_EMB = 33
HYENA_DECAY_SHORT = 0.3
HYENA_DECAY_LONG = 1.5
HYENA_TARGET = 1e-2
MOE_TOP_K = 2
MOE_ROW_BLOCK = 512
MLSTM_CHUNK = 256
SSD_CHUNK = 256
VMEM_LIMIT = 56 * 1024 * 1024

_NT = (((1,), (1,)), ((), ()))
_TN = (((0,), (0,)), ((), ()))


def _tile(n, pref, align):
    if n <= pref:
        return n
    t = (pref // align) * align
    while t >= align:
        if n % t == 0:
            return t
        t -= align
    return n


def _params(sem):
    return pltpu.CompilerParams(dimension_semantics=sem, vmem_limit_bytes=VMEM_LIMIT)


def _dot(a, b):
    return jnp.dot(a, b, preferred_element_type=F32)


def _split_bf16(a):
    hi = a.astype(BF16)
    lo = (a - hi.astype(F32)).astype(BF16)
    return hi, lo


def _silu(x):
    return x / (1.0 + jnp.exp(-x))


def _mm_kernel(*refs, nk, x3, n_extra, n_out, epi):
    x_ref, w_ref = refs[:2]
    extra = refs[2:2 + n_extra]
    o_refs = refs[2 + n_extra:2 + n_extra + n_out]

    if x3:
        xh, xl = _split_bf16(x_ref[...].astype(F32))
        wh, wl = _split_bf16(w_ref[...].astype(F32))
        part = _dot(xh, wh) + _dot(xh, wl) + _dot(xl, wh)
    else:
        part = _dot(x_ref[...].astype(BF16), w_ref[...].astype(BF16))

    def finish(r):
        for o_ref, val in zip(o_refs, epi(r, extra, pl.program_id(0))):
            o_ref[...] = val.astype(o_ref.dtype)

    if nk == 1:
        finish(part)
        return
    acc_ref = refs[-1]
    k = pl.program_id(2)

    @pl.when(k == 0)
    def _():
        acc_ref[...] = part

    @pl.when(k > 0)
    def _():
        acc_ref[...] += part

    @pl.when(k == nk - 1)
    def _():
        finish(acc_ref[...])


def _epi_plain(r, extra, i):
    return (r,)


def _epi_bias(r, extra, i):
    return (r + extra[0][...],)


def _epi_resid(r, extra, i):
    return (extra[0][...] + extra[1][...] * r,)


def _epi_sigmoid(r, extra, i):
    return (1.0 / (1.0 + jnp.exp(-r)),)


def _epi_two(r, extra, i):
    return (r, r)


def matmul(x, w, *, w_idx=None, n0=0, n=None, tm=2048, tn=512, tk=2048, out_dtypes=(F32,), x3=False,
           extras=(), epi=_epi_plain, name="matmul"):
    m, kdim = x.shape
    n = w.shape[-1] if n is None else n
    tm = _tile(m, tm, 8)
    tk = _tile(kdim, tk, LANES)
    tn = _tile(n, tn, LANES)
    assert n0 % tn == 0 and m % tm == 0 and kdim % tk == 0 and n % tn == 0
    j0 = n0 // tn
    nk = kdim // tk
    if w.ndim == 3:
        w_spec = pl.BlockSpec((None, tk, tn), lambda i, j, k: (w_idx, k, j + j0))
    else:
        w_spec = pl.BlockSpec((tk, tn), lambda i, j, k: (k, j + j0))
    in_specs = [pl.BlockSpec((tm, tk), lambda i, j, k: (i, k)), w_spec]
    args = [x, w]
    for arr, shape, imap in extras:
        in_specs.append(pl.BlockSpec(shape, lambda i, j, k, imap=imap: imap(i, j)))
        args.append(arr)
    outs = pl.pallas_call(
        functools.partial(_mm_kernel, nk=nk, x3=x3, n_extra=len(extras), n_out=len(out_dtypes), epi=epi),
        grid=(m // tm, n // tn, nk),
        in_specs=in_specs,
        out_specs=[pl.BlockSpec((tm, tn), lambda i, j, k: (i, j)) for _ in out_dtypes],
        out_shape=[jax.ShapeDtypeStruct((m, n), dt) for dt in out_dtypes],
        scratch_shapes=[pltpu.VMEM((tm, tn), F32)] if nk > 1 else [],
        compiler_params=_params(("parallel", "parallel", "arbitrary")),
        name=name,
    )(*args)
    return outs[0] if len(out_dtypes) == 1 else outs


def _row_vec(v, tn, col0=0):
    return (v.reshape(1, -1), (1, tn), lambda i, j, c=col0 // tn: (0, j + c))


def _make_epi_conv(tm, seq_of_tile, act):
    def epi(r, extra, i):
        taps = extra[0][...]
        length = seq_of_tile(i)
        pos = lax.broadcasted_iota(jnp.int32, (tm, 1), 0) & (length - 1)
        prev = jnp.where(pos == 0, 0.0, pltpu.roll(r, 1, 0))
        nxt = jnp.where(pos == length - 1, 0.0, pltpu.roll(r, tm - 1, 0))
        y = prev * taps[0:1] + r * taps[1:2] + nxt * taps[2:3] + extra[1][...]
        if act:
            y = _silu(y)
        return (y * extra[2][...],)
    return epi


def _make_epi_qk(tm, tn, hd, n_ctx, n_out):
    def epi(r, extra, i):
        ga = lax.broadcasted_iota(jnp.int32, (LANES, LANES), 0) // hd
        gb = lax.broadcasted_iota(jnp.int32, (LANES, LANES), 1) // hd
        avg = jnp.where(ga == gb, 1.0 / hd, 0.0).astype(BF16)
        lane = lax.broadcasted_iota(jnp.int32, (1, LANES), 1)
        first_half = (lane & (hd // 2 - 1)) < hd // 4
        rope_on = i * tm >= n_ctx
        cos = extra[1][...]
        sin = extra[2][...]
        rot, plain = [], []
        for c0 in range(0, tn, LANES):
            xc = r[:, c0:c0 + LANES]
            hi, lo = _split_bf16(xc * xc)
            y = xc * lax.rsqrt(_dot(hi, avg) + _dot(lo, avg) + NORM_EPS) * extra[0][:, c0:c0 + LANES]
            partner = jnp.where(first_half, pltpu.roll(y, LANES - hd // 4, 1), pltpu.roll(y, hd // 4, 1))
            rot.append(jnp.where(rope_on, y * cos + partner * sin, y))
            plain.append(y)
        outs = (jnp.concatenate(rot, axis=1),)
        if n_out == 2:
            outs += (jnp.concatenate(plain, axis=1),)
        return outs
    return epi


def _norm_mod_kernel(x_ref, w_ref, sc_ref, sh_ref, o_ref):
    x = x_ref[...]
    y = x * lax.rsqrt(jnp.mean(x * x, axis=-1, keepdims=True) + NORM_EPS) * w_ref[...]
    o_ref[...] = (y * (1.0 + sc_ref[...]) + sh_ref[...]).astype(o_ref.dtype)


def norm_mod(x, w, scale, shift, row_map, tm):
    rows, d = x.shape
    row = pl.BlockSpec((tm, d), lambda i: (i, 0))
    mod = pl.BlockSpec((None, 1, d), lambda i: (row_map(i * tm), 0, 0))
    return pl.pallas_call(
        _norm_mod_kernel,
        grid=(rows // tm,),
        in_specs=[row, pl.BlockSpec((1, d), lambda i: (0, 0)), mod, mod],
        out_specs=row,
        out_shape=jax.ShapeDtypeStruct((rows, d), BF16),
        compiler_params=_params(("parallel",)),
        name="norm_mod",
    )(x, w.reshape(1, d), scale, shift)


def _store_slab(ref, val, slab):
    n = val.shape[0]
    for s in range(slab):
        ref[pl.ds(s, n, stride=slab), :] = val[:, s * LANES:(s + 1) * LANES]


def _load_slab(ref, n, slab):
    return jnp.concatenate([ref[pl.ds(s, n, stride=slab), :] for s in range(slab)], axis=1)


def _pack_bf16_pairs(x):
    c = x.shape[1] // 2
    hi = lax.bitcast_convert_type(x[:, :c].astype(BF16).astype(F32), U32)
    lo = lax.bitcast_convert_type(x[:, c:].astype(BF16).astype(F32), U32)
    return hi | (lo >> 16)


def _unpack_hi(p):
    return lax.bitcast_convert_type(p & jnp.uint32(0xFFFF0000), F32)


def _unpack_lo(p):
    return lax.bitcast_convert_type(p << 16, F32)


def _slab_at(ref, row, n, slab):
    return ref.at[pl.ds(pl.multiple_of(row * slab, slab), n * slab)]


def _norm_router_kernel(x_ref, w_ref, sc_ref, sh_ref, wr_ref, h_ref, lt_ref, *, slab):
    x = x_ref[...]
    y = x * lax.rsqrt(jnp.mean(x * x, axis=-1, keepdims=True) + NORM_EPS) * w_ref[...]
    h = y * (1.0 + sc_ref[...]) + sh_ref[...]
    _store_slab(h_ref, _pack_bf16_pairs(h), slab)
    hh, hl = _split_bf16(h)
    wh, wl = _split_bf16(wr_ref[...])
    nt = lambda a, b: lax.dot_general(a, b, _NT, preferred_element_type=F32)
    lt_ref[...] = nt(wh, hh) + nt(wh, hl) + nt(wl, hh)


def norm_mod_router(x, w, scale, shift, w_router_t, row_map, tm):
    rows, d = x.shape
    nr = w_router_t.shape[0]
    slab = d // 2 // LANES
    row = pl.BlockSpec((tm, d), lambda i: (i, 0))
    mod = pl.BlockSpec((None, 1, d), lambda i: (row_map(i * tm), 0, 0))
    return pl.pallas_call(
        functools.partial(_norm_router_kernel, slab=slab),
        grid=(rows // tm,),
        in_specs=[row, pl.BlockSpec((1, d), lambda i: (0, 0)), mod, mod, pl.BlockSpec((nr, d), lambda i: (0, 0))],
        out_specs=[pl.BlockSpec((tm * slab, LANES), lambda i: (i, 0)), pl.BlockSpec((nr, tm), lambda i: (0, i))],
        out_shape=[jax.ShapeDtypeStruct((rows * slab, LANES), U32), jax.ShapeDtypeStruct((nr, rows), F32)],
        compiler_params=_params(("parallel",)),
        name="norm_mod_router",
    )(x, w.reshape(1, d), scale, shift, w_router_t)


def _attn_kernel(lam_ref, q_ref, k_ref, v_ref, *rest, hd2, scale, post_scale, has_cache):
    if has_cache:
        kc_ref, vc_ref, sn_ref, o_ref = rest
    else:
        sn_ref, o_ref = rest
    nt = lambda a, b: lax.dot_general(a, b, _NT, preferred_element_type=F32)
    lane = lax.broadcasted_iota(jnp.int32, (q_ref.shape[0], hd2), 1)
    for c0 in range(0, q_ref.shape[1], hd2):
        cols = slice(c0, c0 + hd2)
        q = q_ref[:, cols].astype(F32) * scale
        k = k_ref[:, cols]
        if has_cache:
            kc = kc_ref[:, cols].astype(BF16)

        def exps(qm):
            s = nt(qm, k)
            m = jnp.max(s, axis=-1, keepdims=True)
            if not has_cache:
                e = jnp.exp(s - m)
                return e, None, 1.0 / jnp.sum(e, axis=-1, keepdims=True)
            sc = nt(qm, kc)
            m = jnp.maximum(m, jnp.max(sc, axis=-1, keepdims=True))
            e = jnp.exp(s - m)
            ec = jnp.exp(sc - m)
            return e, ec, 1.0 / (jnp.sum(e, axis=-1, keepdims=True) + jnp.sum(ec, axis=-1, keepdims=True))

        e1, e1c, r1 = exps(jnp.where(lane < hd2 // 2, q, 0.0).astype(BF16))
        e2, e2c, r2 = exps(jnp.where(lane >= hd2 // 2, q, 0.0).astype(BF16))
        r2 = lam_ref[0] * r2
        o = _dot((e1 * r1 - e2 * r2).astype(BF16), v_ref[:, cols])
        if has_cache:
            o = o + _dot((e1c * r1 - e2c * r2).astype(BF16), vc_ref[:, cols].astype(BF16))
        o = o * lax.rsqrt(jnp.mean(o * o, axis=-1, keepdims=True) + NORM_EPS) * sn_ref[...] * post_scale
        o_ref[:, cols] = o.astype(o_ref.dtype)


def diff_attention(q, k, v, cache, lam, sub_norm, *, row0, nseq, length, heads, post_scale):
    hd2 = q.shape[1] // heads
    hp = max(p for p in (1, 2, 4) if heads % p == 0)
    wid = hp * hd2
    tq = _tile(length, 256, 8)
    nq = length // tq
    in_specs = [
        pl.BlockSpec(memory_space=pltpu.SMEM),
        pl.BlockSpec((tq, wid), lambda s, h, i: (row0 // tq + s * nq + i, h)),
        pl.BlockSpec((length, wid), lambda s, h, i: (row0 // length + s, h)),
        pl.BlockSpec((length, wid), lambda s, h, i: (row0 // length + s, h)),
    ]
    args = [lam.reshape(1), q, k, v]
    if cache is not None:
        past = cache[0].shape[0] // nseq
        in_specs += [pl.BlockSpec((past, wid), lambda s, h, i: (s, h))] * 2
        args += list(cache)
    in_specs.append(pl.BlockSpec((1, hd2), lambda s, h, i: (0, 0)))
    args.append(sub_norm.reshape(1, hd2))
    return pl.pallas_call(
        functools.partial(_attn_kernel, hd2=hd2, scale=(hd2 // 2) ** -0.5, post_scale=post_scale,
                          has_cache=cache is not None),
        grid=(nseq, heads // hp, nq),
        in_specs=in_specs,
        out_specs=pl.BlockSpec((tq, wid), lambda s, h, i: (s * nq + i, h)),
        out_shape=jax.ShapeDtypeStruct((nseq * length, q.shape[1]), BF16),
        compiler_params=_params(("parallel", "parallel", "parallel")),
        name="diff_attention",
    )(*args)


def _causal_masks(t, d):
    sign = jnp.where(d == 0, 1, -1)
    r = lax.broadcasted_iota(jnp.int32, (t, t), 0)
    s = lax.broadcasted_iota(jnp.int32, (t, t), 1)
    diff = (s - r) * sign
    return diff <= 0, diff >= 0


def _scan_masks(t):
    r = jnp.arange(t)[:, None]
    s = jnp.arange(t)[None, :]
    vis = jnp.stack([s <= r, s >= r])
    return vis.astype(BF16), vis.astype(F32), jnp.where(vis, 0.0, -jnp.inf).astype(F32)


def _mask_specs(t, dir_of):
    return [pl.BlockSpec((None, t, t), lambda *g: (dir_of(*g), 0, 0)),
            pl.BlockSpec((None, t, t), lambda *g: (1 - dir_of(*g), 0, 0)),
            pl.BlockSpec((None, t, t), lambda *g: (dir_of(*g), 0, 0))]


def _split3(a):
    h = a.astype(BF16)
    r1 = a - h.astype(F32)
    m = r1.astype(BF16)
    return h, m, (r1 - m.astype(F32)).astype(BF16)


def _masked_cumsum(vis, a):
    h, m, l = _split3(a)
    return _dot(vis, h) + _dot(vis, m) + _dot(vis, l)


def _chunk_row(s, d, c, nc):
    return s * nc + jnp.where(d == 0, c, nc - 1 - c)


def _mlstm_kernel(*refs, t, nc, has_init, emit_state):
    refs = list(refs)
    q_ref, k_ref, v_ref, gcol_ref, grow_ref = refs[:5]
    pos = 5
    if has_init:
        c0_ref, n0_ref, m0_ref = refs[pos:pos + 3]
        pos += 3
    h_ref = refs[pos]
    pos += 1
    if emit_state:
        co_ref, no_ref, mo_ref = refs[pos:pos + 3]
        pos += 3
    c_s, n_s, m_s = refs[pos:pos + 3]
    d = pl.program_id(1)
    c = pl.program_id(3)

    @pl.when(c == 0)
    def _():
        if has_init:
            c_s[...] = c0_ref[...]
            n_s[...] = n0_ref[...]
            m_s[...] = m0_ref[...]
        else:
            c_s[...] = jnp.zeros_like(c_s)
            n_s[...] = jnp.zeros_like(n_s)
            m_s[...] = jnp.zeros_like(m_s)

    dk = c_s.shape[1]
    dv = c_s.shape[2]
    for hh in range(c_s.shape[0]):
        q = q_ref[:, hh * dk:(hh + 1) * dk]
        k = k_ref[:, hh * dk:(hh + 1) * dk]
        v = v_ref[:, hh * dv:(hh + 1) * dv]
        i_col = gcol_ref[hh, :, 0:1]
        f_col = gcol_ref[hh, :, 1:2]
        i_row = grow_ref[hh, 0:1, :]
        f_row = grow_ref[hh, 1:2, :]
        causal, causal_t = _causal_masks(t, d)
        b_col = jnp.sum(jnp.where(causal, f_row, 0.0), axis=1, keepdims=True)
        b_row = jnp.sum(jnp.where(causal_t, f_col, 0.0), axis=0, keepdims=True)
        dlog = jnp.where(causal, b_col - b_row + i_row, -jnp.inf)
        m_prev = m_s[hh]
        inter = b_col + m_prev
        mt = jnp.maximum(inter, jnp.max(dlog, axis=1, keepdims=True))
        w_intra = jnp.exp(dlog - mt)
        w_inter = jnp.exp(inter - mt)
        cm = c_s[hh]
        nv = n_s[hh]
        sm = lax.dot_general(q, k, _NT, preferred_element_type=F32) * w_intra
        num = _dot(sm.astype(BF16), v) + w_inter * _dot(q, cm.astype(BF16))
        den = jnp.sum(sm, axis=1, keepdims=True) + w_inter * jnp.sum(q.astype(F32) * nv, axis=1, keepdims=True)
        h_ref[:, hh * dv:(hh + 1) * dv] = num / jnp.maximum(jnp.abs(den), jnp.exp(-mt))

        rowid = lax.broadcasted_iota(jnp.int32, (t, 1), 0)
        end_row = jnp.where(d == 0, t - 1, 0)
        m_new = jnp.sum(jnp.where(rowid == end_row, mt, 0.0), axis=0, keepdims=True)
        b_last = jnp.sum(f_row, axis=1, keepdims=True)
        kw = k.astype(F32) * jnp.exp(b_last - b_col + i_col - m_new)
        decay = jnp.exp(b_last + m_prev - m_new)
        c_new = decay * cm + lax.dot_general(kw.astype(BF16), v, _TN, preferred_element_type=F32)
        n_new = decay * nv + jnp.sum(kw, axis=0, keepdims=True)
        c_s[hh] = c_new
        n_s[hh] = n_new
        m_s[hh] = m_new
        if emit_state:
            @pl.when(c == nc - 1)
            def _(hh=hh, c_new=c_new, n_new=n_new, m_new=m_new):
                co_ref[hh] = c_new
                no_ref[hh] = n_new
                mo_ref[hh] = m_new


def mlstm_scan(qk, v, gcol, grow, init, *, row0, nseq, length, heads, emit_state):
    n = nseq * length
    dk = qk.shape[1] // (2 * heads)
    dv = v.shape[1] // heads
    t = _tile(length, MLSTM_CHUNK, 8)
    nc = length // t
    hp = 2 if heads % 2 == 0 else 1
    rc = lambda s, d, h, c: _chunk_row(s, d, c, nc)
    r0 = row0 // t
    in_specs = [
        pl.BlockSpec((t, hp * dk), lambda s, d, h, c: (r0 + rc(s, d, h, c), h)),
        pl.BlockSpec((t, hp * dk), lambda s, d, h, c: (r0 + rc(s, d, h, c), heads // hp + h)),
        pl.BlockSpec((t, hp * dv), lambda s, d, h, c: (r0 + rc(s, d, h, c), h)),
        pl.BlockSpec((None, hp, t, 2), lambda s, d, h, c: (d, h, rc(s, d, h, c), 0)),
        pl.BlockSpec((None, hp, None, 2, t), lambda s, d, h, c: (d, h, rc(s, d, h, c), 0, 0)),
    ]
    args = [qk, qk, v, gcol, grow]
    st_specs = [
        pl.BlockSpec((None, None, hp, dk, dv), lambda s, d, h, c: (s, d, h, 0, 0)),
        pl.BlockSpec((None, None, hp, 1, dk), lambda s, d, h, c: (s, d, h, 0, 0)),
        pl.BlockSpec((None, None, hp, 1, 1), lambda s, d, h, c: (s, d, h, 0, 0)),
    ]
    if init is not None:
        in_specs += st_specs
        args += list(init)
    out_specs = [pl.BlockSpec((None, t, hp * dv), lambda s, d, h, c: (d, rc(s, d, h, c), h))]
    out_shape = [jax.ShapeDtypeStruct((2, n, heads * dv), F32)]
    if emit_state:
        out_specs += st_specs
        out_shape += [jax.ShapeDtypeStruct((nseq, 2, heads, dk, dv), F32),
                      jax.ShapeDtypeStruct((nseq, 2, heads, 1, dk), F32),
                      jax.ShapeDtypeStruct((nseq, 2, heads, 1, 1), F32)]
    return pl.pallas_call(
        functools.partial(_mlstm_kernel, t=t, nc=nc, has_init=init is not None, emit_state=emit_state),
        grid=(nseq, 2, heads // hp, nc),
        in_specs=in_specs,
        out_specs=out_specs,
        out_shape=out_shape,
        scratch_shapes=[pltpu.VMEM((hp, dk, dv), F32), pltpu.VMEM((hp, 1, dk), F32), pltpu.VMEM((hp, 1, 1), F32)],
        compiler_params=_params(("parallel", "parallel", "parallel", "arbitrary")),
        name="mlstm_scan",
    )(*args)


def _mlstm_post_kernel(h_ref, g_ref, w_ref, o_ref):
    hs = h_ref[0] + h_ref[1]
    y = hs * lax.rsqrt(jnp.mean(hs * hs, axis=-1, keepdims=True) + NORM_EPS) * w_ref[...]
    o_ref[...] = (y * g_ref[...]).astype(o_ref.dtype)


def mlstm_post(h, o_gate, head_norm, *, row0, heads):
    _, n, width = h.shape
    dv = width // heads
    tm = _tile(n, 512, 8)
    return pl.pallas_call(
        _mlstm_post_kernel,
        grid=(n // tm, heads),
        in_specs=[pl.BlockSpec((2, tm, dv), lambda i, hh: (0, i, hh)),
                  pl.BlockSpec((tm, dv), lambda i, hh: (row0 // tm + i, hh)),
                  pl.BlockSpec((1, dv), lambda i, hh: (0, 0))],
        out_specs=pl.BlockSpec((tm, dv), lambda i, hh: (i, hh)),
        out_shape=jax.ShapeDtypeStruct((n, width), BF16),
        compiler_params=_params(("parallel", "parallel")),
        name="mlstm_post",
    )(h, o_gate, head_norm.reshape(1, dv))


def _ssd_kernel(*refs, t, nc, hpg, p, has_init, emit_state):
    refs = list(refs)
    x_ref, b_ref, c_ref, acol_ref, dtcol_ref, aneg_ref, vis_ref, vist_ref, neg_ref = refs[:9]
    pos = 9
    if has_init:
        s0_ref = refs[pos]
        pos += 1
    y_ref = refs[pos]
    pos += 1
    if emit_state:
        so_ref = refs[pos]
        pos += 1
    st = refs[pos]
    c = pl.program_id(3)
    width = hpg * p
    ns = st.shape[1]

    @pl.when(c == 0)
    def _():
        for gg in range(st.shape[0]):
            if has_init:
                st[gg] = s0_ref[gg].T
            else:
                st[gg] = jnp.zeros(st.shape[1:], F32)

    lane_head = lax.broadcasted_iota(jnp.int32, (1, width), 1) // p
    lane_pair = lax.broadcasted_iota(jnp.int32, (1, 2 * p), 1)
    vis_t = vist_ref[...]
    neg = neg_ref[...]
    for gg in range(st.shape[0]):
        lanes = slice(gg * width, (gg + 1) * width)
        bm = b_ref[:, gg * ns:(gg + 1) * ns]
        cmat = c_ref[:, gg * ns:(gg + 1) * ns]
        dt_e = jnp.zeros((t, width), F32)
        for e in range(hpg):
            dt_e = jnp.where(lane_head == e, dtcol_ref[gg, :, e:e + 1], dt_e)
        xd = x_ref[:, lanes] * dt_e
        xdb = xd.astype(BF16)
        a_e = dt_e * aneg_ref[:, lanes]
        cum_e = _masked_cumsum(vis_ref[...], a_e)
        tot_e = jnp.sum(a_e, axis=0, keepdims=True)
        cb = lax.dot_general(cmat, bm, _NT, preferred_element_type=F32)
        y_pairs = []
        for e in range(hpg):
            cum_col = cum_e[:, e * p:e * p + 1]
            cum_row = jnp.sum(vis_t * acol_ref[gg, :, e:e + 1], axis=0, keepdims=True)
            wmat = (cb * jnp.exp(cum_col - cum_row + neg)).astype(BF16)
            j = e // 2
            yp = _dot(wmat, xdb[:, j * 2 * p:(j + 1) * 2 * p])
            if e % 2 == 0:
                y_even = yp
            else:
                y_pairs.append(jnp.where(lane_pair < p, y_even, yp))
        s_prev = st[gg]
        y_off = _dot(cmat, s_prev.astype(BF16)) * jnp.exp(cum_e)
        y_ref[:, lanes] = jnp.concatenate(y_pairs, axis=1) + y_off
        xw = (xd * jnp.exp(tot_e - cum_e)).astype(BF16)
        s_new = jnp.exp(tot_e) * s_prev + lax.dot_general(bm, xw, _TN, preferred_element_type=F32)
        st[gg] = s_new
        if emit_state:
            @pl.when(c == nc - 1)
            def _(gg=gg, s_new=s_new):
                so_ref[gg] = s_new.T


def ssd_scan(x, bc, acol, dtcol, aneg, init, *, row0, nseq, length, groups, hpg, emit_state):
    n = nseq * length
    inner = x.shape[1]
    width = inner // groups
    p = width // hpg
    ns = bc.shape[1] // (2 * groups)
    t = _tile(length, SSD_CHUNK, 8)
    nc = length // t
    gp = 2 if groups % 2 == 0 else 1
    rc = lambda s, d, g, c: _chunk_row(s, d, c, nc)
    r0 = row0 // t
    in_specs = [
        pl.BlockSpec((t, gp * width), lambda s, d, g, c: (r0 + rc(s, d, g, c), g)),
        pl.BlockSpec((t, gp * ns), lambda s, d, g, c: (r0 + rc(s, d, g, c), g)),
        pl.BlockSpec((t, gp * ns), lambda s, d, g, c: (r0 + rc(s, d, g, c), groups // gp + g)),
        pl.BlockSpec((None, gp, t, hpg), lambda s, d, g, c: (d, g, rc(s, d, g, c), 0)),
        pl.BlockSpec((None, gp, t, hpg), lambda s, d, g, c: (d, g, rc(s, d, g, c), 0)),
        pl.BlockSpec((None, 1, gp * width), lambda s, d, g, c: (d, 0, g)),
    ] + _mask_specs(t, lambda s, d, g, c: d)
    args = [x, bc, bc, acol, dtcol, aneg, *_scan_masks(t)]
    st_spec = pl.BlockSpec((None, None, gp, width, ns), lambda s, d, g, c: (s, d, g, 0, 0))
    if init is not None:
        in_specs.append(st_spec)
        args.append(init)
    out_specs = [pl.BlockSpec((None, t, gp * width), lambda s, d, g, c: (d, rc(s, d, g, c), g))]
    out_shape = [jax.ShapeDtypeStruct((2, n, inner), F32)]
    if emit_state:
        out_specs.append(st_spec)
        out_shape.append(jax.ShapeDtypeStruct((nseq, 2, groups, width, ns), F32))
    return pl.pallas_call(
        functools.partial(_ssd_kernel, t=t, nc=nc, hpg=hpg, p=p, has_init=init is not None, emit_state=emit_state),
        grid=(nseq, 2, groups // gp, nc),
        in_specs=in_specs,
        out_specs=out_specs,
        out_shape=out_shape,
        scratch_shapes=[pltpu.VMEM((gp, ns, width), F32)],
        compiler_params=_params(("parallel", "parallel", "parallel", "arbitrary")),
        name="ssd_scan",
    )(*args)


def _ssd_post_kernel(y_ref, x_ref, z_ref, dsk_ref, w_ref, o_ref):
    y = (y_ref[0] + y_ref[1] + x_ref[...] * dsk_ref[...]) * _silu(z_ref[...])
    o_ref[...] = (y * lax.rsqrt(jnp.mean(y * y, axis=-1, keepdims=True) + NORM_EPS) * w_ref[...]).astype(o_ref.dtype)


def ssd_post(y, x, z, d_skip_cols, norm_w, *, row0):
    _, n, inner = y.shape
    tm = _tile(n, 128, 8)
    row = pl.BlockSpec((tm, inner), lambda i: (row0 // tm + i, 0))
    vec = pl.BlockSpec((1, inner), lambda i: (0, 0))
    return pl.pallas_call(
        _ssd_post_kernel,
        grid=(n // tm,),
        in_specs=[pl.BlockSpec((2, tm, inner), lambda i: (0, i, 0)), row, row, vec, vec],
        out_specs=pl.BlockSpec((tm, inner), lambda i: (i, 0)),
        out_shape=jax.ShapeDtypeStruct((n, inner), BF16),
        compiler_params=_params(("parallel",)),
        name="ssd_post",
    )(y, x, z, d_skip_cols.reshape(1, inner), norm_w.reshape(1, inner))


def _hyena_kernel(u_ref, f_ref, g_ref, tp_ref, tq_ref, bias_ref, xm_ref, o_ref, *, length):
    u = u_ref[...]
    a = _dot(f_ref[...], u.astype(BF16))
    a_sw = jnp.concatenate([a[length:], a[:length]], axis=0)
    y = a * tp_ref[...] + a_sw * tq_ref[...]
    conv = _dot(g_ref[...], y.astype(BF16))
    o_ref[...] = ((conv + u * bias_ref[...]) * xm_ref[...]).astype(o_ref.dtype)


def hyena_conv(u, u_at, fmat, gmat, tp, tq, bias, xm, xm_at, *, nseq, length, d, out_dtype):
    tc = _tile(d, 256, LANES)
    n2 = 2 * length
    (u_r0, u_c0), (x_r0, x_c0) = u_at, xm_at
    return pl.pallas_call(
        functools.partial(_hyena_kernel, length=length),
        grid=(d // tc, nseq),
        in_specs=[
            pl.BlockSpec((length, tc), lambda j, s: (u_r0 // length + s, u_c0 // tc + j)),
            pl.BlockSpec((n2, length), lambda j, s: (0, 0)),
            pl.BlockSpec((length, n2), lambda j, s: (0, 0)),
            pl.BlockSpec((n2, tc), lambda j, s: (0, j)),
            pl.BlockSpec((n2, tc), lambda j, s: (0, j)),
            pl.BlockSpec((1, tc), lambda j, s: (0, j)),
            pl.BlockSpec((length, tc), lambda j, s: (x_r0 // length + s, x_c0 // tc + j)),
        ],
        out_specs=pl.BlockSpec((length, tc), lambda j, s: (s, j)),
        out_shape=jax.ShapeDtypeStruct((nseq * length, d), out_dtype),
        compiler_params=_params(("parallel", "parallel")),
        name="hyena_conv",
    )(u, fmat, gmat, tp, tq, bias.reshape(1, d), xm)


def _dft_matrices(length):
    n2 = 2 * length
    r = jnp.arange(n2, dtype=jnp.int32)
    kfreq = jnp.where(r <= length, r, r - length)
    ang = ((kfreq[:, None] * r[None, :]) % n2).astype(F32) * (2.0 * math.pi / n2)
    is_sin = (r > length)[:, None]
    fwd = jnp.where(is_sin, -jnp.sin(ang), jnp.cos(ang))[:, :length]
    bwd = jnp.where(is_sin, -fwd, fwd) * (jnp.arange(length) > 0)
    wk = jnp.where((r == 0) | (r == length), 1.0, 2.0) / n2
    inv = (fwd * wk[:, None]).T
    return jnp.concatenate([fwd, bwd], axis=1), fwd.astype(BF16), inv.astype(BF16)


def _hyena_filter_spectra(length, f_w1, f_b1, f_w2, f_b2, f_w3, f_freq, d):
    hp = lax.Precision.HIGHEST
    t = jnp.linspace(0.0, 1.0, length, dtype=F32)[:, None]
    bands = (HYENA_EMB - 1) // 2
    f = jnp.linspace(1e-4, bands - 1, bands, dtype=F32)
    w = 2.0 * math.pi * jnp.arange(length, dtype=F32)[:, None] / length
    z = jnp.concatenate([t, jnp.cos(f * w), -jnp.sin(f * w)], axis=-1)
    hid = jnp.sin(f_freq * (jnp.dot(z, f_w1, precision=hp) + f_b1))
    hid = jnp.sin(f_freq * (jnp.dot(hid, f_w2, precision=hp) + f_b2))
    filt = matmul(hid, f_w3, x3=True, tk=hid.shape[1], name="hyena_filter").reshape(length, 2, 2, d)
    min_decay = math.log(HYENA_TARGET) / HYENA_DECAY_LONG
    max_decay = math.log(HYENA_TARGET) / HYENA_DECAY_SHORT
    deltas = jnp.linspace(min_decay, max_decay, d, dtype=F32)
    filt = filt * jnp.exp(-t * jnp.abs(deltas))[:, None, None, :]
    both, fmat, gmat = _dft_matrices(length)
    tps, tqs = [], []
    for o in range(2):
        h_fwd, h_bwd = filt[:, o, 0], filt[:, o, 1]
        norm = jnp.sum(jnp.abs(h_fwd), axis=0) + jnp.sum(jnp.abs(h_bwd[1:]), axis=0)
        spec = matmul(both, jnp.concatenate([h_fwd, h_bwd], axis=0), x3=True, tm=512,
                      name="hyena_filter_dft") / norm
        top, bot = spec[:length], spec[length:]
        zero = jnp.zeros((1, d), F32)
        im = jnp.concatenate([zero, bot[1:]], axis=0)
        tps.append(jnp.concatenate([top, bot[0:1], top[1:]], axis=0))
        tqs.append(jnp.concatenate([-im, im], axis=0))
    return fmat, gmat, tps, tqs


def _router_kernel(l_ref, b_ref, gates_ref, dest_ref, cnt_ref, pre_s, *, n_grp, epg, tb, chunk):
    n_exp = n_grp * epg
    rows = l_ref.shape[1]
    lg = l_ref[...] + b_ref[...]
    g = lg[n_exp:n_exp + n_grp]
    ge = jnp.exp(g - jnp.max(g, axis=0, keepdims=True))
    gp = ge / jnp.sum(ge, axis=0, keepdims=True)
    gpm = jnp.max(gp, axis=0, keepdims=True)
    gi = lax.broadcasted_iota(jnp.int32, gp.shape, 0)
    gsel = jnp.min(jnp.where(gp == gpm, gi, n_grp), axis=0, keepdims=True)
    e_in = lg[0:epg]
    for q in range(1, n_grp):
        e_in = jnp.where(gsel == q, lg[q * epg:(q + 1) * epg], e_in)
    ee = jnp.exp(e_in - jnp.max(e_in, axis=0, keepdims=True))
    ep = ee / jnp.sum(ee, axis=0, keepdims=True)
    ei = lax.broadcasted_iota(jnp.int32, ep.shape, 0)
    p1 = jnp.max(ep, axis=0, keepdims=True)
    s1 = jnp.min(jnp.where(ep == p1, ei, epg), axis=0, keepdims=True)
    ep2 = jnp.where(ei == s1, -1.0, ep)
    p2 = jnp.max(ep2, axis=0, keepdims=True)
    s2 = jnp.min(jnp.where(ep2 == p2, ei, epg), axis=0, keepdims=True)
    den = p1 + p2
    gates_ref[0:1, :] = gpm * p1 / den
    gates_ref[1:2, :] = gpm * p2 / den

    eio = lax.broadcasted_iota(jnp.int32, (n_exp, rows), 0)
    oh0 = eio == gsel * epg + s1
    oh1 = eio == gsel * epg + s2
    cnt = oh0.astype(F32) + oh1.astype(F32)
    ca = lax.broadcasted_iota(jnp.int32, (chunk, chunk), 0)
    cb = lax.broadcasted_iota(jnp.int32, (chunk, chunk), 1)
    tri = (ca < cb).astype(BF16)
    carry = jnp.zeros((n_exp, 1), F32)
    for c0 in range(0, rows, chunk):
        c = cnt[:, c0:c0 + chunk]
        pre_s[:, c0:c0 + chunk] = _dot(c.astype(BF16), tri) + carry
        carry = carry + jnp.sum(c, axis=1, keepdims=True)
    nblk = jnp.floor((carry + (tb - 1)) * (1.0 / tb))
    sa = lax.broadcasted_iota(jnp.int32, (n_exp, n_exp), 0)
    sb = lax.broadcasted_iota(jnp.int32, (n_exp, n_exp), 1)
    nblk_row = jnp.sum(jnp.where(sa == sb, nblk, 0.0), axis=0, keepdims=True)
    blk_start = jnp.sum(jnp.where(sb < sa, nblk_row, 0.0), axis=1, keepdims=True)
    slot = blk_start * tb + pre_s[...]
    dest_ref[0:1, :] = jnp.sum(jnp.where(oh0, slot, 0.0), axis=0, keepdims=True).astype(jnp.int32)
    dest_ref[1:2, :] = jnp.sum(jnp.where(oh1, slot, 0.0), axis=0, keepdims=True).astype(jnp.int32)
    cnt_ref[...] = carry.astype(jnp.int32)


def moe_route(logits_t, bias, *, n_grp, epg, tb):
    nr, rows = logits_t.shape
    n_exp = n_grp * epg
    assert tb & (tb - 1) == 0
    return pl.pallas_call(
        functools.partial(_router_kernel, n_grp=n_grp, epg=epg, tb=tb, chunk=_tile(rows, 512, LANES)),
        out_shape=[jax.ShapeDtypeStruct((MOE_TOP_K, rows), F32),
                   jax.ShapeDtypeStruct((MOE_TOP_K, rows), jnp.int32),
                   jax.ShapeDtypeStruct((n_exp, 1), jnp.int32)],
        scratch_shapes=[pltpu.VMEM((n_exp, rows), F32)],
        compiler_params=pltpu.CompilerParams(vmem_limit_bytes=VMEM_LIMIT),
        name="moe_route",
    )(logits_t, bias)


def _dispatch_kernel(dest_ref, cnt_ref, start_ref, nu_ref, h_ref, xs_ref, zero_s, sem, zsem, *,
                     rows, tm, tb, n_exp, n_blocks, slab):
    i = pl.program_id(0)

    def zero_copy(row):
        return pltpu.make_async_copy(zero_s, _slab_at(xs_ref, row, tb, slab), zsem)

    @pl.when(i == 0)
    def _():
        zero_s[...] = jnp.zeros_like(zero_s)
        for wait in (False, True):
            def go(row):
                if wait:
                    zero_copy(row).wait()
                else:
                    zero_copy(row).start()

            def last_block(e, carry):
                @pl.when(cnt_ref[e] > 0)
                def _():
                    go(start_ref[e] + ((cnt_ref[e] - 1) & ~(tb - 1)))
                return carry

            def tail_block(b, carry):
                go(b * tb)
                return carry

            lax.fori_loop(0, n_exp, last_block, 0)
            lax.fori_loop(nu_ref[0], n_blocks, tail_block, 0)

    def row_copy(r, slot):
        return pltpu.make_async_copy(_slab_at(h_ref, r, 1, slab), _slab_at(xs_ref, slot, 1, slab), sem)

    def issue(r, carry):
        row_copy(r, dest_ref[i * tm + r]).start(priority=0)
        row_copy(r, dest_ref[rows + i * tm + r]).start(priority=1)
        return carry

    def drain(r, carry):
        row_copy(0, 0).wait()
        row_copy(0, 0).wait()
        return carry

    lax.fori_loop(0, tm, issue, 0, unroll=8)
    lax.fori_loop(0, tm, drain, 0, unroll=8)


def _combine_kernel(dest_ref, x_ref, g2_ref, gt_ref, ys_ref, o_ref, buf, sem, *, rows, tm, slab):
    base = pl.program_id(0) * tm

    def row_copy(k, r, slot):
        return pltpu.make_async_copy(_slab_at(ys_ref, slot, 1, slab), _slab_at(buf.at[k], r, 1, slab), sem)

    def issue(r, carry):
        row_copy(0, r, dest_ref[base + r]).start(priority=0)
        row_copy(1, r, dest_ref[rows + base + r]).start(priority=1)
        return carry

    def drain(r, carry):
        row_copy(0, 0, 0).wait()
        row_copy(0, 0, 0).wait()
        return carry

    lax.fori_loop(0, tm, issue, 0, unroll=8)
    lax.fori_loop(0, tm, drain, 0, unroll=8)
    g0 = gt_ref[:, 0:1]
    g1 = gt_ref[:, 1:2]
    half = slab * LANES
    for s in range(slab):
        p0 = buf[0, pl.ds(s, tm, stride=slab), :]
        p1 = buf[1, pl.ds(s, tm, stride=slab), :]
        for unpack, c0 in ((_unpack_hi, s * LANES), (_unpack_lo, half + s * LANES)):
            cols = slice(c0, c0 + LANES)
            o_ref[:, cols] = x_ref[:, cols] + g2_ref[:, cols] * (g0 * unpack(p0) + g1 * unpack(p1))


def _moe_up_kernel(be_ref, nu_ref, x_ref, wg_ref, wu_ref, o_ref, *, fchunk, slab):
    used = pl.program_id(0) < nu_ref[0]

    @pl.when(used)
    def _():
        p = _load_slab(x_ref, o_ref.shape[0], slab)
        x = jnp.concatenate([_unpack_hi(p), _unpack_lo(p)], axis=1).astype(BF16)
        for f0 in range(0, o_ref.shape[1], fchunk):
            g = _dot(x, wg_ref[:, f0:f0 + fchunk].astype(BF16))
            u = _dot(x, wu_ref[:, f0:f0 + fchunk].astype(BF16))
            o_ref[:, f0:f0 + fchunk] = (_silu(g) * u).astype(o_ref.dtype)

    @pl.when(jnp.logical_not(used))
    def _():
        o_ref[...] = jnp.zeros_like(o_ref)


def _moe_down_kernel(be_ref, nu_ref, h_ref, wd_ref, o_ref, *, slab):
    used = pl.program_id(0) < nu_ref[0]

    @pl.when(used)
    def _():
        _store_slab(o_ref, _pack_bf16_pairs(_dot(h_ref[...], wd_ref[...].astype(BF16))), slab)

    @pl.when(jnp.logical_not(used))
    def _():
        o_ref[...] = jnp.zeros_like(o_ref)


def moe_experts(x_slots, block_e, n_used, w_gate, w_up, w_down, layer):
    d, ff = w_gate.shape[-2:]
    slab = d // 2 // LANES
    n_slots = x_slots.shape[0] // slab
    tb = MOE_ROW_BLOCK
    n_blocks = n_slots // tb
    blk = lambda b, be, nu: (jnp.minimum(b, nu[0] - 1), 0)
    out_blk = lambda b, be, nu: (b, 0)
    wsel = lambda b, be, nu: (layer, be[b], 0, 0)
    hmid = pl.pallas_call(
        functools.partial(_moe_up_kernel, fchunk=_tile(ff, 256, LANES), slab=slab),
        grid_spec=pltpu.PrefetchScalarGridSpec(
            num_scalar_prefetch=2, grid=(n_blocks,),
            in_specs=[pl.BlockSpec((tb * slab, LANES), blk),
                      pl.BlockSpec((None, None, d, ff), wsel),
                      pl.BlockSpec((None, None, d, ff), wsel)],
            out_specs=pl.BlockSpec((tb, ff), out_blk)),
        out_shape=jax.ShapeDtypeStruct((n_slots, ff), BF16),
        compiler_params=_params(("arbitrary",)),
        name="moe_gate_up",
    )(block_e, n_used, x_slots, w_gate, w_up)
    return pl.pallas_call(
        functools.partial(_moe_down_kernel, slab=slab),
        grid_spec=pltpu.PrefetchScalarGridSpec(
            num_scalar_prefetch=2, grid=(n_blocks,),
            in_specs=[pl.BlockSpec((tb, ff), blk),
                      pl.BlockSpec((None, None, ff, d), wsel)],
            out_specs=pl.BlockSpec((tb * slab, LANES), out_blk)),
        out_shape=jax.ShapeDtypeStruct((n_slots * slab, LANES), U32),
        compiler_params=_params(("arbitrary",)),
        name="moe_down",
    )(block_e, n_used, hmid, w_down)


def moe_router_weights(w_rg, b_rg, w_re, b_re, layer):
    w = jnp.concatenate([w_re[layer], w_rg[layer]], axis=-1).T
    b = jnp.concatenate([b_re[layer], b_rg[layer]])[:, None]
    pad = -w.shape[0] % 8
    return jnp.pad(w, ((0, pad), (0, 0))), jnp.pad(b, ((0, pad), (0, 0)))


def hier_moe_residual(x, h, logits_t, bias, g2, row_map, w_gate, w_up, w_down, layer, *, n_grp, tm):
    rows, d = x.shape
    slab = d // 2 // LANES
    nsteps = rows // tm
    n_exp = w_gate.shape[1]
    tb = MOE_ROW_BLOCK
    n_pair = rows * MOE_TOP_K
    n_blocks = -(-(n_pair + n_exp * (tb - 1)) // tb)
    gates_t, dest_t, counts = moe_route(logits_t, bias, n_grp=n_grp, epg=n_exp // n_grp, tb=tb)
    counts = counts[:, 0]
    blocks = (counts + tb - 1) // tb
    blk_end = jnp.cumsum(blocks)
    starts = ((blk_end - blocks) * tb).astype(jnp.int32)
    n_used = blk_end[-1].astype(jnp.int32)
    block_e = jnp.minimum(jnp.searchsorted(blk_end, jnp.arange(n_blocks), side="right"), n_exp - 1)
    block_e = jnp.where(jnp.arange(n_blocks) < n_used, block_e, block_e[n_used - 1]).astype(jnp.int32)
    dest = dest_t.reshape(-1)

    x_slots = pl.pallas_call(
        functools.partial(_dispatch_kernel, rows=rows, tm=tm, tb=tb, n_exp=n_exp, n_blocks=n_blocks, slab=slab),
        grid_spec=pltpu.PrefetchScalarGridSpec(
            num_scalar_prefetch=4, grid=(nsteps,),
            in_specs=[pl.BlockSpec((tm * slab, LANES), lambda i, *_: (i, 0))],
            out_specs=pl.BlockSpec(memory_space=pl.ANY),
            scratch_shapes=[pltpu.VMEM((tb * slab, LANES), U32), pltpu.SemaphoreType.DMA,
                            pltpu.SemaphoreType.DMA]),
        out_shape=jax.ShapeDtypeStruct((n_blocks * tb * slab, LANES), U32),
        compiler_params=_params(("arbitrary",)),
        name="moe_dispatch",
    )(dest, counts, starts, n_used.reshape(1), h)
    y_slots = moe_experts(x_slots, block_e, n_used.reshape(1), w_gate, w_up, w_down, layer)
    return pl.pallas_call(
        functools.partial(_combine_kernel, rows=rows, tm=tm, slab=slab),
        grid_spec=pltpu.PrefetchScalarGridSpec(
            num_scalar_prefetch=1, grid=(nsteps,),
            in_specs=[pl.BlockSpec((tm, d), lambda i, *_: (i, 0)),
                      pl.BlockSpec((None, 1, d), lambda i, *_: (row_map(i * tm), 0, 0)),
                      pl.BlockSpec((tm, MOE_TOP_K), lambda i, *_: (i, 0)),
                      pl.BlockSpec(memory_space=pl.ANY)],
            out_specs=pl.BlockSpec((tm, d), lambda i, *_: (i, 0)),
            scratch_shapes=[pltpu.VMEM((MOE_TOP_K, tm * slab, LANES), U32), pltpu.SemaphoreType.DMA]),
        out_shape=jax.ShapeDtypeStruct((rows, d), F32),
        compiler_params=_params(("arbitrary",)),
        name="moe_combine",
    )(dest, x, g2, gates_t.T, y_slots)


def _rope_lane_tables(length, hd, n_rows):
    grid_rows = length // ROPE_GRID_W
    row = jnp.broadcast_to(jnp.arange(grid_rows, dtype=F32)[:, None], (grid_rows, ROPE_GRID_W)).reshape(length)
    col = jnp.broadcast_to(jnp.arange(ROPE_GRID_W, dtype=F32)[None, :], (grid_rows, ROPE_GRID_W)).reshape(length)
    axis_dim = hd // 2
    inv_freq = ROPE_BASE ** (-jnp.arange(0, axis_dim, 2, dtype=F32) / axis_dim)
    ang_r = row[:, None] * inv_freq
    ang_c = col[:, None] * inv_freq
    cos = jnp.concatenate([jnp.cos(ang_r)] * 2 + [jnp.cos(ang_c)] * 2, axis=1)
    sin = jnp.concatenate([-jnp.sin(ang_r), jnp.sin(ang_r), -jnp.sin(ang_c), jnp.sin(ang_c)], axis=1)
    reps = (n_rows // length, 2 * LANES // (2 * hd))
    return jnp.tile(cos, reps), jnp.tile(sin, reps)


def kernel(x_prompt, x_sample, cache_attn_k, cache_attn_v, state_mlstm_C, state_mlstm_n, state_mlstm_m, state_ssd, c, c_ctx, norm_mix, norm_ffn, w_mod, b_mod, attn_w_in, attn_q_norm, attn_k_norm, attn_lambda, attn_sub_norm, attn_w_out, mlstm_w_in, mlstm_conv_w, mlstm_conv_b, mlstm_w_gate, mlstm_b_gate, mlstm_head_norm, mlstm_w_out, ssd_w_in, ssd_conv_w, ssd_conv_b, ssd_dt_bias, ssd_a_log, ssd_d_skip, ssd_norm, ssd_w_out, hyena_w_in, hyena_conv_w, hyena_conv_b, hyena_f_w1, hyena_f_b1, hyena_f_w2, hyena_f_b2, hyena_f_w3, hyena_f_freq, hyena_skip_bias, hyena_w_out, moe_w_group, moe_b_group, moe_w_expert, moe_b_expert, moe_w_gate, moe_w_up, moe_w_down):
    batch, seq, dm = x_prompt.shape
    dec_batch, dec_seq, _ = x_sample.shape
    depth = norm_mix.shape[0]
    n_ctx = batch * seq
    n_lat = dec_batch * dec_seq
    rows = n_ctx + n_lat
    past = cache_attn_k.shape[2]
    a_heads, a_hd = cache_attn_k.shape[3], cache_attn_k.shape[5]
    m_heads, m_dk, m_dv = state_mlstm_C.shape[3:]
    s_heads, s_p, s_n = state_ssd.shape[3:]
    s_inner = s_heads * s_p
    s_groups = (ssd_conv_w.shape[-1] - s_inner) // (2 * s_n)
    s_hpg = s_heads // s_groups

    def row_map(r):
        return jnp.where(r < n_ctx, 0, 1 + (r - n_ctx) // dec_seq)

    row_unit = math.gcd(n_ctx, dec_seq)
    tm_norm = _tile(row_unit, 256, 8)
    tm_res = _tile(row_unit, 1024, 8)
    seq_lcm = math.lcm(seq, dec_seq)
    pass_unit = math.gcd(n_ctx, n_lat)
    assert pass_unit % seq_lcm == 0 and seq & (seq - 1) == 0 and dec_seq & (dec_seq - 1) == 0
    tm_seq = seq_lcm * max(k for k in range(1, max(2, 2048 // seq_lcm + 1)) if pass_unit % (seq_lcm * k) == 0)
    seq_of_tile = lambda i: jnp.where(i * tm_seq < n_ctx, seq, dec_seq)
    parts = ((0, batch, seq), (n_ctx, dec_batch, dec_seq))
    ones = lambda n: jnp.ones((n,), F32)

    tn_of = lambda n: _tile(n, 512, LANES)

    def conv_extras(taps, bias, scale, col0):
        tn = tn_of(scale.shape[0])
        assert col0 % tn == 0
        return [(taps, (3, tn), lambda i, j, c=col0 // tn: (0, j + c)), _row_vec(bias, tn, col0), _row_vec(scale, tn)]

    n_cond = 1 + dec_batch
    cond = jnp.concatenate([c_ctx[None, :], c], axis=0)
    cond = jnp.pad(jax.nn.silu(cond), ((0, -n_cond % 8), (0, 0))).astype(BF16)

    x = jnp.concatenate([x_prompt.reshape(n_ctx, dm), x_sample.reshape(n_lat, dm)], axis=0)
    outs = {}
    for i in range(depth):
        kind, j = i % 4, i // 4
        mod = matmul(cond, w_mod, w_idx=i, tm=8, tn=1024, tk=2048, extras=[_row_vec(b_mod[i], 1024)],
                     epi=_epi_bias, name="adaln_mod")
        sh1, sc1, g1, sh2, sc2, g2 = [mod[:, None, q * dm:(q + 1) * dm] for q in range(6)]
        h1 = norm_mod(x, norm_mix[i], sc1, sh1, row_map, tm_norm)

        if kind == 0:
            lam_init = 0.8 - 0.6 * math.exp(-0.3 * i)
            tn = tn_of(dm)
            cos, sin = _rope_lane_tables(dec_seq, a_hd, tm_seq)
            tabs = [(cos, (tm_seq, LANES), lambda i_, j_: (0, 0)), (sin, (tm_seq, LANES), lambda i_, j_: (0, 0))]
            rep = dm // a_hd
            q = matmul(h1, attn_w_in, w_idx=j, n=dm, tm=tm_seq, tn=tn, out_dtypes=(BF16,),
                       extras=[_row_vec(jnp.tile(attn_q_norm[j], rep), tn)] + tabs,
                       epi=_make_epi_qk(tm_seq, tn, a_hd, n_ctx, 1), name="attn_in_q")
            k, k_plain = matmul(h1, attn_w_in, w_idx=j, n0=dm, n=dm, tm=tm_seq, tn=tn, out_dtypes=(BF16, F32),
                                extras=[_row_vec(jnp.tile(attn_k_norm[j], rep), tn)] + tabs,
                                epi=_make_epi_qk(tm_seq, tn, a_hd, n_ctx, 2), name="attn_in_k")
            v, v_f32 = matmul(h1, attn_w_in, w_idx=j, n0=2 * dm, n=dm, tm=tm_seq, tn=tn, out_dtypes=(BF16, F32),
                              epi=_epi_two, name="attn_in_v")
            outs["k"] = k_plain[:n_ctx].reshape(batch, 1, seq, a_heads, 2, a_hd)
            outs["v"] = v_f32[:n_ctx].reshape(batch, 1, seq, a_heads, 2 * a_hd)
            lv = attn_lambda[j]
            lam = jnp.exp(jnp.sum(lv[0] * lv[1])) - jnp.exp(jnp.sum(lv[2] * lv[3])) + lam_init
            cache = (cache_attn_k[:, j].reshape(dec_batch * past, dm), cache_attn_v[:, j].reshape(dec_batch * past, dm))
            o_parts = [diff_attention(q, k, v, cache if row0 else None, lam, attn_sub_norm[j], row0=row0, nseq=nseq,
                                      length=length, heads=a_heads, post_scale=1.0 - lam_init)
                       for row0, nseq, length in parts]
            mix_in, w_out = jnp.concatenate(o_parts, axis=0), attn_w_out

        elif kind == 1:
            qk_w = 2 * m_heads * m_dk
            v_w = m_heads * m_dv
            k_scale = jnp.concatenate([ones(qk_w // 2), jnp.full((qk_w // 2,), m_dk ** -0.5, F32)])
            qk = matmul(h1, mlstm_w_in, w_idx=j, n=qk_w, tm=tm_seq, out_dtypes=(BF16,),
                        extras=conv_extras(mlstm_conv_w[j], mlstm_conv_b[j], k_scale, 0),
                        epi=_make_epi_conv(tm_seq, seq_of_tile, True), name="mlstm_in_qk")
            v = matmul(h1, mlstm_w_in, w_idx=j, n0=qk_w, n=v_w, out_dtypes=(BF16,), name="mlstm_in_v")
            o_gate = matmul(h1, mlstm_w_in, w_idx=j, n0=qk_w + v_w, n=v_w, epi=_epi_sigmoid, name="mlstm_in_o")
            g = matmul(h1, mlstm_w_gate, w_idx=j, x3=True, tm=512, extras=[_row_vec(mlstm_b_gate[j], 4 * m_heads)],
                       epi=_epi_bias, name="mlstm_gate")
            g = g.reshape(rows, 2, 2, m_heads)
            gates = jnp.stack([g[:, :, 0], jax.nn.log_sigmoid(g[:, :, 1])], axis=-1)
            gcol = gates.transpose(1, 2, 0, 3)
            init = (state_mlstm_C[:, j], state_mlstm_n[:, j][:, :, :, None, :], state_mlstm_m[:, j][:, :, :, None, None])
            y_parts = []
            for row0, nseq, length in parts:
                t = _tile(length, MLSTM_CHUNK, 8)
                n = nseq * length
                gc = gcol[:, :, row0:row0 + n]
                grow = gc.reshape(2, m_heads, n // t, t, 2).transpose(0, 1, 2, 4, 3)
                res = mlstm_scan(qk, v, gc, grow, init if row0 else None, row0=row0, nseq=nseq, length=length,
                                 heads=m_heads, emit_state=not row0)
                if not row0:
                    outs["C"] = res[1][:, None]
                    outs["n"] = res[2][:, None, :, :, 0]
                    outs["m"] = res[3][:, None, :, :, 0, 0]
                y_parts.append(mlstm_post(res[0], o_gate, mlstm_head_norm[j], row0=row0, heads=m_heads))
            mix_in, w_out = jnp.concatenate(y_parts, axis=0), mlstm_w_out

        elif kind == 2:
            gn = s_groups * s_n
            conv = _make_epi_conv(tm_seq, seq_of_tile, True)
            z = matmul(h1, ssd_w_in, w_idx=j, n=s_inner, name="ssd_in_z")
            xs = matmul(h1, ssd_w_in, w_idx=j, n0=s_inner, n=s_inner, tm=tm_seq,
                        extras=conv_extras(ssd_conv_w[j], ssd_conv_b[j], ones(s_inner), 0),
                        epi=conv, name="ssd_in_x")
            bc = matmul(h1, ssd_w_in, w_idx=j, n0=2 * s_inner, n=2 * gn, tm=tm_seq, out_dtypes=(BF16,),
                        extras=conv_extras(ssd_conv_w[j], ssd_conv_b[j], ones(2 * gn), s_inner),
                        epi=conv, name="ssd_in_bc")
            dt_raw = matmul(h1, ssd_w_in, w_idx=j, n0=2 * s_inner + 2 * gn, n=2 * s_heads, x3=True, tm=512, name="ssd_dt")
            dt = jax.nn.softplus(dt_raw.reshape(rows, 2, s_heads) + ssd_dt_bias[j])
            a_neg = -jnp.exp(ssd_a_log[j])
            a = dt * a_neg
            to_col = lambda y: y.reshape(rows, 2, s_groups, s_hpg).transpose(1, 2, 0, 3)
            acol, dtcol = to_col(a), to_col(dt)
            aneg_lanes = jnp.repeat(a_neg, s_p, axis=1)[:, None, :]
            init = state_ssd[:, j].reshape(dec_batch, 2, s_groups, s_hpg * s_p, s_n)
            d_skip_cols = jnp.repeat(ssd_d_skip[j], s_p)
            y_parts = []
            for row0, nseq, length in parts:
                n = nseq * length
                res = ssd_scan(xs, bc, acol[:, :, row0:row0 + n], dtcol[:, :, row0:row0 + n], aneg_lanes,
                               init if row0 else None, row0=row0,
                               nseq=nseq, length=length, groups=s_groups, hpg=s_hpg, emit_state=not row0)
                if not row0:
                    outs["ssd"] = res[1].reshape(batch, 1, 2, s_heads, s_p, s_n)
                y_parts.append(ssd_post(res[0], xs, z, d_skip_cols, ssd_norm[j], row0=row0))
            mix_in, w_out = jnp.concatenate(y_parts, axis=0), ssd_w_out

        else:
            proj = matmul(h1, hyena_w_in, w_idx=j, tm=tm_seq,
                          extras=conv_extras(hyena_conv_w[j], hyena_conv_b[j], ones(3 * dm), 0),
                          epi=_make_epi_conv(tm_seq, seq_of_tile, False), name="hyena_in")
            y_parts = []
            for row0, nseq, length in parts:
                fmat, gmat, tps, tqs = _hyena_filter_spectra(length, hyena_f_w1[j], hyena_f_b1[j], hyena_f_w2[j],
                                                             hyena_f_b2[j], hyena_f_w3[j], hyena_f_freq[j], dm)
                y = hyena_conv(proj, (row0, 0), fmat, gmat, tps[0], tqs[0], hyena_skip_bias[j][0], proj, (row0, dm),
                               nseq=nseq, length=length, d=dm, out_dtype=F32)
                y = hyena_conv(y, (0, 0), fmat, gmat, tps[1], tqs[1], hyena_skip_bias[j][1], proj, (row0, 2 * dm),
                               nseq=nseq, length=length, d=dm, out_dtype=BF16)
                y_parts.append(y)
            mix_in, w_out = jnp.concatenate(y_parts, axis=0), hyena_w_out

        tn = _tile(dm, 1024, LANES)
        x = matmul(mix_in, w_out[j].astype(BF16), tm=tm_res, tn=tn,
                   extras=[(x, (tm_res, tn), lambda i_, j_: (i_, j_)),
                           (g1, (None, 1, tn), lambda i_, j_: (row_map(i_ * tm_res), 0, j_))],
                   epi=_epi_resid, name="mixer_out")
        w_rt, b_rt = moe_router_weights(moe_w_group, moe_b_group, moe_w_expert, moe_b_expert, i)
        h2, logits_t = norm_mod_router(x, norm_ffn[i], sc2, sh2, w_rt, row_map, tm_norm)
        x = hier_moe_residual(x, h2, logits_t, b_rt, g2, row_map, moe_w_gate, moe_w_up, moe_w_down, i,
                              n_grp=moe_w_group.shape[-1], tm=tm_norm)

    y_prompt = x[:n_ctx].reshape(batch, seq, dm)
    y_sample = x[n_ctx:].reshape(dec_batch, dec_seq, dm)
    return (y_prompt, y_sample, outs["k"], outs["v"], outs["C"], outs["n"], outs["m"], outs["ssd"])
```
